```python
import math
import jax, jax.numpy as jnp
from jax import lax
import numpy as np

D_MODEL = 2048
BATCH = 16
SEQ = 256
DEPTH = 2
DEC_BATCH = 2
DEC_SEQ = 2048
PAST_LEN = 512

GRID_W = 64
HEAD_DIM = 128
CONV_DIM = 1024
CONV_K = 3
WIN_HEADS = 8
WIN_KV_HEADS = 2
WINDOW = 128
BLOCK = 128
DIFF_HEADS = 8
DIFF_QK_DIM = 64
DIFF_V_DIM = 128
D_FF = 4 * D_MODEL
N_BRANCH = 3
EPS = 1e-6
ROPE_BASE = 10000.0
NEG = -1e30
IN_SIZES = (CONV_DIM, CONV_DIM, CONV_DIM,
            WIN_HEADS * HEAD_DIM, WIN_KV_HEADS * HEAD_DIM, WIN_KV_HEADS * HEAD_DIM,
            DIFF_HEADS * 2 * DIFF_QK_DIM, DIFF_HEADS * 2 * DIFF_QK_DIM, DIFF_HEADS * DIFF_V_DIM,
            N_BRANCH * D_MODEL)
IN_TOTAL = sum(IN_SIZES)

kernel_name = "hybrid_conv_window_diff_dit_step"


def rmsnorm(x, g):
    xf = x.astype(jnp.float32)
    y = xf * lax.rsqrt(jnp.mean(xf * xf, axis=-1, keepdims=True) + EPS)
    return (y * g.astype(jnp.float32)).astype(x.dtype)


def rope1d(x, pos):
    half = x.shape[-1] // 2
    inv = ROPE_BASE ** (-jnp.arange(half, dtype=jnp.float32) / half)
    ang = pos.astype(jnp.float32)[:, None] * inv[None, :]
    shape = (1, x.shape[1]) + (1,) * (x.ndim - 3) + (half,)
    cos = jnp.cos(ang).reshape(shape).astype(x.dtype)
    sin = jnp.sin(ang).reshape(shape).astype(x.dtype)
    x1, x2 = x[..., :half], x[..., half:]
    return jnp.concatenate([x1 * cos - x2 * sin, x1 * sin + x2 * cos], axis=-1)


def rope2d(x):
    T = x.shape[1]
    n_rows = T // GRID_W
    rows = jnp.repeat(jnp.arange(n_rows, dtype=jnp.int32), GRID_W)
    cols = jnp.tile(jnp.arange(GRID_W, dtype=jnp.int32), n_rows)
    h = x.shape[-1] // 2
    return jnp.concatenate([rope1d(x[..., :h], rows), rope1d(x[..., h:], cols)], axis=-1)


def short_conv(u, w):
    T = u.shape[1]
    pad = CONV_K // 2
    up = jnp.pad(u, ((0, 0), (pad, pad), (0, 0)))
    out = up[:, 0:T] * w[0]
    for j in range(1, CONV_K):
        out = out + up[:, j:j + T] * w[j]
    return out


def query_blocks(fn, q):
    B, T = q.shape[:2]
    nb = T // BLOCK
    qb = jnp.moveaxis(q.reshape((B, nb, BLOCK) + q.shape[2:]), 1, 0)
    out = lax.map(fn, qb)
    out = jnp.moveaxis(out, 0, 1)
    return out.reshape((B, T) + out.shape[3:])


def ctx_window_attn(q, k, v, sink):
    B, S, Hq, d = q.shape
    Hkv = k.shape[2]
    G = Hq // Hkv
    scale = d ** -0.5

    def blk(qb):
        qg = qb.reshape(B, BLOCK, Hkv, G, d)
        s = jnp.einsum('bqhgd,bkhd->bhgqk', qg, k).astype(jnp.float32) * scale
        s_sink = jnp.broadcast_to(sink.astype(jnp.float32).reshape(1, Hkv, G, 1, 1), s.shape[:-1] + (1,))
        p = jax.nn.softmax(jnp.concatenate([s_sink, s], axis=-1), axis=-1).astype(v.dtype)
        o = jnp.einsum('bhgqk,bkhd->bqhgd', p[..., 1:], v)
        return o.reshape(B, BLOCK, Hq * d)

    return query_blocks(blk, q)


def latent_window_attn(q, k, v, kc, vc, sink):
    B, T, Hq, d = q.shape
    Hkv = k.shape[2]
    G = Hq // Hkv
    nb = T // BLOCK
    P = kc.shape[1]
    scale = d ** -0.5
    qg = q.reshape(B, nb, BLOCK, Hkv, G, d)

    def bands(a):
        ap = jnp.pad(a, ((0, 0), (BLOCK, BLOCK), (0, 0), (0, 0)))
        parts = [ap[:, o:o + T].reshape(B, nb, BLOCK, Hkv, d) for o in (0, BLOCK, 2 * BLOCK)]
        return jnp.concatenate(parts, axis=2)

    kb, vb = bands(k), bands(v)
    qi = jnp.arange(BLOCK)[:, None]
    kj = jnp.arange(3 * BLOCK)[None, :] - BLOCK
    rel_ok = jnp.abs(qi - kj) <= WINDOW
    kpos = jnp.arange(nb)[:, None] * BLOCK + kj
    in_range = (kpos >= 0) & (kpos < T)
    mask = rel_ok[None] & in_range[:, None, :]
    s_loc = jnp.einsum('bnqhgd,bnkhd->bnhgqk', qg, kb).astype(jnp.float32) * scale
    s_loc = jnp.where(mask[None, :, None, None], s_loc, NEG)
    s_ctx = jnp.einsum('bnqhgd,bphd->bnhgqp', qg, kc).astype(jnp.float32) * scale
    s_sink = jnp.broadcast_to(sink.astype(jnp.float32).reshape(1, 1, Hkv, G, 1, 1), s_ctx.shape[:-1] + (1,))
    p = jax.nn.softmax(jnp.concatenate([s_sink, s_ctx, s_loc], axis=-1), axis=-1).astype(v.dtype)
    o = (jnp.einsum('bnhgqp,bphd->bnqhgd', p[..., 1:1 + P], vc)
         + jnp.einsum('bnhgqk,bnkhd->bnqhgd', p[..., 1 + P:], vb))
    return o.reshape(B, T, Hq * d)


def diff_attn(q, k, v, lam, subln_g, lam_init):
    B, T, H = q.shape[:3]
    scale = q.shape[-1] ** -0.5

    def blk(qb):
        s = jnp.einsum('bqhmd,bkhmd->bhmqk', qb, k).astype(jnp.float32) * scale
        p = jax.nn.softmax(s, axis=-1)
        a = (p[:, :, 0] - lam * p[:, :, 1]).astype(v.dtype)
        return jnp.einsum('bhqk,bkhd->bqhd', a, v)

    o = query_blocks(blk, q)
    o = rmsnorm(o, subln_g) * (1.0 - lam_init)
    return o.reshape(B, T, H * DIFF_V_DIM)


def trunk_layer(x, mod, l, W, cache):
    B, T, _ = x.shape
    shift1, scale1, gate1, shift2, scale2, gate2 = jnp.split(mod, 6, axis=-1)
    h = rmsnorm(x, W['norm_mix'][l]) * (1.0 + scale1) + shift1
    u = h @ W['w_in'][l]
    idx = []
    acc = 0
    for s in IN_SIZES[:-1]:
        acc += s
        idx.append(acc)
    cb, cc, cx, wq, wk, wv, dq, dk, dv, gts = jnp.split(u, idx, axis=-1)

    y_a = cb * short_conv(cc * cx, W['conv_w'][l])

    q = wq.reshape(B, T, WIN_HEADS, HEAD_DIM)
    k = wk.reshape(B, T, WIN_KV_HEADS, HEAD_DIM)
    v = wv.reshape(B, T, WIN_KV_HEADS, HEAD_DIM)
    q_d = dq.reshape(B, T, DIFF_HEADS, 2, DIFF_QK_DIM)
    k_d = dk.reshape(B, T, DIFF_HEADS, 2, DIFF_QK_DIM)
    v_d = dv.reshape(B, T, DIFF_HEADS, DIFF_V_DIM)
    lam_init = 0.8 - 0.6 * math.exp(-0.3 * l)
    lam = (jnp.exp(jnp.sum(W['lambda_q1'][l].astype(jnp.float32) * W['lambda_k1'][l].astype(jnp.float32)))
           - jnp.exp(jnp.sum(W['lambda_q2'][l].astype(jnp.float32) * W['lambda_k2'][l].astype(jnp.float32)))
           + lam_init)
    sink = W['win_sink'][l]

    if cache is None:
        y_b = ctx_window_attn(q, k, v, sink)
        y_c = diff_attn(q_d, k_d, v_d, lam, W['diff_subln'][l], lam_init)
        state = (k, v, k_d.reshape(B, T, DIFF_HEADS, 2 * DIFF_QK_DIM), v_d)
    else:
        kc_w, vc_w, kc_d, vc_d = cache
        P = kc_d.shape[1]
        y_b = latent_window_attn(rope2d(q), rope2d(k), v, kc_w, vc_w, sink)
        k_all = jnp.concatenate([kc_d.reshape(B, P, DIFF_HEADS, 2, DIFF_QK_DIM), rope2d(k_d)], axis=1)
        v_all = jnp.concatenate([vc_d, v_d], axis=1)
        y_c = diff_attn(rope2d(q_d), k_all, v_all, lam, W['diff_subln'][l], lam_init)
        state = None

    g_a, g_b, g_c = jnp.split(jax.nn.sigmoid(gts), N_BRANCH, axis=-1)
    merged = (g_a * (y_a @ W['w_branch_conv'][l])
              + g_b * (y_b @ W['w_branch_win'][l])
              + g_c * (y_c @ W['w_branch_diff'][l]))
    x = x + gate1 * (merged @ W['w_out'][l])

    h2 = rmsnorm(x, W['norm_mlp'][l]) * (1.0 + scale2) + shift2
    x = x + gate2 * (jnp.square(jax.nn.relu(h2 @ W['w_mlp1'][l])) @ W['w_mlp2'][l])
    return x, state


def setup_inputs(seed: int = 0) -> dict:
    key = jax.random.key(seed)
    ks = iter(jax.random.split(key, 40))

    def nrm(shape, scale):
        return jax.random.normal(next(ks), shape, jnp.float32) * scale

    D = D_MODEL
    return {
        'x_prompt': nrm((BATCH, SEQ, D), 1.0),
        'x_sample': nrm((DEC_BATCH, DEC_SEQ, D), 1.0),
        'cache_win_k': nrm((DEC_BATCH, DEPTH, PAST_LEN, WIN_KV_HEADS, HEAD_DIM), 1.0),
        'cache_win_v': nrm((DEC_BATCH, DEPTH, PAST_LEN, WIN_KV_HEADS, HEAD_DIM), 1.0),
        'cache_diff_k': nrm((DEC_BATCH, DEPTH, PAST_LEN, DIFF_HEADS, 2 * DIFF_QK_DIM), 1.0),
        'cache_diff_v': nrm((DEC_BATCH, DEPTH, PAST_LEN, DIFF_HEADS, DIFF_V_DIM), 1.0),
        'c': nrm((DEC_BATCH, D), 1.0),
        'c_ctx': nrm((D,), 1.0),
        'w_ada': nrm((DEPTH, D, 6 * D), 0.5 * D ** -0.5),
        'b_ada': nrm((DEPTH, 6 * D), 0.01),
        'norm_mix': 1.0 + nrm((DEPTH, D), 0.02),
        'norm_mlp': 1.0 + nrm((DEPTH, D), 0.02),
        'w_in': nrm((DEPTH, D, IN_TOTAL), D ** -0.5),
        'conv_w': nrm((DEPTH, CONV_K, CONV_DIM), CONV_K ** -0.5),
        'win_sink': nrm((DEPTH, WIN_HEADS), 0.5),
        'lambda_q1': nrm((DEPTH, DIFF_QK_DIM), 0.1),
        'lambda_k1': nrm((DEPTH, DIFF_QK_DIM), 0.1),
        'lambda_q2': nrm((DEPTH, DIFF_QK_DIM), 0.1),
        'lambda_k2': nrm((DEPTH, DIFF_QK_DIM), 0.1),
        'diff_subln': 1.0 + nrm((DEPTH, DIFF_V_DIM), 0.02),
        'w_branch_conv': nrm((DEPTH, CONV_DIM, D), CONV_DIM ** -0.5),
        'w_branch_win': nrm((DEPTH, WIN_HEADS * HEAD_DIM, D), (WIN_HEADS * HEAD_DIM) ** -0.5),
        'w_branch_diff': nrm((DEPTH, DIFF_HEADS * DIFF_V_DIM, D), (DIFF_HEADS * DIFF_V_DIM) ** -0.5),
        'w_out': nrm((DEPTH, D, D), D ** -0.5),
        'w_mlp1': nrm((DEPTH, D, D_FF), D ** -0.5),
        'w_mlp2': nrm((DEPTH, D_FF, D), D_FF ** -0.5),
        'norm_final': 1.0 + nrm((D,), 0.02),
    }


def reference(x_prompt, x_sample, cache_win_k, cache_win_v, cache_diff_k, cache_diff_v, c, c_ctx,
              w_ada, b_ada, norm_mix, norm_mlp, w_in, conv_w, win_sink,
              lambda_q1, lambda_k1, lambda_q2, lambda_k2, diff_subln,
              w_branch_conv, w_branch_win, w_branch_diff, w_out, w_mlp1, w_mlp2, norm_final):
    W = {'norm_mix': norm_mix, 'norm_mlp': norm_mlp, 'w_in': w_in, 'conv_w': conv_w,
         'win_sink': win_sink, 'lambda_q1': lambda_q1, 'lambda_k1': lambda_k1,
         'lambda_q2': lambda_q2, 'lambda_k2': lambda_k2, 'diff_subln': diff_subln,
         'w_branch_conv': w_branch_conv, 'w_branch_win': w_branch_win, 'w_branch_diff': w_branch_diff,
         'w_out': w_out, 'w_mlp1': w_mlp1, 'w_mlp2': w_mlp2}
    silu_ctx = jax.nn.silu(c_ctx)
    silu_c = jax.nn.silu(c)

    xp = x_prompt
    kw_l, vw_l, kd_l, vd_l = [], [], [], []
    for l in range(DEPTH):
        mod_ctx = silu_ctx @ w_ada[l] + b_ada[l]
        xp, st = trunk_layer(xp, mod_ctx, l, W, None)
        kw_l.append(st[0]); vw_l.append(st[1]); kd_l.append(st[2]); vd_l.append(st[3])
    y_prompt = rmsnorm(xp, norm_final)
    new_win_k = jnp.stack(kw_l, axis=1)
    new_win_v = jnp.stack(vw_l, axis=1)
    new_diff_k = jnp.stack(kd_l, axis=1)
    new_diff_v = jnp.stack(vd_l, axis=1)

    xs = x_sample
    for l in range(DEPTH):
        mod_lat = (silu_c @ w_ada[l] + b_ada[l])[:, None, :]
        cache = (cache_win_k[:, l], cache_win_v[:, l], cache_diff_k[:, l], cache_diff_v[:, l])
        xs, _ = trunk_layer(xs, mod_lat, l, W, cache)
    y_sample = rmsnorm(xs, norm_final)

    return (y_prompt, y_sample, new_win_k, new_win_v, new_diff_k, new_diff_v)
```

```python
import functools
import math

import jax
import jax.numpy as jnp
from jax import lax
from jax.experimental import pallas as pl
from jax.experimental.pallas import tpu as pltpu

D = 2048
N_CTX_SEQ = 16
CTX_LEN = 256
N_LAT_SEQ = 2
LAT_LEN = 2048
N_LAYERS = 2
PAST = 512
GRID_W = 64
HEAD = 128
CONV_DIM = 1024
WIN_HEADS = 8
WIN_KV = 2
WIN_GROUP = WIN_HEADS // WIN_KV
WINDOW = 128
QBLK = 128
DIFF_HEADS = 8
DIFF_QK = 64
D_FF = 4 * D
EPS = 1e-6
ROPE_BASE = 10000.0
NEG = -1e30

ROWS_CTX = N_CTX_SEQ * CTX_LEN
ROWS_LAT = N_LAT_SEQ * LAT_LEN
ROWS = ROWS_CTX + ROWS_LAT
N_MOD_ROWS = 8
N_MOD = 6

OFF_CONV = 0
OFF_ATTN = 3 * CONV_DIM
ATTN_COLS = (WIN_HEADS + 2 * WIN_KV) * HEAD + 3 * DIFF_HEADS * HEAD
OFF_GATE = OFF_ATTN + ATTN_COLS
IN_TOTAL = OFF_GATE + 3 * D
A_WQ, A_WK, A_WV = 0, WIN_HEADS * HEAD, (WIN_HEADS + WIN_KV) * HEAD
A_DQ = (WIN_HEADS + 2 * WIN_KV) * HEAD
A_DK = A_DQ + DIFF_HEADS * HEAD
A_DV = A_DK + DIFF_HEADS * HEAD
KV_COLS = 2 * WIN_KV * HEAD + 2 * DIFF_HEADS * HEAD

VMEM_LIMIT_V7X = 56 * 1024 * 1024


def _params(sem, vmem=VMEM_LIMIT_V7X):
    return pltpu.CompilerParams(dimension_semantics=sem, vmem_limit_bytes=vmem)


def _bf(x):
    return x.astype(jnp.bfloat16)


def _dot(a, b):
    return jnp.dot(a, b, preferred_element_type=jnp.float32)


def _dot_nt(a, b):
    return lax.dot_general(a, b, (((1,), (1,)), ((), ())), preferred_element_type=jnp.float32)


def _sigmoid(x):
    return 1.0 / (1.0 + jnp.exp(-x))


def _mod_group(i, tm):
    n_ctx = ROWS_CTX // tm
    return jnp.where(i < n_ctx, 0, 1 + (i - n_ctx) // (LAT_LEN // tm))


def _mod_spec(layer, which, tm):
    def imap(i, *_):
        return (layer * N_MOD_ROWS * N_MOD + _mod_group(i, tm) * N_MOD + which, 0, 0)
    return pl.BlockSpec((None, 1, D), imap)


def _vec_spec(layer, width):
    return pl.BlockSpec((None, 1, width), lambda *_: (layer, 0, 0))


def _rms_mod(x, g, scale, shift):
    y = x * lax.rsqrt(jnp.mean(x * x, axis=-1, keepdims=True) + EPS) * g
    return y * (1.0 + scale) + shift


def _mod_kernel(c_ref, w_ref, b_ref, o_ref):
    c = c_ref[...]
    s = c * _sigmoid(c)
    o_ref[...] = _dot(_bf(s), _bf(w_ref[...])) + b_ref[...]


def _modulation(cvec, w_ada, b_ada):
    tn = 1024
    n = N_MOD * D
    return pl.pallas_call(
        _mod_kernel,
        grid=(N_LAYERS, n // tn),
        in_specs=[pl.BlockSpec((N_MOD_ROWS, D), lambda l, j: (0, 0)),
                  pl.BlockSpec((None, D, tn), lambda l, j: (l, 0, j)),
                  pl.BlockSpec((None, 1, tn), lambda l, j: (l, 0, j))],
        out_specs=pl.BlockSpec((None, N_MOD_ROWS, tn), lambda l, j: (l, 0, j)),
        out_shape=jax.ShapeDtypeStruct((N_LAYERS, N_MOD_ROWS, n), jnp.float32),
        compiler_params=_params(("parallel", "parallel")),
        name="modulation",
    )(cvec, w_ada, b_ada.reshape(N_LAYERS, 1, n))


def _prologue_kernel(x_ref, g_ref, shift_ref, scale_ref, h_ref):
    h_ref[...] = _bf(_rms_mod(x_ref[...], g_ref[...], scale_ref[...], shift_ref[...]))


def _prologue(x, mods, norm_mix, layer):
    tm = 512
    return pl.pallas_call(
        _prologue_kernel,
        grid=(ROWS // tm,),
        in_specs=[pl.BlockSpec((tm, D), lambda i: (i, 0)),
                  _vec_spec(layer, D),
                  _mod_spec(layer, 0, tm),
                  _mod_spec(layer, 1, tm)],
        out_specs=pl.BlockSpec((tm, D), lambda i: (i, 0)),
        out_shape=jax.ShapeDtypeStruct((ROWS, D), jnp.bfloat16),
        compiler_params=_params(("parallel",)),
        name="prologue",
    )(x, norm_mix.reshape(N_LAYERS, 1, D), mods, mods)


CONV_TM = LAT_LEN
CONV_TN = 256


def _conv_kernel(h_ref, wb_ref, wc_ref, wx_ref, cw_ref, o_ref):
    i = pl.program_id(0)
    h = h_ref[...]
    cb = _dot(h, wb_ref[...])
    z = _dot(h, wc_ref[...]) * _dot(h, wx_ref[...])
    seq_mask = jnp.where(i < ROWS_CTX // CONV_TM, CTX_LEN - 1, LAT_LEN - 1)
    pos = lax.broadcasted_iota(jnp.int32, z.shape, 0) & seq_mask
    z_prev = jnp.where(pos == 0, 0.0, pltpu.roll(z, 1, 0))
    z_next = jnp.where(pos == seq_mask, 0.0, pltpu.roll(z, CONV_TM - 1, 0))
    cw = cw_ref[...]
    conv = z_prev * cw[0:1, :] + z * cw[1:2, :] + z_next * cw[2:3, :]
    o_ref[...] = _bf(cb * conv)


def _conv_branch(h, w_in, conv_w, layer):
    nj = CONV_DIM // CONV_TN

    def wspec(group):
        return pl.BlockSpec((None, D, CONV_TN), lambda i, j: (layer, 0, group * nj + j))

    return pl.pallas_call(
        _conv_kernel,
        grid=(ROWS // CONV_TM, nj),
        in_specs=[pl.BlockSpec((CONV_TM, D), lambda i, j: (i, 0)),
                  wspec(0), wspec(1), wspec(2),
                  pl.BlockSpec((None, 3, CONV_TN), lambda i, j: (layer, 0, j))],
        out_specs=pl.BlockSpec((CONV_TM, CONV_TN), lambda i, j: (i, j)),
        out_shape=jax.ShapeDtypeStruct((ROWS, CONV_DIM), jnp.bfloat16),
        compiler_params=_params(("parallel", "arbitrary")),
        name="conv_branch",
    )(h, w_in, w_in, w_in, conv_w)


ATTN_TM = 512
ATTN_CHUNK = 512
ROPE_WIN, ROPE_DIFF, ROPE_NONE = 0, 1, 2
_SLAB_KIND = ([ROPE_WIN] * (WIN_HEADS + WIN_KV) + [ROPE_NONE] * WIN_KV
              + [ROPE_DIFF] * (2 * DIFF_HEADS) + [ROPE_NONE] * DIFF_HEADS)
_ROPE_SHIFT = {ROPE_WIN: HEAD // 4, ROPE_DIFF: DIFF_QK // 4}
_STATE_SRC = ((A_WK, 2 * WIN_KV * HEAD), (A_DK, 2 * DIFF_HEADS * HEAD))


def _rope_tables(tm):
    t = jnp.arange(LAT_LEN, dtype=jnp.int32)
    rows = (t // GRID_W).astype(jnp.float32)[:, None]
    cols = (t % GRID_W).astype(jnp.float32)[:, None]
    lane = jnp.arange(HEAD, dtype=jnp.int32)[None, :]
    out = []
    for kind in (ROPE_WIN, ROPE_DIFF):
        s = _ROPE_SHIFT[kind]
        within = lane % (4 * s)
        freq = (within % s).astype(jnp.float32)
        inv = ROPE_BASE ** (-freq / s)
        pos = jnp.where(within < 2 * s, rows, cols)
        ang = pos * inv
        first = (within % (2 * s)) < s
        cos, sin = jnp.cos(ang), jnp.sin(ang)
        up = jnp.where(first, -sin, 0.0)
        down = jnp.where(first, 0.0, sin)
        ident = (jnp.ones((tm, HEAD), jnp.float32), jnp.zeros((tm, HEAD), jnp.float32),
                 jnp.zeros((tm, HEAD), jnp.float32))
        for tab, idn in zip((cos, up, down), ident):
            out.append(jnp.concatenate([idn, tab.astype(jnp.float32)], axis=0))
    return jnp.stack(out)


def _attn_proj_kernel(h_ref, w_ref, tab_ref, qkv_ref, state_ref):
    i = pl.program_id(0)
    h = h_ref[...]
    is_ctx = i < ROWS_CTX // ATTN_TM
    for c0 in range(0, ATTN_COLS, ATTN_CHUNK):
        acc = _dot(h, w_ref[:, c0:c0 + ATTN_CHUNK])
        for s0 in range(0, ATTN_CHUNK, HEAD):
            col = c0 + s0
            x = acc[:, s0:s0 + HEAD]
            kind = _SLAB_KIND[col // HEAD]
            for src, width in _STATE_SRC:
                if src <= col < src + width:
                    dst = col - src + (0 if src == A_WK else 2 * WIN_KV * HEAD)

                    @pl.when(is_ctx)
                    def _(x=x, dst=dst):
                        state_ref[:, dst:dst + HEAD] = x
            if kind != ROPE_NONE:
                s = _ROPE_SHIFT[kind]
                x = (x * tab_ref[3 * kind] + pltpu.roll(x, HEAD - s, 1) * tab_ref[3 * kind + 1]
                     + pltpu.roll(x, s, 1) * tab_ref[3 * kind + 2])
            qkv_ref[:, col:col + HEAD] = _bf(x)


def _attn_proj(h, w_attn, tables, layer):
    tm = ATTN_TM
    n_ctx = ROWS_CTX // tm
    per_lat = LAT_LEN // tm
    return pl.pallas_call(
        _attn_proj_kernel,
        grid=(ROWS // tm,),
        in_specs=[pl.BlockSpec((tm, D), lambda i: (i, 0)),
                  pl.BlockSpec((None, D, ATTN_COLS), lambda i: (layer, 0, 0),
                               pipeline_mode=pl.Buffered(1)),
                  pl.BlockSpec((6, tm, HEAD),
                               lambda i: (0, jnp.where(i < n_ctx, 0, 1 + (i - n_ctx) % per_lat), 0))],
        out_specs=[pl.BlockSpec((tm, ATTN_COLS), lambda i: (i, 0)),
                   pl.BlockSpec((tm, KV_COLS), lambda i: (jnp.minimum(i, n_ctx - 1), 0))],
        out_shape=[jax.ShapeDtypeStruct((ROWS, ATTN_COLS), jnp.bfloat16),
                   jax.ShapeDtypeStruct((ROWS_CTX, KV_COLS), jnp.float32)],
        compiler_params=_params(("arbitrary",)),
        name="attn_proj",
    )(h, w_attn, tables)


def _softmax_sink_pv(s, sink, v):
    m = jnp.maximum(jnp.max(s, axis=-1, keepdims=True), sink)
    p = jnp.exp(s - m)
    den = jnp.sum(p, axis=-1, keepdims=True) + jnp.exp(sink - m)
    return _dot(_bf(p), v) / den


def _ctx_win_kernel(layer, q_ref, k_ref, v_ref, sink_ref, o_ref):
    g = pl.program_id(1)
    k = k_ref[...]
    v = v_ref[...]
    for hh in range(WIN_GROUP):
        s = _dot_nt(q_ref[:, hh * HEAD:(hh + 1) * HEAD], k) * (HEAD ** -0.5)
        sink = sink_ref[layer * WIN_HEADS + g * WIN_GROUP + hh]
        o_ref[:, hh * HEAD:(hh + 1) * HEAD] = _bf(_softmax_sink_pv(s, sink, v))


def _ctx_win_attn(qkv, sink, layer):
    gw = WIN_GROUP * HEAD
    return pl.pallas_call(
        functools.partial(_ctx_win_kernel, layer),
        grid=(N_CTX_SEQ, WIN_KV),
        in_specs=[pl.BlockSpec((CTX_LEN, gw), lambda b, g: (b, A_WQ // gw + g)),
                  pl.BlockSpec((CTX_LEN, HEAD), lambda b, g: (b, A_WK // HEAD + g)),
                  pl.BlockSpec((CTX_LEN, HEAD), lambda b, g: (b, A_WV // HEAD + g)),
                  pl.BlockSpec(memory_space=pltpu.SMEM)],
        out_specs=pl.BlockSpec((CTX_LEN, gw), lambda b, g: (b, g)),
        out_shape=jax.ShapeDtypeStruct((ROWS_CTX, WIN_HEADS * HEAD), jnp.bfloat16),
        compiler_params=_params(("parallel", "parallel")),
        name="ctx_win_attn",
    )(qkv, qkv, qkv, sink)


def _lat_win_kernel(layer, q_ref, kc_ref, vc_ref, km_ref, k0_ref, kp_ref, vm_ref, v0_ref, vp_ref,
                    sink_ref, o_ref):
    n = pl.program_id(1)
    g = pl.program_id(2)
    kcat = jnp.concatenate([_bf(kc_ref[...]), km_ref[...], k0_ref[...], kp_ref[...]], axis=0)
    vcat = jnp.concatenate([_bf(vc_ref[...]), vm_ref[...], v0_ref[...], vp_ref[...]], axis=0)
    q4 = jnp.concatenate([q_ref[:, hh * HEAD:(hh + 1) * HEAD] for hh in range(WIN_GROUP)], axis=0)
    s = _dot_nt(q4, kcat) * (HEAD ** -0.5)
    col = lax.broadcasted_iota(jnp.int32, s.shape, 1)
    qi = lax.broadcasted_iota(jnp.int32, s.shape, 0) & (QBLK - 1)
    kj = col - (PAST + QBLK)
    kpos = n * QBLK + kj
    ok = (col < PAST) | ((jnp.abs(qi - kj) <= WINDOW) & (kpos >= 0) & (kpos < LAT_LEN))
    s = jnp.where(ok, s, NEG)
    sink = jnp.concatenate(
        [jnp.full((QBLK, 1), sink_ref[layer * WIN_HEADS + g * WIN_GROUP + hh], jnp.float32)
         for hh in range(WIN_GROUP)], axis=0)
    o = _softmax_sink_pv(s, sink, vcat)
    for hh in range(WIN_GROUP):
        o_ref[:, hh * HEAD:(hh + 1) * HEAD] = _bf(o[hh * QBLK:(hh + 1) * QBLK])


def _lat_win_attn(qkv, cache_k, cache_v, sink, layer):
    gw = WIN_GROUP * HEAD
    nb = LAT_LEN // QBLK
    base = ROWS_CTX // QBLK

    def band(col0, shift):
        def imap(b, n, g):
            return (base + b * nb + jnp.clip(n + shift, 0, nb - 1), col0 // HEAD + g)
        return pl.BlockSpec((QBLK, HEAD), imap)

    cache_spec = pl.BlockSpec((None, None, PAST, HEAD), lambda b, n, g: (b, layer, 0, g))
    return pl.pallas_call(
        functools.partial(_lat_win_kernel, layer),
        grid=(N_LAT_SEQ, nb, WIN_KV),
        in_specs=[pl.BlockSpec((QBLK, gw), lambda b, n, g: (base + b * nb + n, A_WQ // gw + g)),
                  cache_spec, cache_spec,
                  band(A_WK, -1), band(A_WK, 0), band(A_WK, 1),
                  band(A_WV, -1), band(A_WV, 0), band(A_WV, 1),
                  pl.BlockSpec(memory_space=pltpu.SMEM)],
        out_specs=pl.BlockSpec((QBLK, gw), lambda b, n, g: (b * nb + n, g)),
        out_shape=jax.ShapeDtypeStruct((ROWS_LAT, WIN_HEADS * HEAD), jnp.bfloat16),
        compiler_params=_params(("parallel", "parallel", "parallel")),
        name="lat_win_attn",
    )(qkv, cache_k, cache_v, qkv, qkv, qkv, qkv, qkv, qkv, sink)


def _diff_core(layer, q, k, v, lam_ref, g_ref):
    lam_init = 0.8 - 0.6 * math.exp(-0.3 * layer)
    lp = lam_ref[...]
    lam = (jnp.exp(jnp.sum(lp[0:1] * lp[1:2], axis=-1, keepdims=True))
           - jnp.exp(jnp.sum(lp[2:3] * lp[3:4], axis=-1, keepdims=True)) + lam_init)
    first = lax.broadcasted_iota(jnp.int32, q.shape, 1) < DIFF_QK
    zero = jnp.zeros_like(q)

    def softmax(qm):
        s = _dot_nt(qm, k) * (DIFF_QK ** -0.5)
        e = jnp.exp(s - jnp.max(s, axis=-1, keepdims=True))
        return e / jnp.sum(e, axis=-1, keepdims=True)

    a = softmax(jnp.where(first, q, zero)) - lam * softmax(jnp.where(first, zero, q))
    o = _dot(_bf(a), v)
    o = o * lax.rsqrt(jnp.mean(o * o, axis=-1, keepdims=True) + EPS) * g_ref[...]
    return _bf(o * (1.0 - lam_init))


def _ctx_diff_kernel(layer, q_ref, k_ref, v_ref, lam_ref, g_ref, o_ref):
    o_ref[...] = _diff_core(layer, q_ref[...], k_ref[...], v_ref[...], lam_ref, g_ref)


def _ctx_diff_attn(qkv, lam_params, subln, layer):
    def spec(col0):
        return pl.BlockSpec((CTX_LEN, HEAD), lambda b, h: (b, col0 // HEAD + h))

    return pl.pallas_call(
        functools.partial(_ctx_diff_kernel, layer),
        grid=(N_CTX_SEQ, DIFF_HEADS),
        in_specs=[spec(A_DQ), spec(A_DK), spec(A_DV),
                  pl.BlockSpec((None, 4, DIFF_QK), lambda b, h: (layer, 0, 0)),
                  _vec_spec(layer, HEAD)],
        out_specs=pl.BlockSpec((CTX_LEN, HEAD), lambda b, h: (b, h)),
        out_shape=jax.ShapeDtypeStruct((ROWS_CTX, DIFF_HEADS * HEAD), jnp.bfloat16),
        compiler_params=_params(("parallel", "parallel")),
        name="ctx_diff_attn",
    )(qkv, qkv, qkv, lam_params, subln)


LAT_DIFF_TQ = 256


def _lat_diff_kernel(layer, q_ref, kc_ref, vc_ref, k_ref, v_ref, lam_ref, g_ref, o_ref):
    k = jnp.concatenate([_bf(kc_ref[...]), k_ref[...]], axis=0)
    v = jnp.concatenate([_bf(vc_ref[...]), v_ref[...]], axis=0)
    o_ref[...] = _diff_core(layer, q_ref[...], k, v, lam_ref, g_ref)


def _lat_diff_attn(qkv, cache_k, cache_v, lam_params, subln, layer):
    tq = LAT_DIFF_TQ
    nq = LAT_LEN // tq
    cache_spec = pl.BlockSpec((None, None, PAST, HEAD), lambda b, h, t: (b, layer, 0, h))

    def seq_spec(col0):
        return pl.BlockSpec((LAT_LEN, HEAD), lambda b, h, t: (ROWS_CTX // LAT_LEN + b, col0 // HEAD + h))

    return pl.pallas_call(
        functools.partial(_lat_diff_kernel, layer),
        grid=(N_LAT_SEQ, DIFF_HEADS, nq),
        in_specs=[pl.BlockSpec((tq, HEAD), lambda b, h, t: (ROWS_CTX // tq + b * nq + t, A_DQ // HEAD + h)),
                  cache_spec, cache_spec, seq_spec(A_DK), seq_spec(A_DV),
                  pl.BlockSpec((None, 4, DIFF_QK), lambda b, h, t: (layer, 0, 0)),
                  _vec_spec(layer, HEAD)],
        out_specs=pl.BlockSpec((tq, HEAD), lambda b, h, t: (b * nq + t, h)),
        out_shape=jax.ShapeDtypeStruct((ROWS_LAT, DIFF_HEADS * HEAD), jnp.bfloat16),
        compiler_params=_params(("parallel", "parallel", "arbitrary")),
        name="lat_diff_attn",
    )(qkv, cache_k, cache_v, qkv, qkv, lam_params, subln)


MERGE_TM = 1024
MERGE_TN = 512


def _merge_kernel(h_ref, ya_ref, yb_ref, yc_ref, wga_ref, wgb_ref, wgc_ref, wa_ref, wb_ref, wc_ref, o_ref):
    h = h_ref[...]
    m = _sigmoid(_dot(h, wga_ref[...])) * _dot(ya_ref[...], wa_ref[...])
    m += _sigmoid(_dot(h, wgb_ref[...])) * _dot(yb_ref[...], wb_ref[...])
    m += _sigmoid(_dot(h, wgc_ref[...])) * _dot(yc_ref[...], wc_ref[...])
    o_ref[...] = _bf(m)


def _merge(h, ya, yb, yc, w_in, wa, wb, wc, layer):
    tm, tn = MERGE_TM, MERGE_TN
    nj = D // tn

    def gate_spec(branch):
        return pl.BlockSpec((None, D, tn), lambda i, j: (layer, 0, OFF_GATE // tn + branch * nj + j))

    y_spec = pl.BlockSpec((tm, CONV_DIM), lambda i, j: (i, 0))
    w_spec = pl.BlockSpec((None, CONV_DIM, tn), lambda i, j: (layer, 0, j))
    return pl.pallas_call(
        _merge_kernel,
        grid=(ROWS // tm, nj),
        in_specs=[pl.BlockSpec((tm, D), lambda i, j: (i, 0)), y_spec, y_spec, y_spec,
                  gate_spec(0), gate_spec(1), gate_spec(2), w_spec, w_spec, w_spec],
        out_specs=pl.BlockSpec((tm, tn), lambda i, j: (i, j)),
        out_shape=jax.ShapeDtypeStruct((ROWS, D), jnp.bfloat16),
        compiler_params=_params(("parallel", "arbitrary")),
        name="merge",
    )(h, ya, yb, yc, w_in, w_in, w_in, wa, wb, wc)


OUT_TM = 512


def _out_proj_kernel(x_ref, m_ref, w_ref, gate_ref, g_ref, shift_ref, scale_ref, xo_ref, h_ref):
    x = x_ref[...] + gate_ref[...] * _dot(m_ref[...], w_ref[...])
    xo_ref[...] = x
    h_ref[...] = _bf(_rms_mod(x, g_ref[...], scale_ref[...], shift_ref[...]))


def _out_proj(x, merged, w_out, mods, norm_mlp, layer):
    tm = OUT_TM
    row = pl.BlockSpec((tm, D), lambda i: (i, 0))
    return pl.pallas_call(
        _out_proj_kernel,
        grid=(ROWS // tm,),
        in_specs=[row, row,
                  pl.BlockSpec((None, D, D), lambda i: (layer, 0, 0)),
                  _mod_spec(layer, 2, tm), _vec_spec(layer, D),
                  _mod_spec(layer, 3, tm), _mod_spec(layer, 4, tm)],
        out_specs=[row, row],
        out_shape=[jax.ShapeDtypeStruct((ROWS, D), jnp.float32),
                   jax.ShapeDtypeStruct((ROWS, D), jnp.bfloat16)],
        compiler_params=_params(("parallel",)),
        name="out_proj",
    )(x, merged, w_out, mods, norm_mlp.reshape(N_LAYERS, 1, D), mods, mods)


MLP_TM = 512
MLP_TF = 1024


def _mlp_kernel(final, h_ref, w1_ref, w2_ref, x_ref, gate_ref, g_ref, shift_ref, scale_ref,
                *rest):
    if final:
        y_ref, acc_ref = rest
    else:
        xo_ref, hn_ref, acc_ref = rest
    j = pl.program_id(1)

    @pl.when(j == 0)
    def _():
        acc_ref[...] = jnp.zeros_like(acc_ref)

    t = jnp.maximum(_dot(h_ref[...], w1_ref[...]), 0.0)
    acc_ref[...] += _dot(_bf(t * t), w2_ref[...])

    @pl.when(j == pl.num_programs(1) - 1)
    def _():
        x = x_ref[...] + gate_ref[...] * acc_ref[...]
        if final:
            y_ref[...] = x * lax.rsqrt(jnp.mean(x * x, axis=-1, keepdims=True) + EPS) * g_ref[...]
        else:
            xo_ref[...] = x
            hn_ref[...] = _bf(_rms_mod(x, g_ref[...], scale_ref[...], shift_ref[...]))


def _mlp(h2, x, w1, w2, mods, norm_next, layer, final):
    tm, tf = MLP_TM, MLP_TF
    row = pl.BlockSpec((tm, D), lambda i, j: (i, 0))
    nxt = layer if final else layer + 1
    g_spec = pl.BlockSpec((None, 1, D), lambda i, j: (0 if final else nxt, 0, 0))
    if final:
        out_specs, out_shape = row, jax.ShapeDtypeStruct((ROWS, D), jnp.float32)
    else:
        out_specs = [row, row]
        out_shape = [jax.ShapeDtypeStruct((ROWS, D), jnp.float32),
                     jax.ShapeDtypeStruct((ROWS, D), jnp.bfloat16)]
    return pl.pallas_call(
        functools.partial(_mlp_kernel, final),
        grid=(ROWS // tm, D_FF // tf),
        in_specs=[row,
                  pl.BlockSpec((None, D, tf), lambda i, j: (layer, 0, j)),
                  pl.BlockSpec((None, tf, D), lambda i, j: (layer, j, 0)),
                  row, _mod_spec(layer, 5, tm), g_spec,
                  _mod_spec(nxt, 0, tm), _mod_spec(nxt, 1, tm)],
        out_specs=out_specs,
        out_shape=out_shape,
        scratch_shapes=[pltpu.VMEM((tm, D), jnp.float32)],
        compiler_params=_params(("parallel", "arbitrary")),
        name="mlp_final" if final else "mlp",
    )(h2, w1, w2, x, mods, norm_next, mods, mods)


def kernel(x_prompt, x_sample, cache_win_k, cache_win_v, cache_diff_k, cache_diff_v, c, c_ctx, w_ada, b_ada, norm_mix, norm_mlp, w_in, conv_w, win_sink, lambda_q1, lambda_k1, lambda_q2, lambda_k2, diff_subln, w_branch_conv, w_branch_win, w_branch_diff, w_out, w_mlp1, w_mlp2, norm_final):
    x = jnp.concatenate([x_prompt.reshape(ROWS_CTX, D), x_sample.reshape(ROWS_LAT, D)], axis=0)
    cvec = jnp.concatenate([c_ctx[None], c, jnp.zeros((N_MOD_ROWS - 1 - N_LAT_SEQ, D), jnp.float32)], axis=0)
    mods = _modulation(cvec, w_ada, b_ada).reshape(N_LAYERS * N_MOD_ROWS * N_MOD, 1, D)

    w_in_bf = _bf(w_in)
    w_attn_bf = w_in_bf[:, :, OFF_ATTN:OFF_ATTN + ATTN_COLS]
    wa, wb, wc = _bf(w_branch_conv), _bf(w_branch_win), _bf(w_branch_diff)
    w_out_bf, w1_bf, w2_bf = _bf(w_out), _bf(w_mlp1), _bf(w_mlp2)
    tables = _rope_tables(ATTN_TM)
    sink = win_sink.reshape(N_LAYERS * WIN_HEADS)
    lam_params = jnp.stack([lambda_q1, lambda_k1, lambda_q2, lambda_k2], axis=1)
    subln = diff_subln.reshape(N_LAYERS, 1, HEAD)
    ck_win = cache_win_k.reshape(N_LAT_SEQ, N_LAYERS, PAST, WIN_KV * HEAD)
    cv_win = cache_win_v.reshape(N_LAT_SEQ, N_LAYERS, PAST, WIN_KV * HEAD)
    ck_diff = cache_diff_k.reshape(N_LAT_SEQ, N_LAYERS, PAST, DIFF_HEADS * HEAD)
    cv_diff = cache_diff_v.reshape(N_LAT_SEQ, N_LAYERS, PAST, DIFF_HEADS * HEAD)
    norm_final_3d = norm_final.reshape(1, 1, D)
    norm_mix_3d = norm_mix.reshape(N_LAYERS, 1, D)

    h = _prologue(x, mods, norm_mix, 0)
    states = []
    y = None
    for layer in range(N_LAYERS):
        ya = _conv_branch(h, w_in_bf, conv_w, layer)
        qkv, state = _attn_proj(h, w_attn_bf, tables, layer)
        states.append(state)
        yb = jnp.concatenate([_ctx_win_attn(qkv, sink, layer),
                              _lat_win_attn(qkv, ck_win, cv_win, sink, layer)], axis=0)
        yc = jnp.concatenate([_ctx_diff_attn(qkv, lam_params, subln, layer),
                              _lat_diff_attn(qkv, ck_diff, cv_diff, lam_params, subln, layer)], axis=0)
        merged = _merge(h, ya, yb, yc, w_in_bf, wa, wb, wc, layer)
        x, h2 = _out_proj(x, merged, w_out_bf, mods, norm_mlp, layer)
        if layer == N_LAYERS - 1:
            y = _mlp(h2, x, w1_bf, w2_bf, mods, norm_final_3d, layer, True)
        else:
            x, h = _mlp(h2, x, w1_bf, w2_bf, mods, norm_mix_3d, layer, False)

    st = jnp.stack(states, axis=1).reshape(N_CTX_SEQ, CTX_LEN, N_LAYERS, KV_COLS)
    st = jnp.swapaxes(st, 1, 2)
    o_wk, o_wv = WIN_KV * HEAD, 2 * WIN_KV * HEAD
    o_dk = o_wv + DIFF_HEADS * HEAD
    new_win_k = st[..., :o_wk].reshape(N_CTX_SEQ, N_LAYERS, CTX_LEN, WIN_KV, HEAD)
    new_win_v = st[..., o_wk:o_wv].reshape(N_CTX_SEQ, N_LAYERS, CTX_LEN, WIN_KV, HEAD)
    new_diff_k = st[..., o_wv:o_dk].reshape(N_CTX_SEQ, N_LAYERS, CTX_LEN, DIFF_HEADS, HEAD)
    new_diff_v = st[..., o_dk:].reshape(N_CTX_SEQ, N_LAYERS, CTX_LEN, DIFF_HEADS, HEAD)
    y_prompt = y[:ROWS_CTX].reshape(N_CTX_SEQ, CTX_LEN, D)
    y_sample = y[ROWS_CTX:].reshape(N_LAT_SEQ, LAT_LEN, D)
    return (y_prompt, y_sample, new_win_k, new_win_v, new_diff_k, new_diff_v)
```

```python
import functools
import math

import jax
import jax.numpy as jnp
from jax import lax
from jax.experimental import pallas as pl
from jax.experimental.pallas import tpu as pltpu

D = 2048
N_CTX_SEQ = 16
CTX_LEN = 256
N_LAT_SEQ = 2
LAT_LEN = 2048
N_LAYERS = 2
PAST = 512
GRID_W = 64
HEAD = 128
CONV_DIM = 1024
WIN_HEADS = 8
WIN_KV = 2
WIN_GROUP = WIN_HEADS // WIN_KV
WINDOW = 128
QBLK = 128
DIFF_HEADS = 8
DIFF_QK = 64
D_FF = 4 * D
EPS = 1e-6
ROPE_BASE = 10000.0
NEG = -1e30

ROWS_CTX = N_CTX_SEQ * CTX_LEN
ROWS_LAT = N_LAT_SEQ * LAT_LEN
ROWS = ROWS_CTX + ROWS_LAT
N_MOD_ROWS = 8
N_MOD = 6

OFF_CONV = 0
OFF_ATTN = 3 * CONV_DIM
ATTN_COLS = (WIN_HEADS + 2 * WIN_KV) * HEAD + 3 * DIFF_HEADS * HEAD
OFF_GATE = OFF_ATTN + ATTN_COLS
IN_TOTAL = OFF_GATE + 3 * D
A_WQ, A_WK, A_WV = 0, WIN_HEADS * HEAD, (WIN_HEADS + WIN_KV) * HEAD
A_DQ = (WIN_HEADS + 2 * WIN_KV) * HEAD
A_DK = A_DQ + DIFF_HEADS * HEAD
A_DV = A_DK + DIFF_HEADS * HEAD

VMEM_LIMIT_V7X = 60 * 1024 * 1024


def _params(sem, vmem=VMEM_LIMIT_V7X):
    return pltpu.CompilerParams(dimension_semantics=sem, vmem_limit_bytes=vmem)


def _bf(x):
    return x.astype(jnp.bfloat16)


def _dot(a, b):
    return jnp.dot(a, b, preferred_element_type=jnp.float32)


def _dot_nt(a, b):
    return lax.dot_general(a, b, (((1,), (1,)), ((), ())), preferred_element_type=jnp.float32)


def _sigmoid(x):
    return 1.0 / (1.0 + jnp.exp(-x))


def _mod_group(i, tm):
    n_ctx = ROWS_CTX // tm
    return jnp.where(i < n_ctx, 0, 1 + (i - n_ctx) // (LAT_LEN // tm))


def _mod_spec(layer, which, tm):
    def imap(i, *_):
        return (layer * N_MOD_ROWS * N_MOD + _mod_group(i, tm) * N_MOD + which, 0, 0)
    return pl.BlockSpec((None, 1, D), imap)


def _vec_spec(layer, width):
    return pl.BlockSpec((None, 1, width), lambda *_: (layer, 0, 0))


def _rms_mod(x, g, scale, shift):
    y = x * lax.rsqrt(jnp.mean(x * x, axis=-1, keepdims=True) + EPS) * g
    return y * (1.0 + scale) + shift


def _mod_kernel(c_ref, w_ref, b_ref, o_ref):
    c = c_ref[...]
    s = c * _sigmoid(c)
    o_ref[...] = _dot(_bf(s), _bf(w_ref[...])) + b_ref[...]


def _modulation(cvec, w_ada, b_ada):
    tn = 1024
    n = N_MOD * D
    return pl.pallas_call(
        _mod_kernel,
        grid=(N_LAYERS, n // tn),
        in_specs=[pl.BlockSpec((N_MOD_ROWS, D), lambda l, j: (0, 0)),
                  pl.BlockSpec((None, D, tn), lambda l, j: (l, 0, j)),
                  pl.BlockSpec((None, 1, tn), lambda l, j: (l, 0, j))],
        out_specs=pl.BlockSpec((None, N_MOD_ROWS, tn), lambda l, j: (l, 0, j)),
        out_shape=jax.ShapeDtypeStruct((N_LAYERS, N_MOD_ROWS, n), jnp.float32),
        compiler_params=_params(("parallel", "parallel")),
        name="modulation",
    )(cvec, w_ada, b_ada.reshape(N_LAYERS, 1, n))


def _split_specs(tm, width):
    n_ctx = ROWS_CTX // tm
    return (pl.BlockSpec((tm, width), lambda i, *_: (jnp.minimum(i, n_ctx - 1), 0)),
            pl.BlockSpec((tm, width), lambda i, *_: (jnp.maximum(i - n_ctx, 0), 0)))


def _pick_rows(tm, ctx_ref, lat_ref):
    return jnp.where(pl.program_id(0) < ROWS_CTX // tm, ctx_ref[...], lat_ref[...])


PRO_TM = 512


def _prologue_kernel(xc_ref, xl_ref, g_ref, shift_ref, scale_ref, h_ref):
    x = _pick_rows(PRO_TM, xc_ref, xl_ref)
    h_ref[...] = _bf(_rms_mod(x, g_ref[...], scale_ref[...], shift_ref[...]))


def _prologue(x_ctx, x_lat, mods, norm_mix, layer):
    tm = PRO_TM
    return pl.pallas_call(
        _prologue_kernel,
        grid=(ROWS // tm,),
        in_specs=[*_split_specs(tm, D),
                  _vec_spec(layer, D),
                  _mod_spec(layer, 0, tm),
                  _mod_spec(layer, 1, tm)],
        out_specs=pl.BlockSpec((tm, D), lambda i: (i, 0)),
        out_shape=jax.ShapeDtypeStruct((ROWS, D), jnp.bfloat16),
        compiler_params=_params(("arbitrary",)),
        name="prologue",
    )(x_ctx, x_lat, norm_mix.reshape(N_LAYERS, 1, D), mods, mods)


CONV_TM = LAT_LEN
CONV_TN = 256


def _conv_kernel(h_ref, wb_ref, wc_ref, wx_ref, cw_ref, o_ref):
    i = pl.program_id(0)
    h = h_ref[...]
    cb = _dot(h, wb_ref[...])
    z = _dot(h, wc_ref[...]) * _dot(h, wx_ref[...])
    seq_mask = jnp.where(i < ROWS_CTX // CONV_TM, CTX_LEN - 1, LAT_LEN - 1)
    pos = lax.broadcasted_iota(jnp.int32, z.shape, 0) & seq_mask
    z_prev = jnp.where(pos == 0, 0.0, pltpu.roll(z, 1, 0))
    z_next = jnp.where(pos == seq_mask, 0.0, pltpu.roll(z, CONV_TM - 1, 0))
    cw = cw_ref[...]
    conv = z_prev * cw[0:1, :] + z * cw[1:2, :] + z_next * cw[2:3, :]
    o_ref[...] = _bf(cb * conv)


def _conv_branch(h, w_in, conv_w, layer):
    nj = CONV_DIM // CONV_TN

    def wspec(group):
        return pl.BlockSpec((None, D, CONV_TN), lambda i, j: (layer, 0, group * nj + j))

    return pl.pallas_call(
        _conv_kernel,
        grid=(ROWS // CONV_TM, nj),
        in_specs=[pl.BlockSpec((CONV_TM, D), lambda i, j: (i, 0)),
                  wspec(0), wspec(1), wspec(2),
                  pl.BlockSpec((None, 3, CONV_TN), lambda i, j: (layer, 0, j))],
        out_specs=pl.BlockSpec((CONV_TM, CONV_TN), lambda i, j: (i, j)),
        out_shape=jax.ShapeDtypeStruct((ROWS, CONV_DIM), jnp.bfloat16),
        compiler_params=_params(("parallel", "arbitrary")),
        name="conv_branch",
    )(h, w_in, w_in, w_in, conv_w)


ATTN_TM = CTX_LEN
ATTN_CHUNK = 512
ROPE_WIN, ROPE_DIFF, ROPE_NONE = 0, 1, 2
_SLAB_KIND = ([ROPE_WIN] * (WIN_HEADS + WIN_KV) + [ROPE_NONE] * WIN_KV
              + [ROPE_DIFF] * (2 * DIFF_HEADS) + [ROPE_NONE] * DIFF_HEADS)
_ROPE_SHIFT = {ROPE_WIN: HEAD // 4, ROPE_DIFF: DIFF_QK // 4}
_STATE_COLS = ((A_WK, WIN_KV * HEAD), (A_WV, WIN_KV * HEAD),
               (A_DK, DIFF_HEADS * HEAD), (A_DV, DIFF_HEADS * HEAD))
DIFF_Q_SCALE = DIFF_QK ** -0.5 * math.log2(math.e)


def _rope_tables(tm):
    t = jnp.arange(LAT_LEN, dtype=jnp.int32)
    rows = (t // GRID_W).astype(jnp.float32)[:, None]
    cols = (t % GRID_W).astype(jnp.float32)[:, None]
    lane = jnp.arange(HEAD, dtype=jnp.int32)[None, :]
    out = []
    for kind in (ROPE_WIN, ROPE_DIFF):
        s = _ROPE_SHIFT[kind]
        within = lane % (4 * s)
        freq = (within % s).astype(jnp.float32)
        inv = ROPE_BASE ** (-freq / s)
        pos = jnp.where(within < 2 * s, rows, cols)
        ang = pos * inv
        first = (within % (2 * s)) < s
        cos, sin = jnp.cos(ang), jnp.sin(ang)
        up = jnp.where(first, -sin, 0.0)
        down = jnp.where(first, 0.0, sin)
        ident = (jnp.ones((tm, HEAD), jnp.float32), jnp.zeros((tm, HEAD), jnp.float32),
                 jnp.zeros((tm, HEAD), jnp.float32))
        for tab, idn in zip((cos, up, down), ident):
            out.append(jnp.concatenate([idn, tab.astype(jnp.float32)], axis=0))
    return jnp.stack(out)


def _attn_proj_kernel(first, h_ref, w_ref, tab_ref, *refs):
    qkv_ref, *state_refs = refs[-(1 + len(_STATE_COLS)):]
    i = pl.program_id(0)
    h = h_ref[...]
    is_ctx = i < N_CTX_SEQ

    if first:
        @pl.when(is_ctx)
        def _():
            for ref in state_refs:
                ref[1:] = jnp.zeros((N_LAYERS - 1,) + ref.shape[1:], jnp.float32)

    for c0 in range(0, ATTN_COLS, ATTN_CHUNK):
        acc = _dot(h, w_ref[:, c0:c0 + ATTN_CHUNK])
        for s0 in range(0, ATTN_CHUNK, HEAD):
            col = c0 + s0
            x = acc[:, s0:s0 + HEAD]
            kind = _SLAB_KIND[col // HEAD]
            for ref, (src, width) in zip(state_refs, _STATE_COLS):
                if src <= col < src + width:
                    @pl.when(is_ctx)
                    def _(x=x, ref=ref, dst=col - src):
                        if first:
                            ref[0, :, dst:dst + HEAD] = x
                        else:
                            ref[:, dst:dst + HEAD] = x
            if kind != ROPE_NONE:
                s = _ROPE_SHIFT[kind]
                x = (x * tab_ref[3 * kind] + pltpu.roll(x, HEAD - s, 1) * tab_ref[3 * kind + 1]
                     + pltpu.roll(x, s, 1) * tab_ref[3 * kind + 2])
            if A_DQ <= col < A_DK:
                x = x * DIFF_Q_SCALE
            qkv_ref[:, col:col + HEAD] = _bf(x)


def _attn_proj(h, w_attn, tables, layer, prev_states):
    tm = ATTN_TM
    per_lat = LAT_LEN // tm
    first = prev_states is None

    def state_spec(width):
        if first:
            return pl.BlockSpec((None, N_LAYERS, CTX_LEN, width),
                                lambda i: (jnp.minimum(i, N_CTX_SEQ - 1), 0, 0, 0))
        return pl.BlockSpec((None, None, CTX_LEN, width),
                            lambda i: (jnp.minimum(i, N_CTX_SEQ - 1), layer, 0, 0))

    in_specs = [pl.BlockSpec((tm, D), lambda i: (i, 0)),
                pl.BlockSpec((None, D, ATTN_COLS), lambda i: (layer, 0, 0), pipeline_mode=pl.Buffered(1)),
                pl.BlockSpec((6, tm, HEAD),
                             lambda i: (0, jnp.where(i < N_CTX_SEQ, 0, 1 + (i - N_CTX_SEQ) % per_lat), 0))]
    args = [h, w_attn, tables]
    aliases = {}
    if not first:
        for k, st in enumerate(prev_states):
            in_specs.append(pl.BlockSpec(memory_space=pl.ANY))
            aliases[len(args)] = 1 + k
            args.append(st)
    out = pl.pallas_call(
        functools.partial(_attn_proj_kernel, first),
        grid=(ROWS // tm,),
        in_specs=in_specs,
        out_specs=[pl.BlockSpec((tm, ATTN_COLS), lambda i: (i, 0))]
                  + [state_spec(width) for _, width in _STATE_COLS],
        out_shape=[jax.ShapeDtypeStruct((ROWS, ATTN_COLS), jnp.bfloat16)]
                  + [jax.ShapeDtypeStruct((N_CTX_SEQ, N_LAYERS, CTX_LEN, width), jnp.float32)
                     for _, width in _STATE_COLS],
        input_output_aliases=aliases,
        compiler_params=_params(("arbitrary",)),
        name="attn_proj",
    )(*args)
    return out[0], out[1:]


def _slab(ref, col0, idx):
    return ref[:, col0 + idx * HEAD:col0 + (idx + 1) * HEAD]


def _win_head(q, k, v, bias, sink):
    s = _dot_nt(q, k) * (HEAD ** -0.5)
    if bias is not None:
        s = s + bias
    m = jnp.maximum(jnp.max(s, axis=-1, keepdims=True), sink)
    p = jnp.exp(s - m)
    den = jnp.sum(p, axis=-1, keepdims=True) + jnp.exp(sink - m)
    return _bf(_dot(_bf(p), v) / den)


def _lat_win_kernel(layer, q_ref, kc_ref, vc_ref, kseq_ref, vseq_ref, sink_ref, o_ref):
    n = pl.program_id(1)
    start = pl.multiple_of(jnp.clip((n - 1) * QBLK, 0, LAT_LEN - 3 * QBLK), QBLK)
    kb = kseq_ref[pl.ds(start, 3 * QBLK), :]
    vb = vseq_ref[pl.ds(start, 3 * QBLK), :]
    shape = (QBLK, PAST + 3 * QBLK)
    col = lax.broadcasted_iota(jnp.int32, shape, 1)
    qpos = n * QBLK + lax.broadcasted_iota(jnp.int32, shape, 0)
    kpos = start + col - PAST
    bias = jnp.where((col < PAST) | (jnp.abs(qpos - kpos) <= WINDOW), 0.0, NEG)
    for g in range(WIN_KV):
        k = jnp.concatenate([_bf(_slab(kc_ref, 0, g)), kb[:, g * HEAD:(g + 1) * HEAD]], axis=0)
        v = jnp.concatenate([_bf(_slab(vc_ref, 0, g)), vb[:, g * HEAD:(g + 1) * HEAD]], axis=0)
        for hh in range(WIN_GROUP):
            head = g * WIN_GROUP + hh
            o_ref[:, head * HEAD:(head + 1) * HEAD] = _win_head(
                _slab(q_ref, 0, head), k, v, bias, sink_ref[layer * WIN_HEADS + head])


def _lat_win_attn(qkv, cache_k, cache_v, sink, layer):
    nb = LAT_LEN // QBLK
    kvw = WIN_KV * HEAD
    cache_spec = pl.BlockSpec((None, None, PAST, kvw), lambda b, n: (b, layer, 0, 0))

    def seq_spec(col0):
        return pl.BlockSpec((LAT_LEN, kvw), lambda b, n: (ROWS_CTX // LAT_LEN + b, col0 // kvw))

    return pl.pallas_call(
        functools.partial(_lat_win_kernel, layer),
        grid=(N_LAT_SEQ, nb),
        in_specs=[pl.BlockSpec((QBLK, WIN_HEADS * HEAD), lambda b, n: (ROWS_CTX // QBLK + b * nb + n, 0)),
                  cache_spec, cache_spec, seq_spec(A_WK), seq_spec(A_WV),
                  pl.BlockSpec(memory_space=pltpu.SMEM)],
        out_specs=pl.BlockSpec((QBLK, WIN_HEADS * HEAD), lambda b, n: (b * nb + n, 0)),
        out_shape=jax.ShapeDtypeStruct((ROWS_LAT, WIN_HEADS * HEAD), jnp.bfloat16),
        compiler_params=_params(("parallel", "parallel")),
        name="lat_win_attn",
    )(qkv, cache_k, cache_v, qkv, qkv, sink)


def _lam_init(layer):
    return 0.8 - 0.6 * math.exp(-0.3 * layer)


def _diff_lambda(layer, lam_ref):
    lp = lam_ref[...]
    return (jnp.exp(jnp.sum(lp[0:1] * lp[1:2], axis=-1, keepdims=True))
            - jnp.exp(jnp.sum(lp[2:3] * lp[3:4], axis=-1, keepdims=True)) + _lam_init(layer))


def _diff_head(layer, q, k, v, lam, g):
    first = lax.broadcasted_iota(jnp.int32, q.shape, 1) < DIFF_QK
    zero = jnp.zeros_like(q)

    def numerator(qm):
        s = _dot_nt(qm, k)
        e = jnp.exp2(s - jnp.max(s, axis=-1, keepdims=True))
        return e, jnp.sum(e, axis=-1, keepdims=True)

    e1, l1 = numerator(jnp.where(first, q, zero))
    e2, l2 = numerator(jnp.where(first, zero, q))
    a = e1 * (1.0 / l1) - e2 * (lam / l2)
    o = _dot(_bf(a), v)
    o = o * lax.rsqrt(jnp.mean(o * o, axis=-1, keepdims=True) + EPS) * g
    return _bf(o * (1.0 - _lam_init(layer)))


def _ctx_attn_kernel(layer, qkv_ref, sink_ref, lam_ref, g_ref, yb_ref, yc_ref):
    for g in range(WIN_KV):
        k, v = _slab(qkv_ref, A_WK, g), _slab(qkv_ref, A_WV, g)
        for hh in range(WIN_GROUP):
            head = g * WIN_GROUP + hh
            yb_ref[:, head * HEAD:(head + 1) * HEAD] = _win_head(
                _slab(qkv_ref, A_WQ, head), k, v, None, sink_ref[layer * WIN_HEADS + head])
    lam = _diff_lambda(layer, lam_ref)
    for h in range(DIFF_HEADS):
        yc_ref[:, h * HEAD:(h + 1) * HEAD] = _diff_head(
            layer, _slab(qkv_ref, A_DQ, h), _slab(qkv_ref, A_DK, h), _slab(qkv_ref, A_DV, h), lam, g_ref[...])


def _ctx_attn(qkv, sink, lam_params, subln, layer):
    out = pl.BlockSpec((CTX_LEN, WIN_HEADS * HEAD), lambda b: (b, 0))
    return pl.pallas_call(
        functools.partial(_ctx_attn_kernel, layer),
        grid=(N_CTX_SEQ,),
        in_specs=[pl.BlockSpec((CTX_LEN, ATTN_COLS), lambda b: (b, 0)),
                  pl.BlockSpec(memory_space=pltpu.SMEM),
                  pl.BlockSpec((None, 4, DIFF_QK), lambda b: (layer, 0, 0)),
                  _vec_spec(layer, HEAD)],
        out_specs=[out, out],
        out_shape=[jax.ShapeDtypeStruct((ROWS_CTX, WIN_HEADS * HEAD), jnp.bfloat16),
                   jax.ShapeDtypeStruct((ROWS_CTX, DIFF_HEADS * HEAD), jnp.bfloat16)],
        compiler_params=_params(("parallel",)),
        name="ctx_attn",
    )(qkv, sink, lam_params, subln)


LAT_DIFF_TQ = 512


def _lat_diff_kernel(layer, q_ref, kc_ref, vc_ref, k_ref, v_ref, lam_ref, g_ref, o_ref, kall_ref, vall_ref):
    @pl.when(pl.program_id(2) == 0)
    def _():
        kall_ref[0:PAST] = _bf(kc_ref[...])
        kall_ref[PAST:] = k_ref[...]
        vall_ref[0:PAST] = _bf(vc_ref[...])
        vall_ref[PAST:] = v_ref[...]

    o_ref[...] = _diff_head(layer, q_ref[...], kall_ref[...], vall_ref[...],
                            _diff_lambda(layer, lam_ref), g_ref[...])


def _lat_diff_attn(qkv, cache_k, cache_v, lam_params, subln, layer):
    tq = LAT_DIFF_TQ
    nq = LAT_LEN // tq
    cache_spec = pl.BlockSpec((None, None, PAST, HEAD), lambda b, h, t: (b, layer, 0, h))

    def seq_spec(col0):
        return pl.BlockSpec((LAT_LEN, HEAD), lambda b, h, t: (ROWS_CTX // LAT_LEN + b, col0 // HEAD + h))

    return pl.pallas_call(
        functools.partial(_lat_diff_kernel, layer),
        grid=(N_LAT_SEQ, DIFF_HEADS, nq),
        in_specs=[pl.BlockSpec((tq, HEAD), lambda b, h, t: (ROWS_CTX // tq + b * nq + t, A_DQ // HEAD + h)),
                  cache_spec, cache_spec, seq_spec(A_DK), seq_spec(A_DV),
                  pl.BlockSpec((None, 4, DIFF_QK), lambda b, h, t: (layer, 0, 0)),
                  _vec_spec(layer, HEAD)],
        out_specs=pl.BlockSpec((tq, HEAD), lambda b, h, t: (b * nq + t, h)),
        out_shape=jax.ShapeDtypeStruct((ROWS_LAT, DIFF_HEADS * HEAD), jnp.bfloat16),
        scratch_shapes=[pltpu.VMEM((PAST + LAT_LEN, HEAD), jnp.bfloat16),
                        pltpu.VMEM((PAST + LAT_LEN, HEAD), jnp.bfloat16)],
        compiler_params=_params(("parallel", "parallel", "arbitrary")),
        name="lat_diff_attn",
    )(qkv, cache_k, cache_v, qkv, qkv, lam_params, subln)


MERGE_TM = 1024
MERGE_TN = 512


def _merge_kernel(h_ref, ya_ref, ybc_ref, ybl_ref, ycc_ref, ycl_ref,
                  wga_ref, wgb_ref, wgc_ref, wa_ref, wb_ref, wc_ref, o_ref):
    h = h_ref[...]
    yb = _pick_rows(MERGE_TM, ybc_ref, ybl_ref)
    yc = _pick_rows(MERGE_TM, ycc_ref, ycl_ref)
    m = _sigmoid(_dot(h, wga_ref[...])) * _dot(ya_ref[...], wa_ref[...])
    m += _sigmoid(_dot(h, wgb_ref[...])) * _dot(yb, wb_ref[...])
    m += _sigmoid(_dot(h, wgc_ref[...])) * _dot(yc, wc_ref[...])
    o_ref[...] = _bf(m)


def _merge(h, ya, yb_ctx, yb_lat, yc_ctx, yc_lat, w_in, wa, wb, wc, layer):
    tm, tn = MERGE_TM, MERGE_TN
    nj = D // tn

    def gate_spec(branch):
        return pl.BlockSpec((None, D, tn), lambda i, j: (layer, 0, OFF_GATE // tn + branch * nj + j))

    w_spec = pl.BlockSpec((None, CONV_DIM, tn), lambda i, j: (layer, 0, j))
    return pl.pallas_call(
        _merge_kernel,
        grid=(ROWS // tm, nj),
        in_specs=[pl.BlockSpec((tm, D), lambda i, j: (i, 0)),
                  pl.BlockSpec((tm, CONV_DIM), lambda i, j: (i, 0)),
                  *_split_specs(tm, WIN_HEADS * HEAD), *_split_specs(tm, DIFF_HEADS * HEAD),
                  gate_spec(0), gate_spec(1), gate_spec(2), w_spec, w_spec, w_spec],
        out_specs=pl.BlockSpec((tm, tn), lambda i, j: (i, j)),
        out_shape=jax.ShapeDtypeStruct((ROWS, D), jnp.bfloat16),
        compiler_params=_params(("parallel", "arbitrary")),
        name="merge",
    )(h, ya, yb_ctx, yb_lat, yc_ctx, yc_lat, w_in, w_in, w_in, wa, wb, wc)


OUT_TM = 512


def _out_proj_kernel(n_x, *refs):
    x_refs = refs[:n_x]
    m_ref, w_ref, gate_ref, g_ref, shift_ref, scale_ref, xo_ref, h_ref = refs[n_x:]
    x = x_refs[0][...] if n_x == 1 else _pick_rows(OUT_TM, *x_refs)
    x = x + gate_ref[...] * _dot(m_ref[...], w_ref[...])
    xo_ref[...] = x
    h_ref[...] = _bf(_rms_mod(x, g_ref[...], scale_ref[...], shift_ref[...]))


def _out_proj(xs, merged, w_out, mods, norm_mlp, layer):
    tm = OUT_TM
    row = pl.BlockSpec((tm, D), lambda i: (i, 0))
    x_specs = [row] if len(xs) == 1 else list(_split_specs(tm, D))
    return pl.pallas_call(
        functools.partial(_out_proj_kernel, len(xs)),
        grid=(ROWS // tm,),
        in_specs=x_specs + [row,
                            pl.BlockSpec((None, D, D), lambda i: (layer, 0, 0)),
                            _mod_spec(layer, 2, tm), _vec_spec(layer, D),
                            _mod_spec(layer, 3, tm), _mod_spec(layer, 4, tm)],
        out_specs=[row, row],
        out_shape=[jax.ShapeDtypeStruct((ROWS, D), jnp.float32),
                   jax.ShapeDtypeStruct((ROWS, D), jnp.bfloat16)],
        compiler_params=_params(("arbitrary",)),
        name="out_proj",
    )(*xs, merged, w_out, mods, norm_mlp.reshape(N_LAYERS, 1, D), mods, mods)


MLP_TM = 512
MLP_TF = 1024


def _mlp_kernel(final, h_ref, w1_ref, w2_ref, x_ref, gate_ref, g_ref, shift_ref, scale_ref,
                *rest):
    if final:
        yc_ref, yl_ref, acc_ref = rest
    else:
        xo_ref, hn_ref, acc_ref = rest
    i = pl.program_id(0)
    j = pl.program_id(1)

    @pl.when(j == 0)
    def _():
        acc_ref[...] = jnp.zeros_like(acc_ref)

    t = jnp.maximum(_dot(h_ref[...], w1_ref[...]), 0.0)
    acc_ref[...] += _dot(_bf(t * t), w2_ref[...])

    @pl.when(j == pl.num_programs(1) - 1)
    def _():
        x = x_ref[...] + gate_ref[...] * acc_ref[...]
        if final:
            y = x * lax.rsqrt(jnp.mean(x * x, axis=-1, keepdims=True) + EPS) * g_ref[...]

            @pl.when(i < ROWS_CTX // MLP_TM)
            def _():
                yc_ref[...] = y

            @pl.when(i >= ROWS_CTX // MLP_TM)
            def _():
                yl_ref[...] = y
        else:
            xo_ref[...] = x
            hn_ref[...] = _bf(_rms_mod(x, g_ref[...], scale_ref[...], shift_ref[...]))


def _mlp(h2, x, w1, w2, mods, norm_next, layer, final):
    tm, tf = MLP_TM, MLP_TF
    row = pl.BlockSpec((tm, D), lambda i, j: (i, 0))
    nxt = layer if final else layer + 1
    g_spec = pl.BlockSpec((None, 1, D), lambda i, j: (0 if final else nxt, 0, 0))
    if final:
        out_specs = list(_split_specs(tm, D))
        out_shape = [jax.ShapeDtypeStruct((ROWS_CTX, D), jnp.float32),
                     jax.ShapeDtypeStruct((ROWS_LAT, D), jnp.float32)]
    else:
        out_specs = [row, row]
        out_shape = [jax.ShapeDtypeStruct((ROWS, D), jnp.float32),
                     jax.ShapeDtypeStruct((ROWS, D), jnp.bfloat16)]
    return pl.pallas_call(
        functools.partial(_mlp_kernel, final),
        grid=(ROWS // tm, D_FF // tf),
        in_specs=[row,
                  pl.BlockSpec((None, D, tf), lambda i, j: (layer, 0, j)),
                  pl.BlockSpec((None, tf, D), lambda i, j: (layer, j, 0)),
                  row, _mod_spec(layer, 5, tm), g_spec,
                  _mod_spec(nxt, 0, tm), _mod_spec(nxt, 1, tm)],
        out_specs=out_specs,
        out_shape=out_shape,
        scratch_shapes=[pltpu.VMEM((tm, D), jnp.float32)],
        compiler_params=_params(("arbitrary", "arbitrary")),
        name="mlp_final" if final else "mlp",
    )(h2, w1, w2, x, mods, norm_next, mods, mods)


def kernel(x_prompt, x_sample, cache_win_k, cache_win_v, cache_diff_k, cache_diff_v, c, c_ctx, w_ada, b_ada, norm_mix, norm_mlp, w_in, conv_w, win_sink, lambda_q1, lambda_k1, lambda_q2, lambda_k2, diff_subln, w_branch_conv, w_branch_win, w_branch_diff, w_out, w_mlp1, w_mlp2, norm_final):
    xs = (x_prompt.reshape(ROWS_CTX, D), x_sample.reshape(ROWS_LAT, D))
    cvec = jnp.concatenate([c_ctx[None], c, jnp.zeros((N_MOD_ROWS - 1 - N_LAT_SEQ, D), jnp.float32)], axis=0)
    mods = _modulation(cvec, w_ada, b_ada).reshape(N_LAYERS * N_MOD_ROWS * N_MOD, 1, D)

    w_in_bf = _bf(w_in)
    w_attn_bf = w_in_bf[:, :, OFF_ATTN:OFF_ATTN + ATTN_COLS]
    wa, wb, wc = _bf(w_branch_conv), _bf(w_branch_win), _bf(w_branch_diff)
    w_out_bf, w1_bf, w2_bf = _bf(w_out), _bf(w_mlp1), _bf(w_mlp2)
    tables = _rope_tables(ATTN_TM)
    sink = win_sink.reshape(N_LAYERS * WIN_HEADS)
    lam_params = jnp.stack([lambda_q1, lambda_k1, lambda_q2, lambda_k2], axis=1)
    subln = diff_subln.reshape(N_LAYERS, 1, HEAD)
    ck_win = cache_win_k.reshape(N_LAT_SEQ, N_LAYERS, PAST, WIN_KV * HEAD)
    cv_win = cache_win_v.reshape(N_LAT_SEQ, N_LAYERS, PAST, WIN_KV * HEAD)
    ck_diff = cache_diff_k.reshape(N_LAT_SEQ, N_LAYERS, PAST, DIFF_HEADS * HEAD)
    cv_diff = cache_diff_v.reshape(N_LAT_SEQ, N_LAYERS, PAST, DIFF_HEADS * HEAD)
    norm_final_3d = norm_final.reshape(1, 1, D)
    norm_mix_3d = norm_mix.reshape(N_LAYERS, 1, D)

    h = _prologue(*xs, mods, norm_mix, 0)
    states = None
    for layer in range(N_LAYERS):
        ya = _conv_branch(h, w_in_bf, conv_w, layer)
        qkv, states = _attn_proj(h, w_attn_bf, tables, layer, states)
        yb_ctx, yc_ctx = _ctx_attn(qkv, sink, lam_params, subln, layer)
        yb_lat = _lat_win_attn(qkv, ck_win, cv_win, sink, layer)
        yc_lat = _lat_diff_attn(qkv, ck_diff, cv_diff, lam_params, subln, layer)
        merged = _merge(h, ya, yb_ctx, yb_lat, yc_ctx, yc_lat, w_in_bf, wa, wb, wc, layer)
        x, h2 = _out_proj(xs, merged, w_out_bf, mods, norm_mlp, layer)
        if layer == N_LAYERS - 1:
            y_ctx, y_lat = _mlp(h2, x, w1_bf, w2_bf, mods, norm_final_3d, layer, True)
        else:
            x, h = _mlp(h2, x, w1_bf, w2_bf, mods, norm_mix_3d, layer, False)
            xs = (x,)

    new_win_k, new_win_v, new_diff_k, new_diff_v = states
    return (y_ctx.reshape(N_CTX_SEQ, CTX_LEN, D), y_lat.reshape(N_LAT_SEQ, LAT_LEN, D),
            new_win_k.reshape(N_CTX_SEQ, N_LAYERS, CTX_LEN, WIN_KV, HEAD),
            new_win_v.reshape(N_CTX_SEQ, N_LAYERS, CTX_LEN, WIN_KV, HEAD),
            new_diff_k.reshape(N_CTX_SEQ, N_LAYERS, CTX_LEN, DIFF_HEADS, HEAD),
            new_diff_v.reshape(N_CTX_SEQ, N_LAYERS, CTX_LEN, DIFF_HEADS, HEAD))
```

```python
import functools
import math

import jax
import jax.numpy as jnp
from jax import lax
from jax.experimental import pallas as pl
from jax.experimental.pallas import tpu as pltpu

D = 2048
N_CTX_SEQ = 16
CTX_LEN = 256
N_LAT_SEQ = 2
LAT_LEN = 2048
N_LAYERS = 2
PAST = 512
GRID_W = 64
HEAD = 128
CONV_DIM = 1024
WIN_HEADS = 8
WIN_KV = 2
WIN_GROUP = WIN_HEADS // WIN_KV
WINDOW = 128
QBLK = 128
DIFF_HEADS = 8
DIFF_QK = 64
D_FF = 4 * D
EPS = 1e-6
ROPE_BASE = 10000.0
NEG = -1e30

ROWS_CTX = N_CTX_SEQ * CTX_LEN
ROWS_LAT = N_LAT_SEQ * LAT_LEN
ROWS = ROWS_CTX + ROWS_LAT
N_MOD_ROWS = 8
N_MOD = 6

OFF_CONV = 0
OFF_ATTN = 3 * CONV_DIM
ATTN_COLS = (WIN_HEADS + 2 * WIN_KV) * HEAD + 3 * DIFF_HEADS * HEAD
OFF_GATE = OFF_ATTN + ATTN_COLS
IN_TOTAL = OFF_GATE + 3 * D
A_WQ, A_WK, A_WV = 0, WIN_HEADS * HEAD, (WIN_HEADS + WIN_KV) * HEAD
A_DQ = (WIN_HEADS + 2 * WIN_KV) * HEAD
A_DK = A_DQ + DIFF_HEADS * HEAD
A_DV = A_DK + DIFF_HEADS * HEAD

VMEM_LIMIT_V7X = 60 * 1024 * 1024


def _params(sem, vmem=VMEM_LIMIT_V7X):
    return pltpu.CompilerParams(dimension_semantics=sem, vmem_limit_bytes=vmem)


def _bf(x):
    return x.astype(jnp.bfloat16)


def _dot(a, b):
    return jnp.dot(a, b, preferred_element_type=jnp.float32)


def _dot_nt(a, b):
    return lax.dot_general(a, b, (((1,), (1,)), ((), ())), preferred_element_type=jnp.float32)


def _sigmoid(x):
    return 1.0 / (1.0 + jnp.exp(-x))


def _mod_group(i, tm):
    n_ctx = ROWS_CTX // tm
    return jnp.where(i < n_ctx, 0, 1 + (i - n_ctx) // (LAT_LEN // tm))


def _mod_spec(layer, which, tm):
    def imap(i, *_):
        return (layer * N_MOD_ROWS * N_MOD + _mod_group(i, tm) * N_MOD + which, 0, 0)
    return pl.BlockSpec((None, 1, D), imap)


def _vec_spec(layer, width):
    return pl.BlockSpec((None, 1, width), lambda *_: (layer, 0, 0))


def _rms_mod(x, g, scale, shift):
    y = x * lax.rsqrt(jnp.mean(x * x, axis=-1, keepdims=True) + EPS) * g
    return y * (1.0 + scale) + shift


def _mod_kernel(c_ref, w_ref, b_ref, o_ref):
    c = c_ref[...]
    s = c * _sigmoid(c)
    o_ref[...] = _dot(_bf(s), _bf(w_ref[...])) + b_ref[...]


def _modulation(cvec, w_ada, b_ada):
    tn = 1024
    n = N_MOD * D
    return pl.pallas_call(
        _mod_kernel,
        grid=(N_LAYERS, n // tn),
        in_specs=[pl.BlockSpec((N_MOD_ROWS, D), lambda l, j: (0, 0)),
                  pl.BlockSpec((None, D, tn), lambda l, j: (l, 0, j)),
                  pl.BlockSpec((None, 1, tn), lambda l, j: (l, 0, j))],
        out_specs=pl.BlockSpec((None, N_MOD_ROWS, tn), lambda l, j: (l, 0, j)),
        out_shape=jax.ShapeDtypeStruct((N_LAYERS, N_MOD_ROWS, n), jnp.float32),
        compiler_params=_params(("parallel", "parallel")),
        name="modulation",
    )(cvec, w_ada, b_ada.reshape(N_LAYERS, 1, n))


def _split_specs(tm, width):
    n_ctx = ROWS_CTX // tm
    return (pl.BlockSpec((tm, width), lambda i, *_: (jnp.minimum(i, n_ctx - 1), 0)),
            pl.BlockSpec((tm, width), lambda i, *_: (jnp.maximum(i - n_ctx, 0), 0)))


def _pick_rows(tm, ctx_ref, lat_ref):
    return jnp.where(pl.program_id(0) < ROWS_CTX // tm, ctx_ref[...], lat_ref[...])


PRO_TM = 512


def _prologue_kernel(xc_ref, xl_ref, g_ref, shift_ref, scale_ref, h_ref):
    x = _pick_rows(PRO_TM, xc_ref, xl_ref)
    h_ref[...] = _bf(_rms_mod(x, g_ref[...], scale_ref[...], shift_ref[...]))


def _prologue(x_ctx, x_lat, mods, norm_mix, layer):
    tm = PRO_TM
    return pl.pallas_call(
        _prologue_kernel,
        grid=(ROWS // tm,),
        in_specs=[*_split_specs(tm, D),
                  _vec_spec(layer, D),
                  _mod_spec(layer, 0, tm),
                  _mod_spec(layer, 1, tm)],
        out_specs=pl.BlockSpec((tm, D), lambda i: (i, 0)),
        out_shape=jax.ShapeDtypeStruct((ROWS, D), jnp.bfloat16),
        compiler_params=_params(("arbitrary",)),
        name="prologue",
    )(x_ctx, x_lat, norm_mix.reshape(N_LAYERS, 1, D), mods, mods)


CONV_TM = LAT_LEN
CONV_TN = 256


def _conv_kernel(h_ref, wb_ref, wc_ref, wx_ref, cw_ref, o_ref):
    i = pl.program_id(0)
    h = h_ref[...]
    cb = _dot(h, wb_ref[...])
    z = _dot(h, wc_ref[...]) * _dot(h, wx_ref[...])
    seq_mask = jnp.where(i < ROWS_CTX // CONV_TM, CTX_LEN - 1, LAT_LEN - 1)
    pos = lax.broadcasted_iota(jnp.int32, z.shape, 0) & seq_mask
    z_prev = jnp.where(pos == 0, 0.0, pltpu.roll(z, 1, 0))
    z_next = jnp.where(pos == seq_mask, 0.0, pltpu.roll(z, CONV_TM - 1, 0))
    cw = cw_ref[...]
    conv = z_prev * cw[0:1, :] + z * cw[1:2, :] + z_next * cw[2:3, :]
    o_ref[...] = _bf(cb * conv)


def _conv_branch(h, w_in, conv_w, layer):
    nj = CONV_DIM // CONV_TN

    def wspec(group):
        return pl.BlockSpec((None, D, CONV_TN), lambda i, j: (layer, 0, group * nj + j))

    return pl.pallas_call(
        _conv_kernel,
        grid=(ROWS // CONV_TM, nj),
        in_specs=[pl.BlockSpec((CONV_TM, D), lambda i, j: (i, 0)),
                  wspec(0), wspec(1), wspec(2),
                  pl.BlockSpec((None, 3, CONV_TN), lambda i, j: (layer, 0, j))],
        out_specs=pl.BlockSpec((CONV_TM, CONV_TN), lambda i, j: (i, j)),
        out_shape=jax.ShapeDtypeStruct((ROWS, CONV_DIM), jnp.bfloat16),
        compiler_params=_params(("parallel", "arbitrary")),
        name="conv_branch",
    )(h, w_in, w_in, w_in, conv_w)


ATTN_TM = CTX_LEN
ATTN_CHUNK = 512
ROPE_WIN, ROPE_DIFF, ROPE_NONE = 0, 1, 2
_SLAB_KIND = ([ROPE_WIN] * (WIN_HEADS + WIN_KV) + [ROPE_NONE] * WIN_KV
              + [ROPE_DIFF] * (2 * DIFF_HEADS) + [ROPE_NONE] * DIFF_HEADS)
_ROPE_SHIFT = {ROPE_WIN: HEAD // 4, ROPE_DIFF: DIFF_QK // 4}
_STATE_COLS = ((A_WK, WIN_KV * HEAD), (A_WV, WIN_KV * HEAD),
               (A_DK, DIFF_HEADS * HEAD), (A_DV, DIFF_HEADS * HEAD))
DIFF_Q_SCALE = DIFF_QK ** -0.5 * math.log2(math.e)


def _rope_tables():
    t = jnp.arange(LAT_LEN, dtype=jnp.int32)
    rows = (t // GRID_W).astype(jnp.float32)[:, None]
    cols = (t % GRID_W).astype(jnp.float32)[:, None]
    lane = jnp.arange(HEAD, dtype=jnp.int32)[None, :]
    out = []
    for kind in (ROPE_WIN, ROPE_DIFF):
        s = _ROPE_SHIFT[kind]
        within = lane % (4 * s)
        freq = (within % s).astype(jnp.float32)
        inv = ROPE_BASE ** (-freq / s)
        pos = jnp.where(within < 2 * s, rows, cols)
        ang = pos * inv
        first = (within % (2 * s)) < s
        cos, sin = jnp.cos(ang), jnp.sin(ang)
        out += [cos, jnp.where(first, -sin, 0.0), jnp.where(first, 0.0, sin)]
    return jnp.stack(out).astype(jnp.float32)


def _attn_proj_kernel(first, h_ref, w_ref, tab_ref, *refs):
    qkv_ref, *state_refs = refs[-(1 + len(_STATE_COLS)):]
    is_ctx = pl.program_id(0) < N_CTX_SEQ

    def body(ctx):
        h = h_ref[...]
        if ctx and first:
            for ref in state_refs:
                ref[1:] = jnp.zeros((N_LAYERS - 1,) + ref.shape[1:], jnp.float32)
        for c0 in range(0, ATTN_COLS, ATTN_CHUNK):
            acc = _dot(h, w_ref[:, c0:c0 + ATTN_CHUNK])
            for s0 in range(0, ATTN_CHUNK, HEAD):
                col = c0 + s0
                x = acc[:, s0:s0 + HEAD]
                kind = _SLAB_KIND[col // HEAD]
                if ctx:
                    for ref, (src, width) in zip(state_refs, _STATE_COLS):
                        if src <= col < src + width:
                            dst = col - src
                            if first:
                                ref[0, :, dst:dst + HEAD] = x
                            else:
                                ref[:, dst:dst + HEAD] = x
                elif kind != ROPE_NONE:
                    s = _ROPE_SHIFT[kind]
                    x = (x * tab_ref[3 * kind] + pltpu.roll(x, HEAD - s, 1) * tab_ref[3 * kind + 1]
                         + pltpu.roll(x, s, 1) * tab_ref[3 * kind + 2])
                if A_DQ <= col < A_DK:
                    x = x * DIFF_Q_SCALE
                qkv_ref[:, col:col + HEAD] = _bf(x)

    pl.when(is_ctx)(lambda: body(True))
    pl.when(jnp.logical_not(is_ctx))(lambda: body(False))


def _attn_proj(h, w_attn, tables, layer, prev_states):
    tm = ATTN_TM
    per_lat = LAT_LEN // tm
    first = prev_states is None

    def state_spec(width):
        if first:
            return pl.BlockSpec((None, N_LAYERS, CTX_LEN, width),
                                lambda i: (jnp.minimum(i, N_CTX_SEQ - 1), 0, 0, 0))
        return pl.BlockSpec((None, None, CTX_LEN, width),
                            lambda i: (jnp.minimum(i, N_CTX_SEQ - 1), layer, 0, 0))

    in_specs = [pl.BlockSpec((tm, D), lambda i: (i, 0)),
                pl.BlockSpec((None, D, ATTN_COLS), lambda i: (layer, 0, 0), pipeline_mode=pl.Buffered(1)),
                pl.BlockSpec((6, tm, HEAD), lambda i: (0, jnp.maximum(i - N_CTX_SEQ, 0) % per_lat, 0))]
    args = [h, w_attn, tables]
    aliases = {}
    if not first:
        for k, st in enumerate(prev_states):
            in_specs.append(pl.BlockSpec(memory_space=pl.ANY))
            aliases[len(args)] = 1 + k
            args.append(st)
    out = pl.pallas_call(
        functools.partial(_attn_proj_kernel, first),
        grid=(ROWS // tm,),
        in_specs=in_specs,
        out_specs=[pl.BlockSpec((tm, ATTN_COLS), lambda i: (i, 0))]
                  + [state_spec(width) for _, width in _STATE_COLS],
        out_shape=[jax.ShapeDtypeStruct((ROWS, ATTN_COLS), jnp.bfloat16)]
                  + [jax.ShapeDtypeStruct((N_CTX_SEQ, N_LAYERS, CTX_LEN, width), jnp.float32)
                     for _, width in _STATE_COLS],
        input_output_aliases=aliases,
        compiler_params=_params(("arbitrary",)),
        name="attn_proj",
    )(*args)
    return out[0], out[1:]


def _slab(ref, col0, idx):
    return ref[:, col0 + idx * HEAD:col0 + (idx + 1) * HEAD]


def _pipelined(items, scores, finish):
    s = scores(items[0])
    for r, item in enumerate(items):
        s_next = scores(items[r + 1]) if r + 1 < len(items) else None
        finish(item, s)
        s = s_next


def _win_scores(q_ref, col0, g, k, bias):
    q = jnp.concatenate([_slab(q_ref, col0, g * WIN_GROUP + hh) for hh in range(WIN_GROUP)], axis=0)
    s = _dot_nt(q, k) * (HEAD ** -0.5)
    return s if bias is None else s + bias


def _win_finish(layer, g, s, v, sink_ref, o_ref):
    rows = s.shape[0] // WIN_GROUP
    sink = jnp.concatenate(
        [jnp.full((rows, 1), sink_ref[layer * WIN_HEADS + g * WIN_GROUP + hh], jnp.float32)
         for hh in range(WIN_GROUP)], axis=0)
    m = jnp.maximum(jnp.max(s, axis=-1, keepdims=True), sink)
    p = jnp.exp(s - m)
    den = jnp.sum(p, axis=-1, keepdims=True) + jnp.exp(sink - m)
    o = _dot(_bf(p), v) / den
    for hh in range(WIN_GROUP):
        head = g * WIN_GROUP + hh
        o_ref[:, head * HEAD:(head + 1) * HEAD] = _bf(o[hh * rows:(hh + 1) * rows])


def _lat_win_kernel(layer, q_ref, kc_ref, vc_ref, kseq_ref, vseq_ref, sink_ref, o_ref):
    n = pl.program_id(1)
    start = pl.multiple_of(jnp.clip((n - 1) * QBLK, 0, LAT_LEN - 3 * QBLK), QBLK)
    kb = kseq_ref[pl.ds(start, 3 * QBLK), :]
    vb = vseq_ref[pl.ds(start, 3 * QBLK), :]
    shape = (QBLK, PAST + 3 * QBLK)
    col = lax.broadcasted_iota(jnp.int32, shape, 1)
    qpos = n * QBLK + lax.broadcasted_iota(jnp.int32, shape, 0)
    kpos = start + col - PAST
    bias = jnp.where((col < PAST) | (jnp.abs(qpos - kpos) <= WINDOW), 0.0, NEG)
    bias = jnp.concatenate([bias] * WIN_GROUP, axis=0)

    def keys(ref, band, g):
        return jnp.concatenate([_bf(_slab(ref, 0, g)), band[:, g * HEAD:(g + 1) * HEAD]], axis=0)

    _pipelined(list(range(WIN_KV)),
               lambda g: _win_scores(q_ref, 0, g, keys(kc_ref, kb, g), bias),
               lambda g, s: _win_finish(layer, g, s, keys(vc_ref, vb, g), sink_ref, o_ref))


def _lat_win_attn(qkv, cache_k, cache_v, sink, layer):
    nb = LAT_LEN // QBLK
    kvw = WIN_KV * HEAD
    cache_spec = pl.BlockSpec((None, None, PAST, kvw), lambda b, n: (b, layer, 0, 0))

    def seq_spec(col0):
        return pl.BlockSpec((LAT_LEN, kvw), lambda b, n: (ROWS_CTX // LAT_LEN + b, col0 // kvw))

    return pl.pallas_call(
        functools.partial(_lat_win_kernel, layer),
        grid=(N_LAT_SEQ, nb),
        in_specs=[pl.BlockSpec((QBLK, WIN_HEADS * HEAD), lambda b, n: (ROWS_CTX // QBLK + b * nb + n, 0)),
                  cache_spec, cache_spec, seq_spec(A_WK), seq_spec(A_WV),
                  pl.BlockSpec(memory_space=pltpu.SMEM)],
        out_specs=pl.BlockSpec((QBLK, WIN_HEADS * HEAD), lambda b, n: (b * nb + n, 0)),
        out_shape=jax.ShapeDtypeStruct((ROWS_LAT, WIN_HEADS * HEAD), jnp.bfloat16),
        compiler_params=_params(("parallel", "parallel")),
        name="lat_win_attn",
    )(qkv, cache_k, cache_v, qkv, qkv, sink)


def _lam_init(layer):
    return 0.8 - 0.6 * math.exp(-0.3 * layer)


def _diff_lambda(layer, lam_ref):
    lp = lam_ref[...]
    return (jnp.exp(jnp.sum(lp[0:1] * lp[1:2], axis=-1, keepdims=True))
            - jnp.exp(jnp.sum(lp[2:3] * lp[3:4], axis=-1, keepdims=True)) + _lam_init(layer))


def _diff_scores(q, k):
    first = lax.broadcasted_iota(jnp.int32, q.shape, 1) < DIFF_QK
    zero = jnp.zeros_like(q)
    return _dot_nt(jnp.concatenate([jnp.where(first, q, zero), jnp.where(first, zero, q)], axis=0), k)


def _diff_finish(layer, s, v, lam, g):
    tq = s.shape[0] // 2
    e = jnp.exp2(s - jnp.max(s, axis=-1, keepdims=True))
    l = jnp.sum(e, axis=-1, keepdims=True)
    a = e[:tq] * (1.0 / l[:tq]) - e[tq:] * (lam / l[tq:])
    o = _dot(_bf(a), v)
    o = o * lax.rsqrt(jnp.mean(o * o, axis=-1, keepdims=True) + EPS) * g
    return _bf(o * (1.0 - _lam_init(layer)))


def _ctx_attn_kernel(layer, qkv_ref, sink_ref, lam_ref, g_ref, yb_ref, yc_ref):
    lam = _diff_lambda(layer, lam_ref)
    stages = [("win", g) for g in range(WIN_KV)] + [("diff", h) for h in range(DIFF_HEADS)]

    def scores(stage):
        kind, idx = stage
        if kind == "win":
            return _win_scores(qkv_ref, A_WQ, idx, _slab(qkv_ref, A_WK, idx), None)
        return _diff_scores(_slab(qkv_ref, A_DQ, idx), _slab(qkv_ref, A_DK, idx))

    def finish(stage, s):
        kind, idx = stage
        if kind == "win":
            _win_finish(layer, idx, s, _slab(qkv_ref, A_WV, idx), sink_ref, yb_ref)
        else:
            yc_ref[:, idx * HEAD:(idx + 1) * HEAD] = _diff_finish(
                layer, s, _slab(qkv_ref, A_DV, idx), lam, g_ref[...])

    _pipelined(stages, scores, finish)


def _ctx_attn(qkv, sink, lam_params, subln, layer):
    out = pl.BlockSpec((CTX_LEN, WIN_HEADS * HEAD), lambda b: (b, 0))
    return pl.pallas_call(
        functools.partial(_ctx_attn_kernel, layer),
        grid=(N_CTX_SEQ,),
        in_specs=[pl.BlockSpec((CTX_LEN, ATTN_COLS), lambda b: (b, 0)),
                  pl.BlockSpec(memory_space=pltpu.SMEM),
                  pl.BlockSpec((None, 4, DIFF_QK), lambda b: (layer, 0, 0)),
                  _vec_spec(layer, HEAD)],
        out_specs=[out, out],
        out_shape=[jax.ShapeDtypeStruct((ROWS_CTX, WIN_HEADS * HEAD), jnp.bfloat16),
                   jax.ShapeDtypeStruct((ROWS_CTX, DIFF_HEADS * HEAD), jnp.bfloat16)],
        compiler_params=_params(("parallel",)),
        name="ctx_attn",
    )(qkv, sink, lam_params, subln)


LAT_DIFF_TQ = 1024
LAT_DIFF_SUB = 256


def _lat_diff_kernel(layer, q_ref, kc_ref, vc_ref, k_ref, v_ref, lam_ref, g_ref, o_ref, kall_ref, vall_ref):
    @pl.when(pl.program_id(2) == 0)
    def _():
        kall_ref[0:PAST] = _bf(kc_ref[...])
        kall_ref[PAST:] = k_ref[...]
        vall_ref[0:PAST] = _bf(vc_ref[...])
        vall_ref[PAST:] = v_ref[...]

    lam = _diff_lambda(layer, lam_ref)

    def finish(r0, s):
        o_ref[r0:r0 + LAT_DIFF_SUB] = _diff_finish(layer, s, vall_ref[...], lam, g_ref[...])

    _pipelined(list(range(0, LAT_DIFF_TQ, LAT_DIFF_SUB)),
               lambda r0: _diff_scores(q_ref[r0:r0 + LAT_DIFF_SUB], kall_ref[...]), finish)


def _lat_diff_attn(qkv, cache_k, cache_v, lam_params, subln, layer):
    tq = LAT_DIFF_TQ
    nq = LAT_LEN // tq
    cache_spec = pl.BlockSpec((None, None, PAST, HEAD), lambda b, h, t: (b, layer, 0, h))

    def seq_spec(col0):
        return pl.BlockSpec((LAT_LEN, HEAD), lambda b, h, t: (ROWS_CTX // LAT_LEN + b, col0 // HEAD + h))

    return pl.pallas_call(
        functools.partial(_lat_diff_kernel, layer),
        grid=(N_LAT_SEQ, DIFF_HEADS, nq),
        in_specs=[pl.BlockSpec((tq, HEAD), lambda b, h, t: (ROWS_CTX // tq + b * nq + t, A_DQ // HEAD + h)),
                  cache_spec, cache_spec, seq_spec(A_DK), seq_spec(A_DV),
                  pl.BlockSpec((None, 4, DIFF_QK), lambda b, h, t: (layer, 0, 0)),
                  _vec_spec(layer, HEAD)],
        out_specs=pl.BlockSpec((tq, HEAD), lambda b, h, t: (b * nq + t, h)),
        out_shape=jax.ShapeDtypeStruct((ROWS_LAT, DIFF_HEADS * HEAD), jnp.bfloat16),
        scratch_shapes=[pltpu.VMEM((PAST + LAT_LEN, HEAD), jnp.bfloat16),
                        pltpu.VMEM((PAST + LAT_LEN, HEAD), jnp.bfloat16)],
        compiler_params=_params(("parallel", "parallel", "arbitrary")),
        name="lat_diff_attn",
    )(qkv, cache_k, cache_v, qkv, qkv, lam_params, subln)


MERGE_TM = 1024
MERGE_TN = 512


def _merge_kernel(h_ref, ya_ref, ybc_ref, ybl_ref, ycc_ref, ycl_ref,
                  wga_ref, wgb_ref, wgc_ref, wa_ref, wb_ref, wc_ref, o_ref):
    h = h_ref[...]
    yb = _pick_rows(MERGE_TM, ybc_ref, ybl_ref)
    yc = _pick_rows(MERGE_TM, ycc_ref, ycl_ref)
    m = _sigmoid(_dot(h, wga_ref[...])) * _dot(ya_ref[...], wa_ref[...])
    m += _sigmoid(_dot(h, wgb_ref[...])) * _dot(yb, wb_ref[...])
    m += _sigmoid(_dot(h, wgc_ref[...])) * _dot(yc, wc_ref[...])
    o_ref[...] = _bf(m)


def _merge(h, ya, yb_ctx, yb_lat, yc_ctx, yc_lat, w_in, wa, wb, wc, layer):
    tm, tn = MERGE_TM, MERGE_TN
    nj = D // tn

    def gate_spec(branch):
        return pl.BlockSpec((None, D, tn), lambda i, j: (layer, 0, OFF_GATE // tn + branch * nj + j))

    w_spec = pl.BlockSpec((None, CONV_DIM, tn), lambda i, j: (layer, 0, j))
    return pl.pallas_call(
        _merge_kernel,
        grid=(ROWS // tm, nj),
        in_specs=[pl.BlockSpec((tm, D), lambda i, j: (i, 0)),
                  pl.BlockSpec((tm, CONV_DIM), lambda i, j: (i, 0)),
                  *_split_specs(tm, WIN_HEADS * HEAD), *_split_specs(tm, DIFF_HEADS * HEAD),
                  gate_spec(0), gate_spec(1), gate_spec(2), w_spec, w_spec, w_spec],
        out_specs=pl.BlockSpec((tm, tn), lambda i, j: (i, j)),
        out_shape=jax.ShapeDtypeStruct((ROWS, D), jnp.bfloat16),
        compiler_params=_params(("parallel", "arbitrary")),
        name="merge",
    )(h, ya, yb_ctx, yb_lat, yc_ctx, yc_lat, w_in, w_in, w_in, wa, wb, wc)


OUT_TM = 512


def _out_proj_kernel(n_x, *refs):
    x_refs = refs[:n_x]
    m_ref, w_ref, gate_ref, g_ref, shift_ref, scale_ref, xo_ref, h_ref = refs[n_x:]
    x = x_refs[0][...] if n_x == 1 else _pick_rows(OUT_TM, *x_refs)
    x = x + gate_ref[...] * _dot(m_ref[...], w_ref[...])
    xo_ref[...] = x
    h_ref[...] = _bf(_rms_mod(x, g_ref[...], scale_ref[...], shift_ref[...]))


def _out_proj(xs, merged, w_out, mods, norm_mlp, layer):
    tm = OUT_TM
    row = pl.BlockSpec((tm, D), lambda i: (i, 0))
    x_specs = [row] if len(xs) == 1 else list(_split_specs(tm, D))
    return pl.pallas_call(
        functools.partial(_out_proj_kernel, len(xs)),
        grid=(ROWS // tm,),
        in_specs=x_specs + [row,
                            pl.BlockSpec((None, D, D), lambda i: (layer, 0, 0)),
                            _mod_spec(layer, 2, tm), _vec_spec(layer, D),
                            _mod_spec(layer, 3, tm), _mod_spec(layer, 4, tm)],
        out_specs=[row, row],
        out_shape=[jax.ShapeDtypeStruct((ROWS, D), jnp.float32),
                   jax.ShapeDtypeStruct((ROWS, D), jnp.bfloat16)],
        compiler_params=_params(("arbitrary",)),
        name="out_proj",
    )(*xs, merged, w_out, mods, norm_mlp.reshape(N_LAYERS, 1, D), mods, mods)


MLP_TM = 512
MLP_TF = 1024


def _mlp_kernel(final, h_ref, w1_ref, w2_ref, x_ref, gate_ref, g_ref, shift_ref, scale_ref,
                *rest):
    if final:
        yc_ref, yl_ref, acc_ref = rest
    else:
        xo_ref, hn_ref, acc_ref = rest
    i = pl.program_id(0)
    j = pl.program_id(1)

    @pl.when(j == 0)
    def _():
        acc_ref[...] = jnp.zeros_like(acc_ref)

    t = jnp.maximum(_dot(h_ref[...], w1_ref[...]), 0.0)
    acc_ref[...] += _dot(_bf(t * t), w2_ref[...])

    @pl.when(j == pl.num_programs(1) - 1)
    def _():
        x = x_ref[...] + gate_ref[...] * acc_ref[...]
        if final:
            y = x * lax.rsqrt(jnp.mean(x * x, axis=-1, keepdims=True) + EPS) * g_ref[...]

            @pl.when(i < ROWS_CTX // MLP_TM)
            def _():
                yc_ref[...] = y

            @pl.when(i >= ROWS_CTX // MLP_TM)
            def _():
                yl_ref[...] = y
        else:
            xo_ref[...] = x
            hn_ref[...] = _bf(_rms_mod(x, g_ref[...], scale_ref[...], shift_ref[...]))


def _mlp(h2, x, w1, w2, mods, norm_next, layer, final):
    tm, tf = MLP_TM, MLP_TF
    row = pl.BlockSpec((tm, D), lambda i, j: (i, 0))
    nxt = layer if final else layer + 1
    g_spec = pl.BlockSpec((None, 1, D), lambda i, j: (0 if final else nxt, 0, 0))
    if final:
        out_specs = list(_split_specs(tm, D))
        out_shape = [jax.ShapeDtypeStruct((ROWS_CTX, D), jnp.float32),
                     jax.ShapeDtypeStruct((ROWS_LAT, D), jnp.float32)]
    else:
        out_specs = [row, row]
        out_shape = [jax.ShapeDtypeStruct((ROWS, D), jnp.float32),
                     jax.ShapeDtypeStruct((ROWS, D), jnp.bfloat16)]
    return pl.pallas_call(
        functools.partial(_mlp_kernel, final),
        grid=(ROWS // tm, D_FF // tf),
        in_specs=[row,
                  pl.BlockSpec((None, D, tf), lambda i, j: (layer, 0, j)),
                  pl.BlockSpec((None, tf, D), lambda i, j: (layer, j, 0)),
                  row, _mod_spec(layer, 5, tm), g_spec,
                  _mod_spec(nxt, 0, tm), _mod_spec(nxt, 1, tm)],
        out_specs=out_specs,
        out_shape=out_shape,
        scratch_shapes=[pltpu.VMEM((tm, D), jnp.float32)],
        compiler_params=_params(("arbitrary", "arbitrary")),
        name="mlp_final" if final else "mlp",
    )(h2, w1, w2, x, mods, norm_next, mods, mods)


def kernel(x_prompt, x_sample, cache_win_k, cache_win_v, cache_diff_k, cache_diff_v, c, c_ctx, w_ada, b_ada, norm_mix, norm_mlp, w_in, conv_w, win_sink, lambda_q1, lambda_k1, lambda_q2, lambda_k2, diff_subln, w_branch_conv, w_branch_win, w_branch_diff, w_out, w_mlp1, w_mlp2, norm_final):
    xs = (x_prompt.reshape(ROWS_CTX, D), x_sample.reshape(ROWS_LAT, D))
    cvec = jnp.concatenate([c_ctx[None], c, jnp.zeros((N_MOD_ROWS - 1 - N_LAT_SEQ, D), jnp.float32)], axis=0)
    mods = _modulation(cvec, w_ada, b_ada).reshape(N_LAYERS * N_MOD_ROWS * N_MOD, 1, D)

    w_in_bf = _bf(w_in)
    w_attn_bf = w_in_bf[:, :, OFF_ATTN:OFF_ATTN + ATTN_COLS]
    wa, wb, wc = _bf(w_branch_conv), _bf(w_branch_win), _bf(w_branch_diff)
    w_out_bf, w1_bf, w2_bf = _bf(w_out), _bf(w_mlp1), _bf(w_mlp2)
    tables = _rope_tables()
    sink = win_sink.reshape(N_LAYERS * WIN_HEADS)
    lam_params = jnp.stack([lambda_q1, lambda_k1, lambda_q2, lambda_k2], axis=1)
    subln = diff_subln.reshape(N_LAYERS, 1, HEAD)
    ck_win = cache_win_k.reshape(N_LAT_SEQ, N_LAYERS, PAST, WIN_KV * HEAD)
    cv_win = cache_win_v.reshape(N_LAT_SEQ, N_LAYERS, PAST, WIN_KV * HEAD)
    ck_diff = cache_diff_k.reshape(N_LAT_SEQ, N_LAYERS, PAST, DIFF_HEADS * HEAD)
    cv_diff = cache_diff_v.reshape(N_LAT_SEQ, N_LAYERS, PAST, DIFF_HEADS * HEAD)
    norm_final_3d = norm_final.reshape(1, 1, D)
    norm_mix_3d = norm_mix.reshape(N_LAYERS, 1, D)

    h = _prologue(*xs, mods, norm_mix, 0)
    states = None
    for layer in range(N_LAYERS):
        ya = _conv_branch(h, w_in_bf, conv_w, layer)
        qkv, states = _attn_proj(h, w_attn_bf, tables, layer, states)
        yb_ctx, yc_ctx = _ctx_attn(qkv, sink, lam_params, subln, layer)
        yb_lat = _lat_win_attn(qkv, ck_win, cv_win, sink, layer)
        yc_lat = _lat_diff_attn(qkv, ck_diff, cv_diff, lam_params, subln, layer)
        merged = _merge(h, ya, yb_ctx, yb_lat, yc_ctx, yc_lat, w_in_bf, wa, wb, wc, layer)
        x, h2 = _out_proj(xs, merged, w_out_bf, mods, norm_mlp, layer)
        if layer == N_LAYERS - 1:
            y_ctx, y_lat = _mlp(h2, x, w1_bf, w2_bf, mods, norm_final_3d, layer, True)
        else:
            x, h = _mlp(h2, x, w1_bf, w2_bf, mods, norm_mix_3d, layer, False)
            xs = (x,)

    new_win_k, new_win_v, new_diff_k, new_diff_v = states
    return (y_ctx.reshape(N_CTX_SEQ, CTX_LEN, D), y_lat.reshape(N_LAT_SEQ, LAT_LEN, D),
            new_win_k.reshape(N_CTX_SEQ, N_LAYERS, CTX_LEN, WIN_KV, HEAD),
            new_win_v.reshape(N_CTX_SEQ, N_LAYERS, CTX_LEN, WIN_KV, HEAD),
            new_diff_k.reshape(N_CTX_SEQ, N_LAYERS, CTX_LEN, DIFF_HEADS, HEAD),
            new_diff_v.reshape(N_CTX_SEQ, N_LAYERS, CTX_LEN, DIFF_HEADS, HEAD))
```

```python
import functools
import math
from typing import NamedTuple

import jax
import jax.numpy as jnp
from jax import lax
from jax.experimental import pallas as pl
from jax.experimental.pallas import tpu as pltpu

D = 2048
N_CTX_SEQ = 16
CTX_LEN = 256
N_LAT_SEQ = 2
LAT_LEN = 2048
N_LAYERS = 2
PAST = 512
GRID_W = 64
HEAD = 128
CONV_DIM = 1024
WIN_HEADS = 8
WIN_KV = 2
WIN_GROUP = WIN_HEADS // WIN_KV
WINDOW = 128
QBLK = 128
DIFF_HEADS = 8
DIFF_QK = 64
D_FF = 4 * D
EPS = 1e-6
ROPE_BASE = 10000.0
NEG = -1e30

ROWS_CTX = N_CTX_SEQ * CTX_LEN
ROWS_LAT = N_LAT_SEQ * LAT_LEN
ROWS = ROWS_CTX + ROWS_LAT
N_MOD_ROWS = 8
N_MOD = 6

OFF_CONV = 0
OFF_ATTN = 3 * CONV_DIM
ATTN_COLS = (WIN_HEADS + 2 * WIN_KV) * HEAD + 3 * DIFF_HEADS * HEAD
OFF_GATE = OFF_ATTN + ATTN_COLS
IN_TOTAL = OFF_GATE + 3 * D
IN_CBW = 1536
A_WQ, A_WK, A_WV = 0, WIN_HEADS * HEAD, (WIN_HEADS + WIN_KV) * HEAD
A_DQ = (WIN_HEADS + 2 * WIN_KV) * HEAD
A_DK = A_DQ + DIFF_HEADS * HEAD
A_DV = A_DK + DIFF_HEADS * HEAD

VMEM_LIMIT_V7X = 60 * 1024 * 1024


def _params(sem, vmem=VMEM_LIMIT_V7X):
    return pltpu.CompilerParams(dimension_semantics=sem, vmem_limit_bytes=vmem)


def _bf(x):
    return x.astype(jnp.bfloat16)


def _dot(a, b):
    return jnp.dot(a, b, preferred_element_type=jnp.float32)


def _dot_nt(a, b):
    return lax.dot_general(a, b, (((1,), (1,)), ((), ())), preferred_element_type=jnp.float32)


def _sigmoid(x):
    return 1.0 / (1.0 + jnp.exp(-x))


def _mod_group(i, tm):
    n_ctx = ROWS_CTX // tm
    return jnp.where(i < n_ctx, 0, 1 + (i - n_ctx) // (LAT_LEN // tm))


def _mod_spec(layer, which, tm):
    def imap(i, *_):
        return (layer * N_MOD_ROWS * N_MOD + _mod_group(i, tm) * N_MOD + which, 0, 0)
    return pl.BlockSpec((None, 1, D), imap)


def _vec_spec(layer, width):
    return pl.BlockSpec((None, 1, width), lambda *_: (layer, 0, 0))


def _rms_mod(x, g, scale, shift):
    y = x * lax.rsqrt(jnp.mean(x * x, axis=-1, keepdims=True) + EPS) * g
    return y * (1.0 + scale) + shift


def _mod_kernel(c_ref, w_ref, b_ref, o_ref):
    c = c_ref[...]
    s = c * _sigmoid(c)
    o_ref[...] = _dot(_bf(s), _bf(w_ref[...])) + b_ref[...]


def _modulation(cvec, w_ada, b_ada):
    tn = 1024
    n = N_MOD * D
    return pl.pallas_call(
        _mod_kernel,
        grid=(N_LAYERS, n // tn),
        in_specs=[pl.BlockSpec((N_MOD_ROWS, D), lambda l, j: (0, 0)),
                  pl.BlockSpec((None, D, tn), lambda l, j: (l, 0, j)),
                  pl.BlockSpec((None, 1, tn), lambda l, j: (l, 0, j))],
        out_specs=pl.BlockSpec((None, N_MOD_ROWS, tn), lambda l, j: (l, 0, j)),
        out_shape=jax.ShapeDtypeStruct((N_LAYERS, N_MOD_ROWS, n), jnp.float32),
        compiler_params=_params(("parallel", "parallel")),
        name="modulation",
    )(cvec, w_ada, b_ada.reshape(N_LAYERS, 1, n))


def _split_specs(tm, width):
    n_ctx = ROWS_CTX // tm
    return (pl.BlockSpec((tm, width), lambda i, *_: (jnp.minimum(i, n_ctx - 1), 0)),
            pl.BlockSpec((tm, width), lambda i, *_: (jnp.maximum(i - n_ctx, 0), 0)))


def _pick_rows(tm, ctx_ref, lat_ref):
    return jnp.where(pl.program_id(0) < ROWS_CTX // tm, ctx_ref[...], lat_ref[...])


PRO_TM = 512


def _prologue_kernel(xc_ref, xl_ref, g_ref, shift_ref, scale_ref, h_ref):
    x = _pick_rows(PRO_TM, xc_ref, xl_ref)
    h_ref[...] = _bf(_rms_mod(x, g_ref[...], scale_ref[...], shift_ref[...]))


def _prologue(x_ctx, x_lat, mods, norm_mix, layer):
    tm = PRO_TM
    return pl.pallas_call(
        _prologue_kernel,
        grid=(ROWS // tm,),
        in_specs=[*_split_specs(tm, D),
                  _vec_spec(layer, D),
                  _mod_spec(layer, 0, tm),
                  _mod_spec(layer, 1, tm)],
        out_specs=pl.BlockSpec((tm, D), lambda i: (i, 0)),
        out_shape=jax.ShapeDtypeStruct((ROWS, D), jnp.bfloat16),
        compiler_params=_params(("arbitrary",)),
        name="prologue",
    )(x_ctx, x_lat, norm_mix.reshape(N_LAYERS, 1, D), mods, mods)


class _CastJob(NamedTuple):
    src: jax.Array
    layer: int
    k0: int
    nb: int
    cbw: int

    @property
    def rows(self):
        return self.src.shape[1]


def _whole(src, layer):
    return _CastJob(src, layer, 0, 1, src.shape[2])


def _cast_plumbing(jobs, n_steps, step_of):
    in_specs, args, out_specs, out_shapes = [], [], [], []
    for job in jobs:
        rows = job.rows // n_steps
        assert rows * n_steps == job.rows and rows % 16 == 0
        for k in range(job.nb):
            in_specs.append(pl.BlockSpec(
                (None, rows, job.cbw), lambda *g, job=job, k=k: (job.layer, step_of(*g), job.k0 + k)))
            args.append(job.src)
        out_specs.append(pl.BlockSpec((rows, job.nb * job.cbw), lambda *g: (step_of(*g), 0)))
        out_shapes.append(jax.ShapeDtypeStruct((job.rows, job.nb * job.cbw), jnp.bfloat16))
    return in_specs, args, out_specs, out_shapes


def _cast_plan(jobs):
    return tuple((job.nb, job.cbw) for job in jobs)


def _run_casts(plan, in_refs, out_refs):
    in_refs = iter(in_refs)
    for (nb, cbw), out_ref in zip(plan, out_refs):
        for k in range(nb):
            out_ref[:, k * cbw:(k + 1) * cbw] = _bf(next(in_refs)[...])


def _n_cast_inputs(plan):
    return sum(nb for nb, _ in plan)


CONV_TM = LAT_LEN
CONV_TN = 256


def _conv_kernel(plan, h_ref, wb_ref, wc_ref, wx_ref, cw_ref, *refs):
    n_in = _n_cast_inputs(plan)
    o_ref = refs[n_in]
    _run_casts(plan, refs[:n_in], refs[n_in + 1:])
    i = pl.program_id(0)
    h = h_ref[...]
    cb = _dot(h, _bf(wb_ref[...]))
    z = _dot(h, _bf(wc_ref[...])) * _dot(h, _bf(wx_ref[...]))
    seq_mask = jnp.where(i < ROWS_CTX // CONV_TM, CTX_LEN - 1, LAT_LEN - 1)
    pos = lax.broadcasted_iota(jnp.int32, z.shape, 0) & seq_mask
    z_prev = jnp.where(pos == 0, 0.0, pltpu.roll(z, 1, 0))
    z_next = jnp.where(pos == seq_mask, 0.0, pltpu.roll(z, CONV_TM - 1, 0))
    cw = cw_ref[...]
    conv = z_prev * cw[0:1, :] + z * cw[1:2, :] + z_next * cw[2:3, :]
    o_ref[...] = _bf(cb * conv)


def _conv_branch(h, w_in, conv_w, layer, jobs):
    nj = CONV_DIM // CONV_TN
    ni = ROWS // CONV_TM

    def wspec(group):
        return pl.BlockSpec((None, D, CONV_TN), lambda i, j: (layer, 0, group * nj + j))

    c_in, c_args, c_out, c_shapes = _cast_plumbing(jobs, ni * nj, lambda i, j: i * nj + j)
    out = pl.pallas_call(
        functools.partial(_conv_kernel, _cast_plan(jobs)),
        grid=(ni, nj),
        in_specs=[pl.BlockSpec((CONV_TM, D), lambda i, j: (i, 0)),
                  wspec(0), wspec(1), wspec(2),
                  pl.BlockSpec((None, 3, CONV_TN), lambda i, j: (layer, 0, j))] + c_in,
        out_specs=[pl.BlockSpec((CONV_TM, CONV_TN), lambda i, j: (i, j))] + c_out,
        out_shape=[jax.ShapeDtypeStruct((ROWS, CONV_DIM), jnp.bfloat16)] + c_shapes,
        compiler_params=_params(("arbitrary", "arbitrary")),
        name="conv_branch",
    )(h, w_in, w_in, w_in, conv_w, *c_args)
    return out[0], out[1:]


ATTN_TM = CTX_LEN
ATTN_CHUNK = 512
ROPE_WIN, ROPE_DIFF, ROPE_NONE = 0, 1, 2
_SLAB_KIND = ([ROPE_WIN] * (WIN_HEADS + WIN_KV) + [ROPE_NONE] * WIN_KV
              + [ROPE_DIFF] * (2 * DIFF_HEADS) + [ROPE_NONE] * DIFF_HEADS)
_ROPE_SHIFT = {ROPE_WIN: HEAD // 4, ROPE_DIFF: DIFF_QK // 4}
_STATE_COLS = ((A_WK, WIN_KV * HEAD), (A_WV, WIN_KV * HEAD),
               (A_DK, DIFF_HEADS * HEAD), (A_DV, DIFF_HEADS * HEAD))
DIFF_Q_SCALE = DIFF_QK ** -0.5 * math.log2(math.e)


def _rope_tables():
    t = jnp.arange(LAT_LEN, dtype=jnp.int32)
    rows = (t // GRID_W).astype(jnp.float32)[:, None]
    cols = (t % GRID_W).astype(jnp.float32)[:, None]
    lane = jnp.arange(HEAD, dtype=jnp.int32)[None, :]
    out = []
    for kind in (ROPE_WIN, ROPE_DIFF):
        s = _ROPE_SHIFT[kind]
        within = lane % (4 * s)
        freq = (within % s).astype(jnp.float32)
        inv = ROPE_BASE ** (-freq / s)
        pos = jnp.where(within < 2 * s, rows, cols)
        ang = pos * inv
        first = (within % (2 * s)) < s
        cos, sin = jnp.cos(ang), jnp.sin(ang)
        out += [cos, jnp.where(first, -sin, 0.0), jnp.where(first, 0.0, sin)]
    return jnp.stack(out).astype(jnp.float32)


def _attn_proj_kernel(first, h_ref, w_ref, tab_ref, *refs):
    qkv_ref, *state_refs = refs[-(1 + len(_STATE_COLS)):]
    is_ctx = pl.program_id(0) < N_CTX_SEQ

    def body(ctx):
        h = h_ref[...]
        if ctx and first:
            for ref in state_refs:
                ref[1:] = jnp.zeros((N_LAYERS - 1,) + ref.shape[1:], jnp.float32)
        for c0 in range(0, ATTN_COLS, ATTN_CHUNK):
            acc = _dot(h, w_ref[:, c0:c0 + ATTN_CHUNK])
            for s0 in range(0, ATTN_CHUNK, HEAD):
                col = c0 + s0
                x = acc[:, s0:s0 + HEAD]
                kind = _SLAB_KIND[col // HEAD]
                if ctx:
                    for ref, (src, width) in zip(state_refs, _STATE_COLS):
                        if src <= col < src + width:
                            dst = col - src
                            if first:
                                ref[0, :, dst:dst + HEAD] = x
                            else:
                                ref[:, dst:dst + HEAD] = x
                elif kind != ROPE_NONE:
                    s = _ROPE_SHIFT[kind]
                    x = (x * tab_ref[3 * kind] + pltpu.roll(x, HEAD - s, 1) * tab_ref[3 * kind + 1]
                         + pltpu.roll(x, s, 1) * tab_ref[3 * kind + 2])
                if A_DQ <= col < A_DK:
                    x = x * DIFF_Q_SCALE
                qkv_ref[:, col:col + HEAD] = _bf(x)

    pl.when(is_ctx)(lambda: body(True))
    pl.when(jnp.logical_not(is_ctx))(lambda: body(False))


def _attn_proj(h, w_attn, tables, layer, prev_states):
    tm = ATTN_TM
    per_lat = LAT_LEN // tm
    first = prev_states is None

    def state_spec(width):
        if first:
            return pl.BlockSpec((None, N_LAYERS, CTX_LEN, width),
                                lambda i: (jnp.minimum(i, N_CTX_SEQ - 1), 0, 0, 0))
        return pl.BlockSpec((None, None, CTX_LEN, width),
                            lambda i: (jnp.minimum(i, N_CTX_SEQ - 1), layer, 0, 0))

    in_specs = [pl.BlockSpec((tm, D), lambda i: (i, 0)),
                pl.BlockSpec((D, ATTN_COLS), lambda i: (0, 0), pipeline_mode=pl.Buffered(1)),
                pl.BlockSpec((6, tm, HEAD), lambda i: (0, jnp.maximum(i - N_CTX_SEQ, 0) % per_lat, 0))]
    args = [h, w_attn, tables]
    aliases = {}
    if not first:
        for k, st in enumerate(prev_states):
            in_specs.append(pl.BlockSpec(memory_space=pl.ANY))
            aliases[len(args)] = 1 + k
            args.append(st)
    out = pl.pallas_call(
        functools.partial(_attn_proj_kernel, first),
        grid=(ROWS // tm,),
        in_specs=in_specs,
        out_specs=[pl.BlockSpec((tm, ATTN_COLS), lambda i: (i, 0))]
                  + [state_spec(width) for _, width in _STATE_COLS],
        out_shape=[jax.ShapeDtypeStruct((ROWS, ATTN_COLS), jnp.bfloat16)]
                  + [jax.ShapeDtypeStruct((N_CTX_SEQ, N_LAYERS, CTX_LEN, width), jnp.float32)
                     for _, width in _STATE_COLS],
        input_output_aliases=aliases,
        compiler_params=_params(("arbitrary",)),
        name="attn_proj",
    )(*args)
    return out[0], out[1:]


def _slab(ref, col0, idx):
    return ref[:, col0 + idx * HEAD:col0 + (idx + 1) * HEAD]


def _pipelined(items, scores, finish):
    s = scores(items[0])
    for r, item in enumerate(items):
        s_next = scores(items[r + 1]) if r + 1 < len(items) else None
        finish(item, s)
        s = s_next


def _win_scores(q_ref, col0, g, k, bias):
    q = jnp.concatenate([_slab(q_ref, col0, g * WIN_GROUP + hh) for hh in range(WIN_GROUP)], axis=0)
    s = _dot_nt(q, k) * (HEAD ** -0.5)
    return s if bias is None else s + bias


def _win_finish(layer, g, s, v, sink_ref, o_ref):
    rows = s.shape[0] // WIN_GROUP
    sink = jnp.concatenate(
        [jnp.full((rows, 1), sink_ref[layer * WIN_HEADS + g * WIN_GROUP + hh], jnp.float32)
         for hh in range(WIN_GROUP)], axis=0)
    m = jnp.maximum(jnp.max(s, axis=-1, keepdims=True), sink)
    p = jnp.exp(s - m)
    den = jnp.sum(p, axis=-1, keepdims=True) + jnp.exp(sink - m)
    o = _dot(_bf(p), v) / den
    for hh in range(WIN_GROUP):
        head = g * WIN_GROUP + hh
        o_ref[:, head * HEAD:(head + 1) * HEAD] = _bf(o[hh * rows:(hh + 1) * rows])


def _lat_win_kernel(layer, plan, q_ref, kc_ref, vc_ref, kseq_ref, vseq_ref, sink_ref, *refs):
    n_in = _n_cast_inputs(plan)
    o_ref = refs[n_in]
    _run_casts(plan, refs[:n_in], refs[n_in + 1:])
    n = pl.program_id(1)
    start = pl.multiple_of(jnp.clip((n - 1) * QBLK, 0, LAT_LEN - 3 * QBLK), QBLK)
    kb = kseq_ref[pl.ds(start, 3 * QBLK), :]
    vb = vseq_ref[pl.ds(start, 3 * QBLK), :]
    shape = (QBLK, PAST + 3 * QBLK)
    col = lax.broadcasted_iota(jnp.int32, shape, 1)
    qpos = n * QBLK + lax.broadcasted_iota(jnp.int32, shape, 0)
    kpos = start + col - PAST
    bias = jnp.where((col < PAST) | (jnp.abs(qpos - kpos) <= WINDOW), 0.0, NEG)
    bias = jnp.concatenate([bias] * WIN_GROUP, axis=0)

    def keys(ref, band, g):
        return jnp.concatenate([_bf(_slab(ref, 0, g)), band[:, g * HEAD:(g + 1) * HEAD]], axis=0)

    _pipelined(list(range(WIN_KV)),
               lambda g: _win_scores(q_ref, 0, g, keys(kc_ref, kb, g), bias),
               lambda g, s: _win_finish(layer, g, s, keys(vc_ref, vb, g), sink_ref, o_ref))


def _lat_win_attn(qkv, cache_k, cache_v, sink, layer, jobs):
    nb = LAT_LEN // QBLK
    kvw = WIN_KV * HEAD
    cache_spec = pl.BlockSpec((None, None, PAST, kvw), lambda b, n: (b, layer, 0, 0))

    def seq_spec(col0):
        return pl.BlockSpec((LAT_LEN, kvw), lambda b, n: (ROWS_CTX // LAT_LEN + b, col0 // kvw))

    c_in, c_args, c_out, c_shapes = _cast_plumbing(jobs, N_LAT_SEQ * nb, lambda b, n: b * nb + n)
    out = pl.pallas_call(
        functools.partial(_lat_win_kernel, layer, _cast_plan(jobs)),
        grid=(N_LAT_SEQ, nb),
        in_specs=[pl.BlockSpec((QBLK, WIN_HEADS * HEAD), lambda b, n: (ROWS_CTX // QBLK + b * nb + n, 0)),
                  cache_spec, cache_spec, seq_spec(A_WK), seq_spec(A_WV),
                  pl.BlockSpec(memory_space=pltpu.SMEM)] + c_in,
        out_specs=[pl.BlockSpec((QBLK, WIN_HEADS * HEAD), lambda b, n: (b * nb + n, 0))] + c_out,
        out_shape=[jax.ShapeDtypeStruct((ROWS_LAT, WIN_HEADS * HEAD), jnp.bfloat16)] + c_shapes,
        compiler_params=_params(("arbitrary", "arbitrary")),
        name="lat_win_attn",
    )(qkv, cache_k, cache_v, qkv, qkv, sink, *c_args)
    return out[0], out[1:]


def _lam_init(layer):
    return 0.8 - 0.6 * math.exp(-0.3 * layer)


def _diff_lambda(layer, lam_ref):
    lp = lam_ref[...]
    return (jnp.exp(jnp.sum(lp[0:1] * lp[1:2], axis=-1, keepdims=True))
            - jnp.exp(jnp.sum(lp[2:3] * lp[3:4], axis=-1, keepdims=True)) + _lam_init(layer))


def _diff_scores(q, k):
    first = lax.broadcasted_iota(jnp.int32, q.shape, 1) < DIFF_QK
    zero = jnp.zeros_like(q)
    return _dot_nt(jnp.concatenate([jnp.where(first, q, zero), jnp.where(first, zero, q)], axis=0), k)


def _diff_finish(layer, s, v, lam, g):
    tq = s.shape[0] // 2
    e = jnp.exp2(s - jnp.max(s, axis=-1, keepdims=True))
    l = jnp.sum(e, axis=-1, keepdims=True)
    a = e[:tq] * (1.0 / l[:tq]) - e[tq:] * (lam / l[tq:])
    o = _dot(_bf(a), v)
    o = o * lax.rsqrt(jnp.mean(o * o, axis=-1, keepdims=True) + EPS) * g
    return _bf(o * (1.0 - _lam_init(layer)))


def _ctx_attn_kernel(layer, plan, qkv_ref, sink_ref, lam_ref, g_ref, *refs):
    n_in = _n_cast_inputs(plan)
    yb_ref, yc_ref = refs[n_in:n_in + 2]
    _run_casts(plan, refs[:n_in], refs[n_in + 2:])
    lam = _diff_lambda(layer, lam_ref)
    stages = [("win", g) for g in range(WIN_KV)] + [("diff", h) for h in range(DIFF_HEADS)]

    def scores(stage):
        kind, idx = stage
        if kind == "win":
            return _win_scores(qkv_ref, A_WQ, idx, _slab(qkv_ref, A_WK, idx), None)
        return _diff_scores(_slab(qkv_ref, A_DQ, idx), _slab(qkv_ref, A_DK, idx))

    def finish(stage, s):
        kind, idx = stage
        if kind == "win":
            _win_finish(layer, idx, s, _slab(qkv_ref, A_WV, idx), sink_ref, yb_ref)
        else:
            yc_ref[:, idx * HEAD:(idx + 1) * HEAD] = _diff_finish(
                layer, s, _slab(qkv_ref, A_DV, idx), lam, g_ref[...])

    _pipelined(stages, scores, finish)


def _ctx_attn(qkv, sink, lam_params, subln, layer, jobs):
    out = pl.BlockSpec((CTX_LEN, WIN_HEADS * HEAD), lambda b: (b, 0))
    c_in, c_args, c_out, c_shapes = _cast_plumbing(jobs, N_CTX_SEQ, lambda b: b)
    res = pl.pallas_call(
        functools.partial(_ctx_attn_kernel, layer, _cast_plan(jobs)),
        grid=(N_CTX_SEQ,),
        in_specs=[pl.BlockSpec((CTX_LEN, ATTN_COLS), lambda b: (b, 0)),
                  pl.BlockSpec(memory_space=pltpu.SMEM),
                  pl.BlockSpec((None, 4, DIFF_QK), lambda b: (layer, 0, 0)),
                  _vec_spec(layer, HEAD)] + c_in,
        out_specs=[out, out] + c_out,
        out_shape=[jax.ShapeDtypeStruct((ROWS_CTX, WIN_HEADS * HEAD), jnp.bfloat16),
                   jax.ShapeDtypeStruct((ROWS_CTX, DIFF_HEADS * HEAD), jnp.bfloat16)] + c_shapes,
        compiler_params=_params(("arbitrary",)),
        name="ctx_attn",
    )(qkv, sink, lam_params, subln, *c_args)
    return res[0], res[1], res[2:]


LAT_DIFF_TQ = 1024
LAT_DIFF_SUB = 256


def _lat_diff_kernel(layer, plan, q_ref, kc_ref, vc_ref, k_ref, v_ref, lam_ref, g_ref, *refs):
    n_in = _n_cast_inputs(plan)
    o_ref = refs[n_in]
    kall_ref, vall_ref = refs[-2:]

    @pl.when(pl.program_id(2) == 0)
    def _():
        kall_ref[0:PAST] = _bf(kc_ref[...])
        kall_ref[PAST:] = k_ref[...]
        vall_ref[0:PAST] = _bf(vc_ref[...])
        vall_ref[PAST:] = v_ref[...]

    _run_casts(plan, refs[:n_in], refs[n_in + 1:-2])
    lam = _diff_lambda(layer, lam_ref)

    def finish(r0, s):
        o_ref[r0:r0 + LAT_DIFF_SUB] = _diff_finish(layer, s, vall_ref[...], lam, g_ref[...])

    _pipelined(list(range(0, LAT_DIFF_TQ, LAT_DIFF_SUB)),
               lambda r0: _diff_scores(q_ref[r0:r0 + LAT_DIFF_SUB], kall_ref[...]), finish)


def _lat_diff_attn(qkv, cache_k, cache_v, lam_params, subln, layer, jobs):
    tq = LAT_DIFF_TQ
    nq = LAT_LEN // tq
    cache_spec = pl.BlockSpec((None, None, PAST, HEAD), lambda b, h, t: (b, layer, 0, h))

    def seq_spec(col0):
        return pl.BlockSpec((LAT_LEN, HEAD), lambda b, h, t: (ROWS_CTX // LAT_LEN + b, col0 // HEAD + h))

    c_in, c_args, c_out, c_shapes = _cast_plumbing(
        jobs, N_LAT_SEQ * DIFF_HEADS * nq, lambda b, h, t: (b * DIFF_HEADS + h) * nq + t)
    out = pl.pallas_call(
        functools.partial(_lat_diff_kernel, layer, _cast_plan(jobs)),
        grid=(N_LAT_SEQ, DIFF_HEADS, nq),
        in_specs=[pl.BlockSpec((tq, HEAD), lambda b, h, t: (ROWS_CTX // tq + b * nq + t, A_DQ // HEAD + h)),
                  cache_spec, cache_spec, seq_spec(A_DK), seq_spec(A_DV),
                  pl.BlockSpec((None, 4, DIFF_QK), lambda b, h, t: (layer, 0, 0)),
                  _vec_spec(layer, HEAD)] + c_in,
        out_specs=[pl.BlockSpec((tq, HEAD), lambda b, h, t: (b * nq + t, h))] + c_out,
        out_shape=[jax.ShapeDtypeStruct((ROWS_LAT, DIFF_HEADS * HEAD), jnp.bfloat16)] + c_shapes,
        scratch_shapes=[pltpu.VMEM((PAST + LAT_LEN, HEAD), jnp.bfloat16),
                        pltpu.VMEM((PAST + LAT_LEN, HEAD), jnp.bfloat16)],
        compiler_params=_params(("arbitrary", "arbitrary", "arbitrary")),
        name="lat_diff_attn",
    )(qkv, cache_k, cache_v, qkv, qkv, lam_params, subln, *c_args)
    return out[0], out[1:]


MERGE_TM = 1024
MERGE_TN = 512


def _merge_kernel(h_ref, ya_ref, ybc_ref, ybl_ref, ycc_ref, ycl_ref,
                  wga_ref, wgb_ref, wgc_ref, wa_ref, wb_ref, wc_ref, o_ref):
    h = h_ref[...]
    yb = _pick_rows(MERGE_TM, ybc_ref, ybl_ref)
    yc = _pick_rows(MERGE_TM, ycc_ref, ycl_ref)
    m = _sigmoid(_dot(h, wga_ref[...])) * _dot(ya_ref[...], wa_ref[...])
    m += _sigmoid(_dot(h, wgb_ref[...])) * _dot(yb, wb_ref[...])
    m += _sigmoid(_dot(h, wgc_ref[...])) * _dot(yc, wc_ref[...])
    o_ref[...] = _bf(m)


def _merge(h, ya, yb_ctx, yb_lat, yc_ctx, yc_lat, w_gate, wa, wb, wc):
    tm, tn = MERGE_TM, MERGE_TN
    nj = D // tn

    def gate_spec(branch):
        return pl.BlockSpec((D, tn), lambda i, j: (0, branch * nj + j))

    w_spec = pl.BlockSpec((CONV_DIM, tn), lambda i, j: (0, j))
    return pl.pallas_call(
        _merge_kernel,
        grid=(ROWS // tm, nj),
        in_specs=[pl.BlockSpec((tm, D), lambda i, j: (i, 0)),
                  pl.BlockSpec((tm, CONV_DIM), lambda i, j: (i, 0)),
                  *_split_specs(tm, WIN_HEADS * HEAD), *_split_specs(tm, DIFF_HEADS * HEAD),
                  gate_spec(0), gate_spec(1), gate_spec(2), w_spec, w_spec, w_spec],
        out_specs=pl.BlockSpec((tm, tn), lambda i, j: (i, j)),
        out_shape=jax.ShapeDtypeStruct((ROWS, D), jnp.bfloat16),
        compiler_params=_params(("parallel", "arbitrary")),
        name="merge",
    )(h, ya, yb_ctx, yb_lat, yc_ctx, yc_lat, w_gate, w_gate, w_gate, wa, wb, wc)


OUT_TM = 512


def _out_proj_kernel(n_x, *refs):
    x_refs = refs[:n_x]
    m_ref, w_ref, gate_ref, g_ref, shift_ref, scale_ref, xo_ref, h_ref = refs[n_x:]
    x = x_refs[0][...] if n_x == 1 else _pick_rows(OUT_TM, *x_refs)
    x = x + gate_ref[...] * _dot(m_ref[...], w_ref[...])
    xo_ref[...] = x
    h_ref[...] = _bf(_rms_mod(x, g_ref[...], scale_ref[...], shift_ref[...]))


def _out_proj(xs, merged, w_out, mods, norm_mlp, layer):
    tm = OUT_TM
    row = pl.BlockSpec((tm, D), lambda i: (i, 0))
    x_specs = [row] if len(xs) == 1 else list(_split_specs(tm, D))
    return pl.pallas_call(
        functools.partial(_out_proj_kernel, len(xs)),
        grid=(ROWS // tm,),
        in_specs=x_specs + [row,
                            pl.BlockSpec((D, D), lambda i: (0, 0)),
                            _mod_spec(layer, 2, tm), _vec_spec(layer, D),
                            _mod_spec(layer, 3, tm), _mod_spec(layer, 4, tm)],
        out_specs=[row, row],
        out_shape=[jax.ShapeDtypeStruct((ROWS, D), jnp.float32),
                   jax.ShapeDtypeStruct((ROWS, D), jnp.bfloat16)],
        compiler_params=_params(("arbitrary",)),
        name="out_proj",
    )(*xs, merged, w_out, mods, norm_mlp.reshape(N_LAYERS, 1, D), mods, mods)


MLP_TM = 512
MLP_TF = 1024


def _mlp_kernel(final, h_ref, w1_ref, w2_ref, x_ref, gate_ref, g_ref, shift_ref, scale_ref,
                *rest):
    if final:
        yc_ref, yl_ref, acc_ref = rest
    else:
        xo_ref, hn_ref, acc_ref = rest
    i = pl.program_id(0)
    j = pl.program_id(1)

    @pl.when(j == 0)
    def _():
        acc_ref[...] = jnp.zeros_like(acc_ref)

    t = jnp.maximum(_dot(h_ref[...], w1_ref[...]), 0.0)
    acc_ref[...] += _dot(_bf(t * t), w2_ref[...])

    @pl.when(j == pl.num_programs(1) - 1)
    def _():
        x = x_ref[...] + gate_ref[...] * acc_ref[...]
        if final:
            y = x * lax.rsqrt(jnp.mean(x * x, axis=-1, keepdims=True) + EPS) * g_ref[...]

            @pl.when(i < ROWS_CTX // MLP_TM)
            def _():
                yc_ref[...] = y

            @pl.when(i >= ROWS_CTX // MLP_TM)
            def _():
                yl_ref[...] = y
        else:
            xo_ref[...] = x
            hn_ref[...] = _bf(_rms_mod(x, g_ref[...], scale_ref[...], shift_ref[...]))


def _mlp(h2, x, w1, w2, mods, norm_next, layer, final):
    tm, tf = MLP_TM, MLP_TF
    row = pl.BlockSpec((tm, D), lambda i, j: (i, 0))
    nxt = layer if final else layer + 1
    g_spec = pl.BlockSpec((None, 1, D), lambda i, j: (0 if final else nxt, 0, 0))
    if final:
        out_specs = list(_split_specs(tm, D))
        out_shape = [jax.ShapeDtypeStruct((ROWS_CTX, D), jnp.float32),
                     jax.ShapeDtypeStruct((ROWS_LAT, D), jnp.float32)]
    else:
        out_specs = [row, row]
        out_shape = [jax.ShapeDtypeStruct((ROWS, D), jnp.float32),
                     jax.ShapeDtypeStruct((ROWS, D), jnp.bfloat16)]
    return pl.pallas_call(
        functools.partial(_mlp_kernel, final),
        grid=(ROWS // tm, D_FF // tf),
        in_specs=[row,
                  pl.BlockSpec((D, tf), lambda i, j: (0, j)),
                  pl.BlockSpec((tf, D), lambda i, j: (j, 0)),
                  row, _mod_spec(layer, 5, tm), g_spec,
                  _mod_spec(nxt, 0, tm), _mod_spec(nxt, 1, tm)],
        out_specs=out_specs,
        out_shape=out_shape,
        scratch_shapes=[pltpu.VMEM((tm, D), jnp.float32)],
        compiler_params=_params(("arbitrary", "arbitrary")),
        name="mlp_final" if final else "mlp",
    )(h2, w1, w2, x, mods, norm_next, mods, mods)


def kernel(x_prompt, x_sample, cache_win_k, cache_win_v, cache_diff_k, cache_diff_v, c, c_ctx, w_ada, b_ada, norm_mix, norm_mlp, w_in, conv_w, win_sink, lambda_q1, lambda_k1, lambda_q2, lambda_k2, diff_subln, w_branch_conv, w_branch_win, w_branch_diff, w_out, w_mlp1, w_mlp2, norm_final):
    xs = (x_prompt.reshape(ROWS_CTX, D), x_sample.reshape(ROWS_LAT, D))
    cvec = jnp.concatenate([c_ctx[None], c, jnp.zeros((N_MOD_ROWS - 1 - N_LAT_SEQ, D), jnp.float32)], axis=0)
    mods = _modulation(cvec, w_ada, b_ada).reshape(N_LAYERS * N_MOD_ROWS * N_MOD, 1, D)

    tables = _rope_tables()
    sink = win_sink.reshape(N_LAYERS * WIN_HEADS)
    lam_params = jnp.stack([lambda_q1, lambda_k1, lambda_q2, lambda_k2], axis=1)
    subln = diff_subln.reshape(N_LAYERS, 1, HEAD)
    ck_win = cache_win_k.reshape(N_LAT_SEQ, N_LAYERS, PAST, WIN_KV * HEAD)
    cv_win = cache_win_v.reshape(N_LAT_SEQ, N_LAYERS, PAST, WIN_KV * HEAD)
    ck_diff = cache_diff_k.reshape(N_LAT_SEQ, N_LAYERS, PAST, DIFF_HEADS * HEAD)
    cv_diff = cache_diff_v.reshape(N_LAT_SEQ, N_LAYERS, PAST, DIFF_HEADS * HEAD)
    norm_final_3d = norm_final.reshape(1, 1, D)
    norm_mix_3d = norm_mix.reshape(N_LAYERS, 1, D)

    h = _prologue(*xs, mods, norm_mix, 0)
    states = None
    for layer in range(N_LAYERS):
        in_job = functools.partial(_CastJob, w_in, layer, cbw=IN_CBW)
        ya, (w_attn,) = _conv_branch(h, w_in, conv_w, layer,
                                     [in_job(k0=OFF_ATTN // IN_CBW, nb=ATTN_COLS // IN_CBW)])
        qkv, states = _attn_proj(h, w_attn, tables, layer, states)
        yb_ctx, yc_ctx, (w_gate,) = _ctx_attn(qkv, sink, lam_params, subln, layer,
                                              [in_job(k0=OFF_GATE // IN_CBW, nb=3 * D // IN_CBW)])
        yb_lat, (wa, wb, wc, w_o) = _lat_win_attn(
            qkv, ck_win, cv_win, sink, layer,
            [_whole(w, layer) for w in (w_branch_conv, w_branch_win, w_branch_diff, w_out)])
        yc_lat, (w1, w2) = _lat_diff_attn(qkv, ck_diff, cv_diff, lam_params, subln, layer,
                                          [_whole(w_mlp1, layer), _whole(w_mlp2, layer)])
        merged = _merge(h, ya, yb_ctx, yb_lat, yc_ctx, yc_lat, w_gate, wa, wb, wc)
        x, h2 = _out_proj(xs, merged, w_o, mods, norm_mlp, layer)
        if layer == N_LAYERS - 1:
            y_ctx, y_lat = _mlp(h2, x, w1, w2, mods, norm_final_3d, layer, True)
        else:
            x, h = _mlp(h2, x, w1, w2, mods, norm_mix_3d, layer, False)
            xs = (x,)

    new_win_k, new_win_v, new_diff_k, new_diff_v = states
    return (y_ctx.reshape(N_CTX_SEQ, CTX_LEN, D), y_lat.reshape(N_LAT_SEQ, LAT_LEN, D),
            new_win_k.reshape(N_CTX_SEQ, N_LAYERS, CTX_LEN, WIN_KV, HEAD),
            new_win_v.reshape(N_CTX_SEQ, N_LAYERS, CTX_LEN, WIN_KV, HEAD),
            new_diff_k.reshape(N_CTX_SEQ, N_LAYERS, CTX_LEN, DIFF_HEADS, HEAD),
            new_diff_v.reshape(N_CTX_SEQ, N_LAYERS, CTX_LEN, DIFF_HEADS, HEAD))
```

```python
import functools
import math
from typing import NamedTuple

import jax
import jax.numpy as jnp
from jax import lax
from jax.experimental import pallas as pl
from jax.experimental.pallas import tpu as pltpu

D = 2048
N_CTX_SEQ = 16
CTX_LEN = 256
N_LAT_SEQ = 2
LAT_LEN = 2048
N_LAYERS = 2
PAST = 512
GRID_W = 64
HEAD = 128
CONV_DIM = 1024
WIN_HEADS = 8
WIN_KV = 2
WIN_GROUP = WIN_HEADS // WIN_KV
WINDOW = 128
QBLK = 128
DIFF_HEADS = 8
DIFF_QK = 64
D_FF = 4 * D
EPS = 1e-6
ROPE_BASE = 10000.0
NEG = -1e30

ROWS_CTX = N_CTX_SEQ * CTX_LEN
ROWS_LAT = N_LAT_SEQ * LAT_LEN
ROWS = ROWS_CTX + ROWS_LAT
N_MOD_ROWS = 8
N_MOD = 6

OFF_CONV = 0
OFF_ATTN = 3 * CONV_DIM
ATTN_COLS = (WIN_HEADS + 2 * WIN_KV) * HEAD + 3 * DIFF_HEADS * HEAD
OFF_GATE = OFF_ATTN + ATTN_COLS
IN_TOTAL = OFF_GATE + 3 * D
IN_CBW = 1536
A_WQ, A_WK, A_WV = 0, WIN_HEADS * HEAD, (WIN_HEADS + WIN_KV) * HEAD
A_DQ = (WIN_HEADS + 2 * WIN_KV) * HEAD
A_DK = A_DQ + DIFF_HEADS * HEAD
A_DV = A_DK + DIFF_HEADS * HEAD

VMEM_LIMIT_V7X = 60 * 1024 * 1024


def _params(sem, vmem=VMEM_LIMIT_V7X):
    return pltpu.CompilerParams(dimension_semantics=sem, vmem_limit_bytes=vmem)


def _bf(x):
    return x.astype(jnp.bfloat16)


def _dot(a, b):
    return jnp.dot(a, b, preferred_element_type=jnp.float32)


def _dot_nt(a, b):
    return lax.dot_general(a, b, (((1,), (1,)), ((), ())), preferred_element_type=jnp.float32)


def _sigmoid(x):
    return 1.0 / (1.0 + jnp.exp(-x))


def _mod_group(i, tm):
    n_ctx = ROWS_CTX // tm
    return jnp.where(i < n_ctx, 0, 1 + (i - n_ctx) // (LAT_LEN // tm))


def _mod_spec(layer, which, tm):
    def imap(i, *_):
        return (layer * N_MOD_ROWS * N_MOD + _mod_group(i, tm) * N_MOD + which, 0, 0)
    return pl.BlockSpec((None, 1, D), imap)


def _vec_spec(layer, width):
    return pl.BlockSpec((None, 1, width), lambda *_: (layer, 0, 0))


def _rms_mod(x, g, scale, shift):
    y = x * lax.rsqrt(jnp.mean(x * x, axis=-1, keepdims=True) + EPS) * g
    return y * (1.0 + scale) + shift


def _mod_kernel(c_ref, w_ref, b_ref, o_ref):
    c = c_ref[...]
    s = c * _sigmoid(c)
    o_ref[...] = _dot(_bf(s), _bf(w_ref[...])) + b_ref[...]


def _modulation(cvec, w_ada, b_ada):
    tn = 1024
    n = N_MOD * D
    return pl.pallas_call(
        _mod_kernel,
        grid=(N_LAYERS, n // tn),
        in_specs=[pl.BlockSpec((N_MOD_ROWS, D), lambda l, j: (0, 0)),
                  pl.BlockSpec((None, D, tn), lambda l, j: (l, 0, j)),
                  pl.BlockSpec((None, 1, tn), lambda l, j: (l, 0, j))],
        out_specs=pl.BlockSpec((None, N_MOD_ROWS, tn), lambda l, j: (l, 0, j)),
        out_shape=jax.ShapeDtypeStruct((N_LAYERS, N_MOD_ROWS, n), jnp.float32),
        compiler_params=_params(("parallel", "parallel")),
        name="modulation",
    )(cvec, w_ada, b_ada.reshape(N_LAYERS, 1, n))


def _split_specs(tm, width):
    n_ctx = ROWS_CTX // tm
    return (pl.BlockSpec((tm, width), lambda i, *_: (jnp.minimum(i, n_ctx - 1), 0)),
            pl.BlockSpec((tm, width), lambda i, *_: (jnp.maximum(i - n_ctx, 0), 0)))


def _pick_rows(tm, ctx_ref, lat_ref):
    return jnp.where(pl.program_id(0) < ROWS_CTX // tm, ctx_ref[...], lat_ref[...])


PRO_TM = 512


def _prologue_kernel(xc_ref, xl_ref, g_ref, shift_ref, scale_ref, h_ref):
    x = _pick_rows(PRO_TM, xc_ref, xl_ref)
    h_ref[...] = _bf(_rms_mod(x, g_ref[...], scale_ref[...], shift_ref[...]))


def _prologue(x_ctx, x_lat, mods, norm_mix, layer):
    tm = PRO_TM
    return pl.pallas_call(
        _prologue_kernel,
        grid=(ROWS // tm,),
        in_specs=[*_split_specs(tm, D),
                  _vec_spec(layer, D),
                  _mod_spec(layer, 0, tm),
                  _mod_spec(layer, 1, tm)],
        out_specs=pl.BlockSpec((tm, D), lambda i: (i, 0)),
        out_shape=jax.ShapeDtypeStruct((ROWS, D), jnp.bfloat16),
        compiler_params=_params(("arbitrary",)),
        name="prologue",
    )(x_ctx, x_lat, norm_mix.reshape(N_LAYERS, 1, D), mods, mods)


class _CastJob(NamedTuple):
    src: jax.Array
    layer: int
    k0: int
    nb: int
    cbw: int

    @property
    def rows(self):
        return self.src.shape[1]


def _whole(src, layer):
    return _CastJob(src, layer, 0, 1, src.shape[2])


def _cast_plumbing(jobs, n_steps, step_of):
    in_specs, args, out_specs, out_shapes = [], [], [], []
    for job in jobs:
        rows = job.rows // n_steps
        assert rows * n_steps == job.rows and rows % 16 == 0
        for k in range(job.nb):
            in_specs.append(pl.BlockSpec(
                (None, rows, job.cbw), lambda *g, job=job, k=k: (job.layer, step_of(*g), job.k0 + k)))
            args.append(job.src)
        out_specs.append(pl.BlockSpec((rows, job.nb * job.cbw), lambda *g: (step_of(*g), 0)))
        out_shapes.append(jax.ShapeDtypeStruct((job.rows, job.nb * job.cbw), jnp.bfloat16))
    return in_specs, args, out_specs, out_shapes


def _cast_plan(jobs):
    return tuple((job.nb, job.cbw) for job in jobs)


def _run_casts(plan, in_refs, out_refs):
    in_refs = iter(in_refs)
    for (nb, cbw), out_ref in zip(plan, out_refs):
        for k in range(nb):
            out_ref[:, k * cbw:(k + 1) * cbw] = _bf(next(in_refs)[...])


def _n_cast_inputs(plan):
    return sum(nb for nb, _ in plan)


CONV_TM = LAT_LEN
CONV_TN = 256


def _conv_kernel(plan, h_ref, wb_ref, wc_ref, wx_ref, cw_ref, *refs):
    n_in = _n_cast_inputs(plan)
    o_ref = refs[n_in]
    _run_casts(plan, refs[:n_in], refs[n_in + 1:])
    i = pl.program_id(0)
    h = h_ref[...]
    cb = _dot(h, _bf(wb_ref[...]))
    z = _dot(h, _bf(wc_ref[...])) * _dot(h, _bf(wx_ref[...]))
    seq_mask = jnp.where(i < ROWS_CTX // CONV_TM, CTX_LEN - 1, LAT_LEN - 1)
    pos = lax.broadcasted_iota(jnp.int32, z.shape, 0) & seq_mask
    z_prev = jnp.where(pos == 0, 0.0, pltpu.roll(z, 1, 0))
    z_next = jnp.where(pos == seq_mask, 0.0, pltpu.roll(z, CONV_TM - 1, 0))
    cw = cw_ref[...]
    conv = z_prev * cw[0:1, :] + z * cw[1:2, :] + z_next * cw[2:3, :]
    o_ref[...] = _bf(cb * conv)


def _conv_branch(h, w_in, conv_w, layer, jobs):
    nj = CONV_DIM // CONV_TN
    ni = ROWS // CONV_TM

    def wspec(group):
        return pl.BlockSpec((None, D, CONV_TN), lambda i, j: (layer, 0, group * nj + j))

    c_in, c_args, c_out, c_shapes = _cast_plumbing(jobs, ni * nj, lambda i, j: i * nj + j)
    out = pl.pallas_call(
        functools.partial(_conv_kernel, _cast_plan(jobs)),
        grid=(ni, nj),
        in_specs=[pl.BlockSpec((CONV_TM, D), lambda i, j: (i, 0)),
                  wspec(0), wspec(1), wspec(2),
                  pl.BlockSpec((None, 3, CONV_TN), lambda i, j: (layer, 0, j))] + c_in,
        out_specs=[pl.BlockSpec((CONV_TM, CONV_TN), lambda i, j: (i, j))] + c_out,
        out_shape=[jax.ShapeDtypeStruct((ROWS, CONV_DIM), jnp.bfloat16)] + c_shapes,
        compiler_params=_params(("arbitrary", "arbitrary")),
        name="conv_branch",
    )(h, w_in, w_in, w_in, conv_w, *c_args)
    return out[0], out[1:]


ATTN_TM = CTX_LEN
ATTN_CHUNK = 512
ROPE_WIN, ROPE_DIFF, ROPE_NONE = 0, 1, 2
_SLAB_KIND = ([ROPE_WIN] * (WIN_HEADS + WIN_KV) + [ROPE_NONE] * WIN_KV
              + [ROPE_DIFF] * (2 * DIFF_HEADS) + [ROPE_NONE] * DIFF_HEADS)
_ROPE_SHIFT = {ROPE_WIN: HEAD // 4, ROPE_DIFF: DIFF_QK // 4}
_STATE_COLS = ((A_WK, WIN_KV * HEAD), (A_WV, WIN_KV * HEAD),
               (A_DK, DIFF_HEADS * HEAD), (A_DV, DIFF_HEADS * HEAD))
DIFF_Q_SCALE = DIFF_QK ** -0.5 * math.log2(math.e)


def _rope_tables():
    t = jnp.arange(LAT_LEN, dtype=jnp.int32)
    rows = (t // GRID_W).astype(jnp.float32)[:, None]
    cols = (t % GRID_W).astype(jnp.float32)[:, None]
    lane = jnp.arange(HEAD, dtype=jnp.int32)[None, :]
    out = []
    for kind in (ROPE_WIN, ROPE_DIFF):
        s = _ROPE_SHIFT[kind]
        within = lane % (4 * s)
        freq = (within % s).astype(jnp.float32)
        inv = ROPE_BASE ** (-freq / s)
        pos = jnp.where(within < 2 * s, rows, cols)
        ang = pos * inv
        first = (within % (2 * s)) < s
        cos, sin = jnp.cos(ang), jnp.sin(ang)
        out += [cos, jnp.where(first, -sin, 0.0), jnp.where(first, 0.0, sin)]
    return jnp.stack(out).astype(jnp.float32)


def _attn_proj_kernel(first, h_ref, w_ref, tab_ref, *refs):
    qkv_ref, *state_refs = refs[-(1 + len(_STATE_COLS)):]
    is_ctx = pl.program_id(0) < N_CTX_SEQ

    def body(ctx):
        h = h_ref[...]
        if ctx and first:
            for ref in state_refs:
                ref[1:] = jnp.zeros((N_LAYERS - 1,) + ref.shape[1:], jnp.float32)
        for c0 in range(0, ATTN_COLS, ATTN_CHUNK):
            acc = _dot(h, w_ref[:, c0:c0 + ATTN_CHUNK])
            for s0 in range(0, ATTN_CHUNK, HEAD):
                col = c0 + s0
                x = acc[:, s0:s0 + HEAD]
                kind = _SLAB_KIND[col // HEAD]
                if ctx:
                    for ref, (src, width) in zip(state_refs, _STATE_COLS):
                        if src <= col < src + width:
                            dst = col - src
                            if first:
                                ref[0, :, dst:dst + HEAD] = x
                            else:
                                ref[:, dst:dst + HEAD] = x
                elif kind != ROPE_NONE:
                    s = _ROPE_SHIFT[kind]
                    x = (x * tab_ref[3 * kind] + pltpu.roll(x, HEAD - s, 1) * tab_ref[3 * kind + 1]
                         + pltpu.roll(x, s, 1) * tab_ref[3 * kind + 2])
                if A_DQ <= col < A_DK:
                    x = x * DIFF_Q_SCALE
                qkv_ref[:, col:col + HEAD] = _bf(x)

    pl.when(is_ctx)(lambda: body(True))
    pl.when(jnp.logical_not(is_ctx))(lambda: body(False))


def _attn_proj(h, w_attn, tables, layer, prev_states):
    tm = ATTN_TM
    per_lat = LAT_LEN // tm
    first = prev_states is None

    def state_spec(width):
        if first:
            return pl.BlockSpec((None, N_LAYERS, CTX_LEN, width),
                                lambda i: (jnp.minimum(i, N_CTX_SEQ - 1), 0, 0, 0))
        return pl.BlockSpec((None, None, CTX_LEN, width),
                            lambda i: (jnp.minimum(i, N_CTX_SEQ - 1), layer, 0, 0))

    in_specs = [pl.BlockSpec((tm, D), lambda i: (i, 0)),
                pl.BlockSpec((D, ATTN_COLS), lambda i: (0, 0), pipeline_mode=pl.Buffered(1)),
                pl.BlockSpec((6, tm, HEAD), lambda i: (0, jnp.maximum(i - N_CTX_SEQ, 0) % per_lat, 0))]
    args = [h, w_attn, tables]
    aliases = {}
    if not first:
        for k, st in enumerate(prev_states):
            in_specs.append(pl.BlockSpec(memory_space=pl.ANY))
            aliases[len(args)] = 1 + k
            args.append(st)
    out = pl.pallas_call(
        functools.partial(_attn_proj_kernel, first),
        grid=(ROWS // tm,),
        in_specs=in_specs,
        out_specs=[pl.BlockSpec((tm, ATTN_COLS), lambda i: (i, 0))]
                  + [state_spec(width) for _, width in _STATE_COLS],
        out_shape=[jax.ShapeDtypeStruct((ROWS, ATTN_COLS), jnp.bfloat16)]
                  + [jax.ShapeDtypeStruct((N_CTX_SEQ, N_LAYERS, CTX_LEN, width), jnp.float32)
                     for _, width in _STATE_COLS],
        input_output_aliases=aliases,
        compiler_params=_params(("arbitrary",)),
        name="attn_proj",
    )(*args)
    return out[0], out[1:]


def _slab(ref, col0, idx):
    return ref[:, col0 + idx * HEAD:col0 + (idx + 1) * HEAD]


def _with_ones(v):
    return jnp.concatenate([v, jnp.ones_like(v)], axis=1)


def _pipelined(items, scores, finish):
    s = scores(items[0])
    for r, item in enumerate(items):
        s_next = scores(items[r + 1]) if r + 1 < len(items) else None
        finish(item, s)
        s = s_next


def _win_scores(q_ref, col0, g, k, bias):
    q = jnp.concatenate([_slab(q_ref, col0, g * WIN_GROUP + hh) for hh in range(WIN_GROUP)], axis=0)
    s = _dot_nt(q, k) * (HEAD ** -0.5)
    return s if bias is None else s + bias


def _win_finish(layer, g, s, v, sink_ref, o_ref):
    rows = s.shape[0] // WIN_GROUP
    sink = jnp.concatenate(
        [jnp.full((rows, 1), sink_ref[layer * WIN_HEADS + g * WIN_GROUP + hh], jnp.float32)
         for hh in range(WIN_GROUP)], axis=0)
    m = jnp.maximum(jnp.max(s, axis=-1, keepdims=True), sink)
    r = _dot(_bf(jnp.exp(s - m)), _with_ones(v))
    o = r[:, :HEAD] / (r[:, HEAD:HEAD + 1] + jnp.exp(sink - m))
    for hh in range(WIN_GROUP):
        head = g * WIN_GROUP + hh
        o_ref[:, head * HEAD:(head + 1) * HEAD] = _bf(o[hh * rows:(hh + 1) * rows])


def _lat_win_kernel(layer, plan, q_ref, kc_ref, vc_ref, kseq_ref, vseq_ref, sink_ref, *refs):
    n_in = _n_cast_inputs(plan)
    o_ref = refs[n_in]
    _run_casts(plan, refs[:n_in], refs[n_in + 1:])
    n = pl.program_id(1)
    start = pl.multiple_of(jnp.clip((n - 1) * QBLK, 0, LAT_LEN - 3 * QBLK), QBLK)
    kb = kseq_ref[pl.ds(start, 3 * QBLK), :]
    vb = vseq_ref[pl.ds(start, 3 * QBLK), :]
    shape = (QBLK, PAST + 3 * QBLK)
    col = lax.broadcasted_iota(jnp.int32, shape, 1)
    qpos = n * QBLK + lax.broadcasted_iota(jnp.int32, shape, 0)
    kpos = start + col - PAST
    bias = jnp.where((col < PAST) | (jnp.abs(qpos - kpos) <= WINDOW), 0.0, NEG)
    bias = jnp.concatenate([bias] * WIN_GROUP, axis=0)

    def keys(ref, band, g):
        return jnp.concatenate([_bf(_slab(ref, 0, g)), band[:, g * HEAD:(g + 1) * HEAD]], axis=0)

    _pipelined(list(range(WIN_KV)),
               lambda g: _win_scores(q_ref, 0, g, keys(kc_ref, kb, g), bias),
               lambda g, s: _win_finish(layer, g, s, keys(vc_ref, vb, g), sink_ref, o_ref))


def _lat_win_attn(qkv, cache_k, cache_v, sink, layer, jobs):
    nb = LAT_LEN // QBLK
    kvw = WIN_KV * HEAD
    cache_spec = pl.BlockSpec((None, None, PAST, kvw), lambda b, n: (b, layer, 0, 0))

    def seq_spec(col0):
        return pl.BlockSpec((LAT_LEN, kvw), lambda b, n: (ROWS_CTX // LAT_LEN + b, col0 // kvw))

    c_in, c_args, c_out, c_shapes = _cast_plumbing(jobs, N_LAT_SEQ * nb, lambda b, n: b * nb + n)
    out = pl.pallas_call(
        functools.partial(_lat_win_kernel, layer, _cast_plan(jobs)),
        grid=(N_LAT_SEQ, nb),
        in_specs=[pl.BlockSpec((QBLK, WIN_HEADS * HEAD), lambda b, n: (ROWS_CTX // QBLK + b * nb + n, 0)),
                  cache_spec, cache_spec, seq_spec(A_WK), seq_spec(A_WV),
                  pl.BlockSpec(memory_space=pltpu.SMEM)] + c_in,
        out_specs=[pl.BlockSpec((QBLK, WIN_HEADS * HEAD), lambda b, n: (b * nb + n, 0))] + c_out,
        out_shape=[jax.ShapeDtypeStruct((ROWS_LAT, WIN_HEADS * HEAD), jnp.bfloat16)] + c_shapes,
        compiler_params=_params(("arbitrary", "arbitrary")),
        name="lat_win_attn",
    )(qkv, cache_k, cache_v, qkv, qkv, sink, *c_args)
    return out[0], out[1:]


def _lam_init(layer):
    return 0.8 - 0.6 * math.exp(-0.3 * layer)


def _diff_lambda(layer, lam_ref):
    lp = lam_ref[...]
    return (jnp.exp(jnp.sum(lp[0:1] * lp[1:2], axis=-1, keepdims=True))
            - jnp.exp(jnp.sum(lp[2:3] * lp[3:4], axis=-1, keepdims=True)) + _lam_init(layer))


def _diff_scores(q, k):
    first = lax.broadcasted_iota(jnp.int32, q.shape, 1) < DIFF_QK
    zero = jnp.zeros_like(q)
    return _dot_nt(jnp.concatenate([jnp.where(first, q, zero), jnp.where(first, zero, q)], axis=0), k)


def _diff_finish(layer, s, v_ones, lam, g):
    tq = s.shape[0] // 2
    e = _bf(jnp.exp2(s - jnp.max(s, axis=-1, keepdims=True)))
    r = _dot(e, v_ones)
    o = (r[:tq, :HEAD] * (1.0 / r[:tq, HEAD:HEAD + 1])
         - r[tq:, :HEAD] * (lam / r[tq:, HEAD:HEAD + 1]))
    o = o * lax.rsqrt(jnp.mean(o * o, axis=-1, keepdims=True) + EPS) * g
    return _bf(o * (1.0 - _lam_init(layer)))


def _ctx_attn_kernel(layer, plan, qkv_ref, sink_ref, lam_ref, g_ref, *refs):
    n_in = _n_cast_inputs(plan)
    yb_ref, yc_ref = refs[n_in:n_in + 2]
    _run_casts(plan, refs[:n_in], refs[n_in + 2:])
    lam = _diff_lambda(layer, lam_ref)
    stages = [("win", g) for g in range(WIN_KV)] + [("diff", h) for h in range(DIFF_HEADS)]

    def scores(stage):
        kind, idx = stage
        if kind == "win":
            return _win_scores(qkv_ref, A_WQ, idx, _slab(qkv_ref, A_WK, idx), None)
        return _diff_scores(_slab(qkv_ref, A_DQ, idx), _slab(qkv_ref, A_DK, idx))

    def finish(stage, s):
        kind, idx = stage
        if kind == "win":
            _win_finish(layer, idx, s, _slab(qkv_ref, A_WV, idx), sink_ref, yb_ref)
        else:
            yc_ref[:, idx * HEAD:(idx + 1) * HEAD] = _diff_finish(
                layer, s, _with_ones(_slab(qkv_ref, A_DV, idx)), lam, g_ref[...])

    _pipelined(stages, scores, finish)


def _ctx_attn(qkv, sink, lam_params, subln, layer, jobs):
    out = pl.BlockSpec((CTX_LEN, WIN_HEADS * HEAD), lambda b: (b, 0))
    c_in, c_args, c_out, c_shapes = _cast_plumbing(jobs, N_CTX_SEQ, lambda b: b)
    res = pl.pallas_call(
        functools.partial(_ctx_attn_kernel, layer, _cast_plan(jobs)),
        grid=(N_CTX_SEQ,),
        in_specs=[pl.BlockSpec((CTX_LEN, ATTN_COLS), lambda b: (b, 0)),
                  pl.BlockSpec(memory_space=pltpu.SMEM),
                  pl.BlockSpec((None, 4, DIFF_QK), lambda b: (layer, 0, 0)),
                  _vec_spec(layer, HEAD)] + c_in,
        out_specs=[out, out] + c_out,
        out_shape=[jax.ShapeDtypeStruct((ROWS_CTX, WIN_HEADS * HEAD), jnp.bfloat16),
                   jax.ShapeDtypeStruct((ROWS_CTX, DIFF_HEADS * HEAD), jnp.bfloat16)] + c_shapes,
        compiler_params=_params(("arbitrary",)),
        name="ctx_attn",
    )(qkv, sink, lam_params, subln, *c_args)
    return res[0], res[1], res[2:]


LAT_DIFF_TQ = 2048
LAT_DIFF_SUB = 256


def _lat_diff_kernel(layer, plan, q_ref, kc_ref, vc_ref, k_ref, v_ref, lam_ref, g_ref, *refs):
    n_in = _n_cast_inputs(plan)
    o_ref = refs[n_in]
    kall_ref, vall_ref = refs[-2:]

    @pl.when(pl.program_id(2) == 0)
    def _():
        kall_ref[0:PAST] = _bf(kc_ref[...])
        kall_ref[PAST:] = k_ref[...]
        vall_ref[0:PAST] = _with_ones(_bf(vc_ref[...]))
        vall_ref[PAST:] = _with_ones(v_ref[...])

    _run_casts(plan, refs[:n_in], refs[n_in + 1:-2])
    lam = _diff_lambda(layer, lam_ref)

    def finish(r0, s):
        o_ref[r0:r0 + LAT_DIFF_SUB] = _diff_finish(layer, s, vall_ref[...], lam, g_ref[...])

    _pipelined(list(range(0, LAT_DIFF_TQ, LAT_DIFF_SUB)),
               lambda r0: _diff_scores(q_ref[r0:r0 + LAT_DIFF_SUB], kall_ref[...]), finish)


def _lat_diff_attn(qkv, cache_k, cache_v, lam_params, subln, layer, jobs):
    tq = LAT_DIFF_TQ
    nq = LAT_LEN // tq
    cache_spec = pl.BlockSpec((None, None, PAST, HEAD), lambda b, h, t: (b, layer, 0, h))

    def seq_spec(col0):
        return pl.BlockSpec((LAT_LEN, HEAD), lambda b, h, t: (ROWS_CTX // LAT_LEN + b, col0 // HEAD + h))

    c_in, c_args, c_out, c_shapes = _cast_plumbing(
        jobs, N_LAT_SEQ * DIFF_HEADS * nq, lambda b, h, t: (b * DIFF_HEADS + h) * nq + t)
    out = pl.pallas_call(
        functools.partial(_lat_diff_kernel, layer, _cast_plan(jobs)),
        grid=(N_LAT_SEQ, DIFF_HEADS, nq),
        in_specs=[pl.BlockSpec((tq, HEAD), lambda b, h, t: (ROWS_CTX // tq + b * nq + t, A_DQ // HEAD + h)),
                  cache_spec, cache_spec, seq_spec(A_DK), seq_spec(A_DV),
                  pl.BlockSpec((None, 4, DIFF_QK), lambda b, h, t: (layer, 0, 0)),
                  _vec_spec(layer, HEAD)] + c_in,
        out_specs=[pl.BlockSpec((tq, HEAD), lambda b, h, t: (b * nq + t, h))] + c_out,
        out_shape=[jax.ShapeDtypeStruct((ROWS_LAT, DIFF_HEADS * HEAD), jnp.bfloat16)] + c_shapes,
        scratch_shapes=[pltpu.VMEM((PAST + LAT_LEN, HEAD), jnp.bfloat16),
                        pltpu.VMEM((PAST + LAT_LEN, 2 * HEAD), jnp.bfloat16)],
        compiler_params=_params(("arbitrary", "arbitrary", "arbitrary")),
        name="lat_diff_attn",
    )(qkv, cache_k, cache_v, qkv, qkv, lam_params, subln, *c_args)
    return out[0], out[1:]


MERGE_TM = 1024
MERGE_TN = 512


def _merge_kernel(h_ref, ya_ref, ybc_ref, ybl_ref, ycc_ref, ycl_ref,
                  wga_ref, wgb_ref, wgc_ref, wa_ref, wb_ref, wc_ref, o_ref):
    h = h_ref[...]
    yb = _pick_rows(MERGE_TM, ybc_ref, ybl_ref)
    yc = _pick_rows(MERGE_TM, ycc_ref, ycl_ref)
    m = _sigmoid(_dot(h, wga_ref[...])) * _dot(ya_ref[...], wa_ref[...])
    m += _sigmoid(_dot(h, wgb_ref[...])) * _dot(yb, wb_ref[...])
    m += _sigmoid(_dot(h, wgc_ref[...])) * _dot(yc, wc_ref[...])
    o_ref[...] = _bf(m)


def _merge(h, ya, yb_ctx, yb_lat, yc_ctx, yc_lat, w_gate, wa, wb, wc):
    tm, tn = MERGE_TM, MERGE_TN
    nj = D // tn

    def gate_spec(branch):
        return pl.BlockSpec((D, tn), lambda i, j: (0, branch * nj + j))

    w_spec = pl.BlockSpec((CONV_DIM, tn), lambda i, j: (0, j))
    return pl.pallas_call(
        _merge_kernel,
        grid=(ROWS // tm, nj),
        in_specs=[pl.BlockSpec((tm, D), lambda i, j: (i, 0)),
                  pl.BlockSpec((tm, CONV_DIM), lambda i, j: (i, 0)),
                  *_split_specs(tm, WIN_HEADS * HEAD), *_split_specs(tm, DIFF_HEADS * HEAD),
                  gate_spec(0), gate_spec(1), gate_spec(2), w_spec, w_spec, w_spec],
        out_specs=pl.BlockSpec((tm, tn), lambda i, j: (i, j)),
        out_shape=jax.ShapeDtypeStruct((ROWS, D), jnp.bfloat16),
        compiler_params=_params(("parallel", "arbitrary")),
        name="merge",
    )(h, ya, yb_ctx, yb_lat, yc_ctx, yc_lat, w_gate, w_gate, w_gate, wa, wb, wc)


OUT_TM = 512


def _out_proj_kernel(n_x, *refs):
    x_refs = refs[:n_x]
    m_ref, w_ref, gate_ref, g_ref, shift_ref, scale_ref, xo_ref, h_ref = refs[n_x:]
    x = x_refs[0][...] if n_x == 1 else _pick_rows(OUT_TM, *x_refs)
    x = x + gate_ref[...] * _dot(m_ref[...], w_ref[...])
    xo_ref[...] = x
    h_ref[...] = _bf(_rms_mod(x, g_ref[...], scale_ref[...], shift_ref[...]))


def _out_proj(xs, merged, w_out, mods, norm_mlp, layer):
    tm = OUT_TM
    row = pl.BlockSpec((tm, D), lambda i: (i, 0))
    x_specs = [row] if len(xs) == 1 else list(_split_specs(tm, D))
    return pl.pallas_call(
        functools.partial(_out_proj_kernel, len(xs)),
        grid=(ROWS // tm,),
        in_specs=x_specs + [row,
                            pl.BlockSpec((D, D), lambda i: (0, 0)),
                            _mod_spec(layer, 2, tm), _vec_spec(layer, D),
                            _mod_spec(layer, 3, tm), _mod_spec(layer, 4, tm)],
        out_specs=[row, row],
        out_shape=[jax.ShapeDtypeStruct((ROWS, D), jnp.float32),
                   jax.ShapeDtypeStruct((ROWS, D), jnp.bfloat16)],
        compiler_params=_params(("arbitrary",)),
        name="out_proj",
    )(*xs, merged, w_out, mods, norm_mlp.reshape(N_LAYERS, 1, D), mods, mods)


MLP_TM = 512
MLP_TF = 1024


def _mlp_kernel(final, h_ref, w1_ref, w2_ref, x_ref, gate_ref, g_ref, shift_ref, scale_ref,
                *rest):
    if final:
        yc_ref, yl_ref, acc_ref = rest
    else:
        xo_ref, hn_ref, acc_ref = rest
    i = pl.program_id(0)
    j = pl.program_id(1)

    @pl.when(j == 0)
    def _():
        acc_ref[...] = jnp.zeros_like(acc_ref)

    t = jnp.maximum(_dot(h_ref[...], w1_ref[...]), 0.0)
    acc_ref[...] += _dot(_bf(t * t), w2_ref[...])

    @pl.when(j == pl.num_programs(1) - 1)
    def _():
        x = x_ref[...] + gate_ref[...] * acc_ref[...]
        if final:
            y = x * lax.rsqrt(jnp.mean(x * x, axis=-1, keepdims=True) + EPS) * g_ref[...]

            @pl.when(i < ROWS_CTX // MLP_TM)
            def _():
                yc_ref[...] = y

            @pl.when(i >= ROWS_CTX // MLP_TM)
            def _():
                yl_ref[...] = y
        else:
            xo_ref[...] = x
            hn_ref[...] = _bf(_rms_mod(x, g_ref[...], scale_ref[...], shift_ref[...]))


def _mlp(h2, x, w1, w2, mods, norm_next, layer, final):
    tm, tf = MLP_TM, MLP_TF
    row = pl.BlockSpec((tm, D), lambda i, j: (i, 0))
    nxt = layer if final else layer + 1
    g_spec = pl.BlockSpec((None, 1, D), lambda i, j: (0 if final else nxt, 0, 0))
    if final:
        out_specs = list(_split_specs(tm, D))
        out_shape = [jax.ShapeDtypeStruct((ROWS_CTX, D), jnp.float32),
                     jax.ShapeDtypeStruct((ROWS_LAT, D), jnp.float32)]
    else:
        out_specs = [row, row]
        out_shape = [jax.ShapeDtypeStruct((ROWS, D), jnp.float32),
                     jax.ShapeDtypeStruct((ROWS, D), jnp.bfloat16)]
    return pl.pallas_call(
        functools.partial(_mlp_kernel, final),
        grid=(ROWS // tm, D_FF // tf),
        in_specs=[row,
                  pl.BlockSpec((D, tf), lambda i, j: (0, j)),
                  pl.BlockSpec((tf, D), lambda i, j: (j, 0)),
                  row, _mod_spec(layer, 5, tm), g_spec,
                  _mod_spec(nxt, 0, tm), _mod_spec(nxt, 1, tm)],
        out_specs=out_specs,
        out_shape=out_shape,
        scratch_shapes=[pltpu.VMEM((tm, D), jnp.float32)],
        compiler_params=_params(("arbitrary", "arbitrary")),
        name="mlp_final" if final else "mlp",
    )(h2, w1, w2, x, mods, norm_next, mods, mods)


def kernel(x_prompt, x_sample, cache_win_k, cache_win_v, cache_diff_k, cache_diff_v, c, c_ctx, w_ada, b_ada, norm_mix, norm_mlp, w_in, conv_w, win_sink, lambda_q1, lambda_k1, lambda_q2, lambda_k2, diff_subln, w_branch_conv, w_branch_win, w_branch_diff, w_out, w_mlp1, w_mlp2, norm_final):
    xs = (x_prompt.reshape(ROWS_CTX, D), x_sample.reshape(ROWS_LAT, D))
    cvec = jnp.concatenate([c_ctx[None], c, jnp.zeros((N_MOD_ROWS - 1 - N_LAT_SEQ, D), jnp.float32)], axis=0)
    mods = _modulation(cvec, w_ada, b_ada).reshape(N_LAYERS * N_MOD_ROWS * N_MOD, 1, D)

    tables = _rope_tables()
    sink = win_sink.reshape(N_LAYERS * WIN_HEADS)
    lam_params = jnp.stack([lambda_q1, lambda_k1, lambda_q2, lambda_k2], axis=1)
    subln = diff_subln.reshape(N_LAYERS, 1, HEAD)
    ck_win = cache_win_k.reshape(N_LAT_SEQ, N_LAYERS, PAST, WIN_KV * HEAD)
    cv_win = cache_win_v.reshape(N_LAT_SEQ, N_LAYERS, PAST, WIN_KV * HEAD)
    ck_diff = cache_diff_k.reshape(N_LAT_SEQ, N_LAYERS, PAST, DIFF_HEADS * HEAD)
    cv_diff = cache_diff_v.reshape(N_LAT_SEQ, N_LAYERS, PAST, DIFF_HEADS * HEAD)
    norm_final_3d = norm_final.reshape(1, 1, D)
    norm_mix_3d = norm_mix.reshape(N_LAYERS, 1, D)

    h = _prologue(*xs, mods, norm_mix, 0)
    states = None
    for layer in range(N_LAYERS):
        in_job = functools.partial(_CastJob, w_in, layer, cbw=IN_CBW)
        ya, (w_attn,) = _conv_branch(h, w_in, conv_w, layer,
                                     [in_job(k0=OFF_ATTN // IN_CBW, nb=ATTN_COLS // IN_CBW)])
        qkv, states = _attn_proj(h, w_attn, tables, layer, states)
        yb_ctx, yc_ctx, (w_gate,) = _ctx_attn(qkv, sink, lam_params, subln, layer,
                                              [in_job(k0=OFF_GATE // IN_CBW, nb=3 * D // IN_CBW)])
        yb_lat, (wa, wb, wc, w_o) = _lat_win_attn(
            qkv, ck_win, cv_win, sink, layer,
            [_whole(w, layer) for w in (w_branch_conv, w_branch_win, w_branch_diff, w_out)])
        yc_lat, (w1, w2) = _lat_diff_attn(qkv, ck_diff, cv_diff, lam_params, subln, layer,
                                          [_whole(w_mlp1, layer), _whole(w_mlp2, layer)])
        merged = _merge(h, ya, yb_ctx, yb_lat, yc_ctx, yc_lat, w_gate, wa, wb, wc)
        x, h2 = _out_proj(xs, merged, w_o, mods, norm_mlp, layer)
        if layer == N_LAYERS - 1:
            y_ctx, y_lat = _mlp(h2, x, w1, w2, mods, norm_final_3d, layer, True)
        else:
            x, h = _mlp(h2, x, w1, w2, mods, norm_mix_3d, layer, False)
            xs = (x,)

    new_win_k, new_win_v, new_diff_k, new_diff_v = states
    return (y_ctx.reshape(N_CTX_SEQ, CTX_LEN, D), y_lat.reshape(N_LAT_SEQ, LAT_LEN, D),
            new_win_k.reshape(N_CTX_SEQ, N_LAYERS, CTX_LEN, WIN_KV, HEAD),
            new_win_v.reshape(N_CTX_SEQ, N_LAYERS, CTX_LEN, WIN_KV, HEAD),
            new_diff_k.reshape(N_CTX_SEQ, N_LAYERS, CTX_LEN, DIFF_HEADS, HEAD),
            new_diff_v.reshape(N_CTX_SEQ, N_LAYERS, CTX_LEN, DIFF_HEADS, HEAD))
```

```python
import functools
import math
from typing import NamedTuple

import jax
import jax.numpy as jnp
import numpy as np
from jax import lax
from jax.experimental import pallas as pl
from jax.experimental.pallas import tpu as pltpu

D = 2048
N_CTX_SEQ = 16
CTX_LEN = 256
N_LAT_SEQ = 2
LAT_LEN = 2048
N_LAYERS = 2
PAST = 512
GRID_W = 64
HEAD = 128
CONV_DIM = 1024
WIN_HEADS = 8
WIN_KV = 2
WIN_GROUP = WIN_HEADS // WIN_KV
WINDOW = 128
QBLK = 128
DIFF_HEADS = 8
DIFF_QK = 64
D_FF = 4 * D
EPS = 1e-6
ROPE_BASE = 10000.0
NEG = -1e30

ROWS_CTX = N_CTX_SEQ * CTX_LEN
ROWS_LAT = N_LAT_SEQ * LAT_LEN
ROWS = ROWS_CTX + ROWS_LAT
N_MOD_ROWS = 8
N_MOD = 6

OFF_CONV = 0
OFF_ATTN = 3 * CONV_DIM
ATTN_COLS = (WIN_HEADS + 2 * WIN_KV) * HEAD + 3 * DIFF_HEADS * HEAD
OFF_GATE = OFF_ATTN + ATTN_COLS
IN_TOTAL = OFF_GATE + 3 * D
IN_CBW = 1536
A_WQ, A_WK, A_WV = 0, WIN_HEADS * HEAD, (WIN_HEADS + WIN_KV) * HEAD
A_DQ = (WIN_HEADS + 2 * WIN_KV) * HEAD
A_DK = A_DQ + DIFF_HEADS * HEAD
A_DV = A_DK + DIFF_HEADS * HEAD

VMEM_LIMIT_V7X = 60 * 1024 * 1024


def _params(sem, vmem=VMEM_LIMIT_V7X):
    return pltpu.CompilerParams(dimension_semantics=sem, vmem_limit_bytes=vmem)


def _bf(x):
    return x.astype(jnp.bfloat16)


def _dot(a, b):
    return jnp.dot(a, b, preferred_element_type=jnp.float32)


def _dot_nt(a, b):
    return lax.dot_general(a, b, (((1,), (1,)), ((), ())), preferred_element_type=jnp.float32)


def _sigmoid(x):
    return 1.0 / (1.0 + jnp.exp(-x))


def _mod_group(i, tm):
    n_ctx = ROWS_CTX // tm
    return jnp.where(i < n_ctx, 0, 1 + (i - n_ctx) // (LAT_LEN // tm))


def _mod_spec(layer, which, tm):
    def imap(i, *_):
        return (layer * N_MOD_ROWS * N_MOD + _mod_group(i, tm) * N_MOD + which, 0, 0)
    return pl.BlockSpec((None, 1, D), imap)


def _vec_spec(layer, width):
    return pl.BlockSpec((None, 1, width), lambda *_: (layer, 0, 0))


def _rms_mod(x, g, scale, shift):
    y = x * lax.rsqrt(jnp.mean(x * x, axis=-1, keepdims=True) + EPS) * g
    return y * (1.0 + scale) + shift


def _mod_kernel(c_ref, w_ref, b_ref, o_ref):
    c = c_ref[...]
    s = c * _sigmoid(c)
    o_ref[...] = _dot(_bf(s), _bf(w_ref[...])) + b_ref[...]


def _modulation(cvec, w_ada, b_ada):
    tn = 1024
    n = N_MOD * D
    return pl.pallas_call(
        _mod_kernel,
        grid=(N_LAYERS, n // tn),
        in_specs=[pl.BlockSpec((N_MOD_ROWS, D), lambda l, j: (0, 0)),
                  pl.BlockSpec((None, D, tn), lambda l, j: (l, 0, j)),
                  pl.BlockSpec((None, 1, tn), lambda l, j: (l, 0, j))],
        out_specs=pl.BlockSpec((None, N_MOD_ROWS, tn), lambda l, j: (l, 0, j)),
        out_shape=jax.ShapeDtypeStruct((N_LAYERS, N_MOD_ROWS, n), jnp.float32),
        compiler_params=_params(("parallel", "parallel")),
        name="modulation",
    )(cvec, w_ada, b_ada.reshape(N_LAYERS, 1, n))


def _split_specs(tm, width):
    n_ctx = ROWS_CTX // tm
    return (pl.BlockSpec((tm, width), lambda i, *_: (jnp.minimum(i, n_ctx - 1), 0)),
            pl.BlockSpec((tm, width), lambda i, *_: (jnp.maximum(i - n_ctx, 0), 0)))


def _pick_rows(tm, ctx_ref, lat_ref):
    return jnp.where(pl.program_id(0) < ROWS_CTX // tm, ctx_ref[...], lat_ref[...])


PRO_TM = 1024


def _prologue_kernel(xc_ref, xl_ref, g_ref, shift_ref, scale_ref, h_ref):
    x = _pick_rows(PRO_TM, xc_ref, xl_ref)
    h_ref[...] = _bf(_rms_mod(x, g_ref[...], scale_ref[...], shift_ref[...]))


def _prologue(x_ctx, x_lat, mods, norm_mix, layer):
    tm = PRO_TM
    return pl.pallas_call(
        _prologue_kernel,
        grid=(ROWS // tm,),
        in_specs=[*_split_specs(tm, D),
                  _vec_spec(layer, D),
                  _mod_spec(layer, 0, tm),
                  _mod_spec(layer, 1, tm)],
        out_specs=pl.BlockSpec((tm, D), lambda i: (i, 0)),
        out_shape=jax.ShapeDtypeStruct((ROWS, D), jnp.bfloat16),
        compiler_params=_params(("arbitrary",)),
        name="prologue",
    )(x_ctx, x_lat, norm_mix.reshape(N_LAYERS, 1, D), mods, mods)


class _CastJob(NamedTuple):
    src: jax.Array
    layer: int
    k0: int
    nb: int
    cbw: int

    @property
    def rows(self):
        return self.src.shape[1]


def _whole(src, layer):
    return _CastJob(src, layer, 0, 1, src.shape[2])


def _cast_plumbing(jobs, n_steps, step_of):
    in_specs, args, out_specs, out_shapes = [], [], [], []
    for job in jobs:
        rows = job.rows // n_steps
        assert rows * n_steps == job.rows and rows % 16 == 0
        for k in range(job.nb):
            in_specs.append(pl.BlockSpec(
                (None, rows, job.cbw), lambda *g, job=job, k=k: (job.layer, step_of(*g), job.k0 + k)))
            args.append(job.src)
        out_specs.append(pl.BlockSpec((rows, job.nb * job.cbw), lambda *g: (step_of(*g), 0)))
        out_shapes.append(jax.ShapeDtypeStruct((job.rows, job.nb * job.cbw), jnp.bfloat16))
    return in_specs, args, out_specs, out_shapes


def _cast_plan(jobs):
    return tuple((job.nb, job.cbw) for job in jobs)


def _run_casts(plan, in_refs, out_refs):
    in_refs = iter(in_refs)
    for (nb, cbw), out_ref in zip(plan, out_refs):
        for k in range(nb):
            out_ref[:, k * cbw:(k + 1) * cbw] = _bf(next(in_refs)[...])


def _n_cast_inputs(plan):
    return sum(nb for nb, _ in plan)


CONV_TM = LAT_LEN
CONV_TN = 256


def _conv_kernel(plan, h_ref, wb_ref, wc_ref, wx_ref, cw_ref, *refs):
    n_in = _n_cast_inputs(plan)
    o_ref = refs[n_in]
    _run_casts(plan, refs[:n_in], refs[n_in + 1:])
    i = pl.program_id(0)
    h = h_ref[...]
    cb = _dot(h, _bf(wb_ref[...]))
    z = _dot(h, _bf(wc_ref[...])) * _dot(h, _bf(wx_ref[...]))
    seq_mask = jnp.where(i < ROWS_CTX // CONV_TM, CTX_LEN - 1, LAT_LEN - 1)
    pos = lax.broadcasted_iota(jnp.int32, z.shape, 0) & seq_mask
    z_prev = jnp.where(pos == 0, 0.0, pltpu.roll(z, 1, 0))
    z_next = jnp.where(pos == seq_mask, 0.0, pltpu.roll(z, CONV_TM - 1, 0))
    cw = cw_ref[...]
    conv = z_prev * cw[0:1, :] + z * cw[1:2, :] + z_next * cw[2:3, :]
    o_ref[...] = _bf(cb * conv)


def _conv_branch(h, w_in, conv_w, layer, jobs):
    nj = CONV_DIM // CONV_TN
    ni = ROWS // CONV_TM

    def wspec(group):
        return pl.BlockSpec((None, D, CONV_TN), lambda i, j: (layer, 0, group * nj + j))

    c_in, c_args, c_out, c_shapes = _cast_plumbing(jobs, ni * nj, lambda i, j: i * nj + j)
    out = pl.pallas_call(
        functools.partial(_conv_kernel, _cast_plan(jobs)),
        grid=(ni, nj),
        in_specs=[pl.BlockSpec((CONV_TM, D), lambda i, j: (i, 0)),
                  wspec(0), wspec(1), wspec(2),
                  pl.BlockSpec((None, 3, CONV_TN), lambda i, j: (layer, 0, j))] + c_in,
        out_specs=[pl.BlockSpec((CONV_TM, CONV_TN), lambda i, j: (i, j))] + c_out,
        out_shape=[jax.ShapeDtypeStruct((ROWS, CONV_DIM), jnp.bfloat16)] + c_shapes,
        compiler_params=_params(("arbitrary", "arbitrary")),
        name="conv_branch",
    )(h, w_in, w_in, w_in, conv_w, *c_args)
    return out[0], out[1:]


ATTN_SEQS = 2
ATTN_TM = ATTN_SEQS * CTX_LEN
ATTN_CHUNK = 512
ROPE_WIN, ROPE_DIFF, ROPE_NONE = 0, 1, 2
_SLAB_KIND = ([ROPE_WIN] * (WIN_HEADS + WIN_KV) + [ROPE_NONE] * WIN_KV
              + [ROPE_DIFF] * (2 * DIFF_HEADS) + [ROPE_NONE] * DIFF_HEADS)
_ROPE_SHIFT = {ROPE_WIN: HEAD // 4, ROPE_DIFF: DIFF_QK // 4}
_STATE_COLS = ((A_WK, WIN_KV * HEAD), (A_WV, WIN_KV * HEAD),
               (A_DK, DIFF_HEADS * HEAD), (A_DV, DIFF_HEADS * HEAD))
DIFF_Q_SCALE = DIFF_QK ** -0.5 * math.log2(math.e)


def _rope_tables():
    t = np.arange(LAT_LEN, dtype=np.int32)
    rows = (t // GRID_W).astype(np.float32)[:, None]
    cols = (t % GRID_W).astype(np.float32)[:, None]
    lane = np.arange(HEAD, dtype=np.int32)[None, :]
    out = []
    for kind in (ROPE_WIN, ROPE_DIFF):
        s = _ROPE_SHIFT[kind]
        within = lane % (4 * s)
        freq = (within % s).astype(np.float32)
        inv = np.float32(ROPE_BASE) ** (-freq / np.float32(s))
        ang = (np.where(within < 2 * s, rows, cols) * inv).astype(np.float32)
        first = (within % (2 * s)) < s
        cos, sin = np.cos(ang), np.sin(ang)
        out += [cos, np.where(first, -sin, 0.0), np.where(first, 0.0, sin)]
    return jnp.asarray(np.stack(out).astype(np.float32))


def _attn_proj_kernel(first, h_ref, w_ref, tab_ref, *refs):
    qkv_ref, *state_refs = refs[-(1 + len(_STATE_COLS)):]
    is_ctx = pl.program_id(0) < ROWS_CTX // ATTN_TM

    def body(ctx):
        h = h_ref[...]
        if ctx and first:
            for ref in state_refs:
                ref[:, 1:] = jnp.zeros((ATTN_SEQS, N_LAYERS - 1) + ref.shape[2:], jnp.float32)
        for c0 in range(0, ATTN_COLS, ATTN_CHUNK):
            acc = _dot(h, w_ref[:, c0:c0 + ATTN_CHUNK])
            for s0 in range(0, ATTN_CHUNK, HEAD):
                col = c0 + s0
                x = acc[:, s0:s0 + HEAD]
                kind = _SLAB_KIND[col // HEAD]
                if ctx:
                    for ref, (src, width) in zip(state_refs, _STATE_COLS):
                        if src <= col < src + width:
                            dst = col - src
                            for q in range(ATTN_SEQS):
                                rows = x[q * CTX_LEN:(q + 1) * CTX_LEN]
                                if first:
                                    ref[q, 0, :, dst:dst + HEAD] = rows
                                else:
                                    ref[q, :, dst:dst + HEAD] = rows
                elif kind != ROPE_NONE:
                    s = _ROPE_SHIFT[kind]
                    x = (x * tab_ref[3 * kind] + pltpu.roll(x, HEAD - s, 1) * tab_ref[3 * kind + 1]
                         + pltpu.roll(x, s, 1) * tab_ref[3 * kind + 2])
                if A_DQ <= col < A_DK:
                    x = x * DIFF_Q_SCALE
                qkv_ref[:, col:col + HEAD] = _bf(x)

    pl.when(is_ctx)(lambda: body(True))
    pl.when(jnp.logical_not(is_ctx))(lambda: body(False))


def _attn_proj(h, w_attn, tables, layer, prev_states):
    tm = ATTN_TM
    per_lat = LAT_LEN // tm
    first = prev_states is None

    n_ctx = ROWS_CTX // tm

    def state_spec(width):
        if first:
            return pl.BlockSpec((ATTN_SEQS, N_LAYERS, CTX_LEN, width),
                                lambda i: (jnp.minimum(i, n_ctx - 1), 0, 0, 0))
        return pl.BlockSpec((ATTN_SEQS, None, CTX_LEN, width),
                            lambda i: (jnp.minimum(i, n_ctx - 1), layer, 0, 0))

    in_specs = [pl.BlockSpec((tm, D), lambda i: (i, 0)),
                pl.BlockSpec((D, ATTN_COLS), lambda i: (0, 0), pipeline_mode=pl.Buffered(1)),
                pl.BlockSpec((6, tm, HEAD), lambda i: (0, jnp.maximum(i - n_ctx, 0) % per_lat, 0))]
    args = [h, w_attn, tables]
    aliases = {}
    if not first:
        for k, st in enumerate(prev_states):
            in_specs.append(pl.BlockSpec(memory_space=pl.ANY))
            aliases[len(args)] = 1 + k
            args.append(st)
    out = pl.pallas_call(
        functools.partial(_attn_proj_kernel, first),
        grid=(ROWS // tm,),
        in_specs=in_specs,
        out_specs=[pl.BlockSpec((tm, ATTN_COLS), lambda i: (i, 0))]
                  + [state_spec(width) for _, width in _STATE_COLS],
        out_shape=[jax.ShapeDtypeStruct((ROWS, ATTN_COLS), jnp.bfloat16)]
                  + [jax.ShapeDtypeStruct((N_CTX_SEQ, N_LAYERS, CTX_LEN, width), jnp.float32)
                     for _, width in _STATE_COLS],
        input_output_aliases=aliases,
        compiler_params=_params(("arbitrary",)),
        name="attn_proj",
    )(*args)
    return out[0], out[1:]


def _slab(ref, col0, idx):
    return ref[:, col0 + idx * HEAD:col0 + (idx + 1) * HEAD]


def _with_ones(v):
    return jnp.concatenate([v, jnp.ones_like(v)], axis=1)


def _pipelined(items, scores, finish):
    s = scores(items[0])
    for r, item in enumerate(items):
        s_next = scores(items[r + 1]) if r + 1 < len(items) else None
        finish(item, s)
        s = s_next


def _win_scores(q_ref, col0, g, k, bias):
    q = jnp.concatenate([_slab(q_ref, col0, g * WIN_GROUP + hh) for hh in range(WIN_GROUP)], axis=0)
    s = _dot_nt(q, k) * (HEAD ** -0.5)
    return s if bias is None else s + bias


def _win_finish(layer, g, s, v, sink_ref, o_ref):
    rows = s.shape[0] // WIN_GROUP
    sink = jnp.concatenate(
        [jnp.full((rows, 1), sink_ref[layer * WIN_HEADS + g * WIN_GROUP + hh], jnp.float32)
         for hh in range(WIN_GROUP)], axis=0)
    m = jnp.maximum(jnp.max(s, axis=-1, keepdims=True), sink)
    r = _dot(_bf(jnp.exp(s - m)), _with_ones(v))
    o = r[:, :HEAD] / (r[:, HEAD:HEAD + 1] + jnp.exp(sink - m))
    for hh in range(WIN_GROUP):
        head = g * WIN_GROUP + hh
        o_ref[:, head * HEAD:(head + 1) * HEAD] = _bf(o[hh * rows:(hh + 1) * rows])


def _lat_win_kernel(layer, plan, q_ref, kc_ref, vc_ref, kseq_ref, vseq_ref, sink_ref, *refs):
    n_in = _n_cast_inputs(plan)
    o_ref = refs[n_in]
    _run_casts(plan, refs[:n_in], refs[n_in + 1:])
    n = pl.program_id(1)
    start = pl.multiple_of(jnp.clip((n - 1) * QBLK, 0, LAT_LEN - 3 * QBLK), QBLK)
    kb = kseq_ref[pl.ds(start, 3 * QBLK), :]
    vb = vseq_ref[pl.ds(start, 3 * QBLK), :]
    shape = (QBLK, PAST + 3 * QBLK)
    col = lax.broadcasted_iota(jnp.int32, shape, 1)
    qpos = n * QBLK + lax.broadcasted_iota(jnp.int32, shape, 0)
    kpos = start + col - PAST
    bias = jnp.where((col < PAST) | (jnp.abs(qpos - kpos) <= WINDOW), 0.0, NEG)
    bias = jnp.concatenate([bias] * WIN_GROUP, axis=0)

    def keys(ref, band, g):
        return jnp.concatenate([_bf(_slab(ref, 0, g)), band[:, g * HEAD:(g + 1) * HEAD]], axis=0)

    _pipelined(list(range(WIN_KV)),
               lambda g: _win_scores(q_ref, 0, g, keys(kc_ref, kb, g), bias),
               lambda g, s: _win_finish(layer, g, s, keys(vc_ref, vb, g), sink_ref, o_ref))


def _lat_win_attn(qkv, cache_k, cache_v, sink, layer, jobs):
    nb = LAT_LEN // QBLK
    kvw = WIN_KV * HEAD
    cache_spec = pl.BlockSpec((None, None, PAST, kvw), lambda b, n: (b, layer, 0, 0))

    def seq_spec(col0):
        return pl.BlockSpec((LAT_LEN, kvw), lambda b, n: (ROWS_CTX // LAT_LEN + b, col0 // kvw))

    c_in, c_args, c_out, c_shapes = _cast_plumbing(jobs, N_LAT_SEQ * nb, lambda b, n: b * nb + n)
    out = pl.pallas_call(
        functools.partial(_lat_win_kernel, layer, _cast_plan(jobs)),
        grid=(N_LAT_SEQ, nb),
        in_specs=[pl.BlockSpec((QBLK, WIN_HEADS * HEAD), lambda b, n: (ROWS_CTX // QBLK + b * nb + n, 0)),
                  cache_spec, cache_spec, seq_spec(A_WK), seq_spec(A_WV),
                  pl.BlockSpec(memory_space=pltpu.SMEM)] + c_in,
        out_specs=[pl.BlockSpec((QBLK, WIN_HEADS * HEAD), lambda b, n: (b * nb + n, 0))] + c_out,
        out_shape=[jax.ShapeDtypeStruct((ROWS_LAT, WIN_HEADS * HEAD), jnp.bfloat16)] + c_shapes,
        compiler_params=_params(("arbitrary", "arbitrary")),
        name="lat_win_attn",
    )(qkv, cache_k, cache_v, qkv, qkv, sink, *c_args)
    return out[0], out[1:]


def _lam_init(layer):
    return 0.8 - 0.6 * math.exp(-0.3 * layer)


def _diff_lambda(layer, lam_ref):
    lp = lam_ref[...]
    return (jnp.exp(jnp.sum(lp[0:1] * lp[1:2], axis=-1, keepdims=True))
            - jnp.exp(jnp.sum(lp[2:3] * lp[3:4], axis=-1, keepdims=True)) + _lam_init(layer))


def _diff_scores(q, k):
    first = lax.broadcasted_iota(jnp.int32, q.shape, 1) < DIFF_QK
    zero = jnp.zeros_like(q)
    return _dot_nt(jnp.concatenate([jnp.where(first, q, zero), jnp.where(first, zero, q)], axis=0), k)


def _diff_finish(layer, s, v_ones, lam, g):
    tq = s.shape[0] // 2
    e = _bf(jnp.exp2(s - jnp.max(s, axis=-1, keepdims=True)))
    r = _dot(e, v_ones)
    o = (r[:tq, :HEAD] * (1.0 / r[:tq, HEAD:HEAD + 1])
         - r[tq:, :HEAD] * (lam / r[tq:, HEAD:HEAD + 1]))
    o = o * lax.rsqrt(jnp.mean(o * o, axis=-1, keepdims=True) + EPS) * g
    return _bf(o * (1.0 - _lam_init(layer)))


def _ctx_attn_kernel(layer, plan, qkv_ref, sink_ref, lam_ref, g_ref, *refs):
    n_in = _n_cast_inputs(plan)
    yb_ref, yc_ref = refs[n_in:n_in + 2]
    _run_casts(plan, refs[:n_in], refs[n_in + 2:])
    lam = _diff_lambda(layer, lam_ref)
    stages = [("win", g) for g in range(WIN_KV)] + [("diff", h) for h in range(DIFF_HEADS)]

    def scores(stage):
        kind, idx = stage
        if kind == "win":
            return _win_scores(qkv_ref, A_WQ, idx, _slab(qkv_ref, A_WK, idx), None)
        return _diff_scores(_slab(qkv_ref, A_DQ, idx), _slab(qkv_ref, A_DK, idx))

    def finish(stage, s):
        kind, idx = stage
        if kind == "win":
            _win_finish(layer, idx, s, _slab(qkv_ref, A_WV, idx), sink_ref, yb_ref)
        else:
            yc_ref[:, idx * HEAD:(idx + 1) * HEAD] = _diff_finish(
                layer, s, _with_ones(_slab(qkv_ref, A_DV, idx)), lam, g_ref[...])

    _pipelined(stages, scores, finish)


def _ctx_attn(qkv, sink, lam_params, subln, layer, jobs):
    out = pl.BlockSpec((CTX_LEN, WIN_HEADS * HEAD), lambda b: (b, 0))
    c_in, c_args, c_out, c_shapes = _cast_plumbing(jobs, N_CTX_SEQ, lambda b: b)
    res = pl.pallas_call(
        functools.partial(_ctx_attn_kernel, layer, _cast_plan(jobs)),
        grid=(N_CTX_SEQ,),
        in_specs=[pl.BlockSpec((CTX_LEN, ATTN_COLS), lambda b: (b, 0)),
                  pl.BlockSpec(memory_space=pltpu.SMEM),
                  pl.BlockSpec((None, 4, DIFF_QK), lambda b: (layer, 0, 0)),
                  _vec_spec(layer, HEAD)] + c_in,
        out_specs=[out, out] + c_out,
        out_shape=[jax.ShapeDtypeStruct((ROWS_CTX, WIN_HEADS * HEAD), jnp.bfloat16),
                   jax.ShapeDtypeStruct((ROWS_CTX, DIFF_HEADS * HEAD), jnp.bfloat16)] + c_shapes,
        compiler_params=_params(("arbitrary",)),
        name="ctx_attn",
    )(qkv, sink, lam_params, subln, *c_args)
    return res[0], res[1], res[2:]


LAT_DIFF_TQ = 2048
LAT_DIFF_SUB = 256


def _lat_diff_kernel(layer, plan, q_ref, kc_ref, vc_ref, k_ref, v_ref, lam_ref, g_ref, *refs):
    n_in = _n_cast_inputs(plan)
    o_ref = refs[n_in]
    kall_ref, vall_ref = refs[-2:]

    @pl.when(pl.program_id(2) == 0)
    def _():
        kall_ref[0:PAST] = _bf(kc_ref[...])
        kall_ref[PAST:] = k_ref[...]
        vall_ref[0:PAST] = _with_ones(_bf(vc_ref[...]))
        vall_ref[PAST:] = _with_ones(v_ref[...])

    _run_casts(plan, refs[:n_in], refs[n_in + 1:-2])
    lam = _diff_lambda(layer, lam_ref)

    def finish(r0, s):
        o_ref[r0:r0 + LAT_DIFF_SUB] = _diff_finish(layer, s, vall_ref[...], lam, g_ref[...])

    _pipelined(list(range(0, LAT_DIFF_TQ, LAT_DIFF_SUB)),
               lambda r0: _diff_scores(q_ref[r0:r0 + LAT_DIFF_SUB], kall_ref[...]), finish)


def _lat_diff_attn(qkv, cache_k, cache_v, lam_params, subln, layer, jobs):
    tq = LAT_DIFF_TQ
    nq = LAT_LEN // tq
    cache_spec = pl.BlockSpec((None, None, PAST, HEAD), lambda b, h, t: (b, layer, 0, h))

    def seq_spec(col0):
        return pl.BlockSpec((LAT_LEN, HEAD), lambda b, h, t: (ROWS_CTX // LAT_LEN + b, col0 // HEAD + h))

    c_in, c_args, c_out, c_shapes = _cast_plumbing(
        jobs, N_LAT_SEQ * DIFF_HEADS * nq, lambda b, h, t: (b * DIFF_HEADS + h) * nq + t)
    out = pl.pallas_call(
        functools.partial(_lat_diff_kernel, layer, _cast_plan(jobs)),
        grid=(N_LAT_SEQ, DIFF_HEADS, nq),
        in_specs=[pl.BlockSpec((tq, HEAD), lambda b, h, t: (ROWS_CTX // tq + b * nq + t, A_DQ // HEAD + h)),
                  cache_spec, cache_spec, seq_spec(A_DK), seq_spec(A_DV),
                  pl.BlockSpec((None, 4, DIFF_QK), lambda b, h, t: (layer, 0, 0)),
                  _vec_spec(layer, HEAD)] + c_in,
        out_specs=[pl.BlockSpec((tq, HEAD), lambda b, h, t: (b * nq + t, h))] + c_out,
        out_shape=[jax.ShapeDtypeStruct((ROWS_LAT, DIFF_HEADS * HEAD), jnp.bfloat16)] + c_shapes,
        scratch_shapes=[pltpu.VMEM((PAST + LAT_LEN, HEAD), jnp.bfloat16),
                        pltpu.VMEM((PAST + LAT_LEN, 2 * HEAD), jnp.bfloat16)],
        compiler_params=_params(("arbitrary", "arbitrary", "arbitrary")),
        name="lat_diff_attn",
    )(qkv, cache_k, cache_v, qkv, qkv, lam_params, subln, *c_args)
    return out[0], out[1:]


MERGE_TM = 1024
MERGE_TN = 512


def _merge_kernel(h_ref, ya_ref, ybc_ref, ybl_ref, ycc_ref, ycl_ref,
                  wga_ref, wgb_ref, wgc_ref, wa_ref, wb_ref, wc_ref, o_ref):
    h = h_ref[...]
    yb = _pick_rows(MERGE_TM, ybc_ref, ybl_ref)
    yc = _pick_rows(MERGE_TM, ycc_ref, ycl_ref)
    m = _sigmoid(_dot(h, wga_ref[...])) * _dot(ya_ref[...], wa_ref[...])
    m += _sigmoid(_dot(h, wgb_ref[...])) * _dot(yb, wb_ref[...])
    m += _sigmoid(_dot(h, wgc_ref[...])) * _dot(yc, wc_ref[...])
    o_ref[...] = _bf(m)


def _merge(h, ya, yb_ctx, yb_lat, yc_ctx, yc_lat, w_gate, wa, wb, wc):
    tm, tn = MERGE_TM, MERGE_TN
    nj = D // tn

    def gate_spec(branch):
        return pl.BlockSpec((D, tn), lambda i, j: (0, branch * nj + j))

    w_spec = pl.BlockSpec((CONV_DIM, tn), lambda i, j: (0, j))
    return pl.pallas_call(
        _merge_kernel,
        grid=(ROWS // tm, nj),
        in_specs=[pl.BlockSpec((tm, D), lambda i, j: (i, 0)),
                  pl.BlockSpec((tm, CONV_DIM), lambda i, j: (i, 0)),
                  *_split_specs(tm, WIN_HEADS * HEAD), *_split_specs(tm, DIFF_HEADS * HEAD),
                  gate_spec(0), gate_spec(1), gate_spec(2), w_spec, w_spec, w_spec],
        out_specs=pl.BlockSpec((tm, tn), lambda i, j: (i, j)),
        out_shape=jax.ShapeDtypeStruct((ROWS, D), jnp.bfloat16),
        compiler_params=_params(("parallel", "arbitrary")),
        name="merge",
    )(h, ya, yb_ctx, yb_lat, yc_ctx, yc_lat, w_gate, w_gate, w_gate, wa, wb, wc)


OUT_TM = 512


def _out_proj_kernel(n_x, *refs):
    x_refs = refs[:n_x]
    m_ref, w_ref, gate_ref, g_ref, shift_ref, scale_ref, xo_ref, h_ref = refs[n_x:]
    x = x_refs[0][...] if n_x == 1 else _pick_rows(OUT_TM, *x_refs)
    x = x + gate_ref[...] * _dot(m_ref[...], w_ref[...])
    xo_ref[...] = x
    h_ref[...] = _bf(_rms_mod(x, g_ref[...], scale_ref[...], shift_ref[...]))


def _out_proj(xs, merged, w_out, mods, norm_mlp, layer):
    tm = OUT_TM
    row = pl.BlockSpec((tm, D), lambda i: (i, 0))
    x_specs = [row] if len(xs) == 1 else list(_split_specs(tm, D))
    return pl.pallas_call(
        functools.partial(_out_proj_kernel, len(xs)),
        grid=(ROWS // tm,),
        in_specs=x_specs + [row,
                            pl.BlockSpec((D, D), lambda i: (0, 0)),
                            _mod_spec(layer, 2, tm), _vec_spec(layer, D),
                            _mod_spec(layer, 3, tm), _mod_spec(layer, 4, tm)],
        out_specs=[row, row],
        out_shape=[jax.ShapeDtypeStruct((ROWS, D), jnp.float32),
                   jax.ShapeDtypeStruct((ROWS, D), jnp.bfloat16)],
        compiler_params=_params(("arbitrary",)),
        name="out_proj",
    )(*xs, merged, w_out, mods, norm_mlp.reshape(N_LAYERS, 1, D), mods, mods)


MLP_TM = 512
MLP_TF = 1024


def _mlp_kernel(final, h_ref, w1_ref, w2_ref, x_ref, gate_ref, g_ref, shift_ref, scale_ref,
                *rest):
    if final:
        yc_ref, yl_ref, acc_ref = rest
    else:
        xo_ref, hn_ref, acc_ref = rest
    i = pl.program_id(0)
    j = pl.program_id(1)

    @pl.when(j == 0)
    def _():
        acc_ref[...] = jnp.zeros_like(acc_ref)

    t = jnp.maximum(_dot(h_ref[...], w1_ref[...]), 0.0)
    acc_ref[...] += _dot(_bf(t * t), w2_ref[...])

    @pl.when(j == pl.num_programs(1) - 1)
    def _():
        x = x_ref[...] + gate_ref[...] * acc_ref[...]
        if final:
            y = x * lax.rsqrt(jnp.mean(x * x, axis=-1, keepdims=True) + EPS) * g_ref[...]

            @pl.when(i < ROWS_CTX // MLP_TM)
            def _():
                yc_ref[...] = y

            @pl.when(i >= ROWS_CTX // MLP_TM)
            def _():
                yl_ref[...] = y
        else:
            xo_ref[...] = x
            hn_ref[...] = _bf(_rms_mod(x, g_ref[...], scale_ref[...], shift_ref[...]))


def _mlp(h2, x, w1, w2, mods, norm_next, layer, final):
    tm, tf = MLP_TM, MLP_TF
    row = pl.BlockSpec((tm, D), lambda i, j: (i, 0))
    nxt = layer if final else layer + 1
    g_spec = pl.BlockSpec((None, 1, D), lambda i, j: (0 if final else nxt, 0, 0))
    if final:
        out_specs = list(_split_specs(tm, D))
        out_shape = [jax.ShapeDtypeStruct((ROWS_CTX, D), jnp.float32),
                     jax.ShapeDtypeStruct((ROWS_LAT, D), jnp.float32)]
    else:
        out_specs = [row, row]
        out_shape = [jax.ShapeDtypeStruct((ROWS, D), jnp.float32),
                     jax.ShapeDtypeStruct((ROWS, D), jnp.bfloat16)]
    return pl.pallas_call(
        functools.partial(_mlp_kernel, final),
        grid=(ROWS // tm, D_FF // tf),
        in_specs=[row,
                  pl.BlockSpec((D, tf), lambda i, j: (0, j)),
                  pl.BlockSpec((tf, D), lambda i, j: (j, 0)),
                  row, _mod_spec(layer, 5, tm), g_spec,
                  _mod_spec(nxt, 0, tm), _mod_spec(nxt, 1, tm)],
        out_specs=out_specs,
        out_shape=out_shape,
        scratch_shapes=[pltpu.VMEM((tm, D), jnp.float32)],
        compiler_params=_params(("arbitrary", "arbitrary")),
        name="mlp_final" if final else "mlp",
    )(h2, w1, w2, x, mods, norm_next, mods, mods)


def kernel(x_prompt, x_sample, cache_win_k, cache_win_v, cache_diff_k, cache_diff_v, c, c_ctx, w_ada, b_ada, norm_mix, norm_mlp, w_in, conv_w, win_sink, lambda_q1, lambda_k1, lambda_q2, lambda_k2, diff_subln, w_branch_conv, w_branch_win, w_branch_diff, w_out, w_mlp1, w_mlp2, norm_final):
    xs = (x_prompt.reshape(ROWS_CTX, D), x_sample.reshape(ROWS_LAT, D))
    cvec = jnp.concatenate([c_ctx[None], c, jnp.zeros((N_MOD_ROWS - 1 - N_LAT_SEQ, D), jnp.float32)], axis=0)
    mods = _modulation(cvec, w_ada, b_ada).reshape(N_LAYERS * N_MOD_ROWS * N_MOD, 1, D)

    tables = _rope_tables()
    sink = win_sink.reshape(N_LAYERS * WIN_HEADS)
    lam_params = jnp.stack([lambda_q1, lambda_k1, lambda_q2, lambda_k2], axis=1)
    subln = diff_subln.reshape(N_LAYERS, 1, HEAD)
    ck_win = cache_win_k.reshape(N_LAT_SEQ, N_LAYERS, PAST, WIN_KV * HEAD)
    cv_win = cache_win_v.reshape(N_LAT_SEQ, N_LAYERS, PAST, WIN_KV * HEAD)
    ck_diff = cache_diff_k.reshape(N_LAT_SEQ, N_LAYERS, PAST, DIFF_HEADS * HEAD)
    cv_diff = cache_diff_v.reshape(N_LAT_SEQ, N_LAYERS, PAST, DIFF_HEADS * HEAD)
    norm_final_3d = norm_final.reshape(1, 1, D)
    norm_mix_3d = norm_mix.reshape(N_LAYERS, 1, D)

    h = _prologue(*xs, mods, norm_mix, 0)
    states = None
    for layer in range(N_LAYERS):
        in_job = functools.partial(_CastJob, w_in, layer, cbw=IN_CBW)
        ya, (w_attn,) = _conv_branch(h, w_in, conv_w, layer,
                                     [in_job(k0=OFF_ATTN // IN_CBW, nb=ATTN_COLS // IN_CBW)])
        qkv, states = _attn_proj(h, w_attn, tables, layer, states)
        yb_ctx, yc_ctx, (w_gate,) = _ctx_attn(qkv, sink, lam_params, subln, layer,
                                              [in_job(k0=OFF_GATE // IN_CBW, nb=3 * D // IN_CBW)])
        yb_lat, (wa, wb, wc, w_o) = _lat_win_attn(
            qkv, ck_win, cv_win, sink, layer,
            [_whole(w, layer) for w in (w_branch_conv, w_branch_win, w_branch_diff, w_out)])
        yc_lat, (w1, w2) = _lat_diff_attn(qkv, ck_diff, cv_diff, lam_params, subln, layer,
                                          [_whole(w_mlp1, layer), _whole(w_mlp2, layer)])
        merged = _merge(h, ya, yb_ctx, yb_lat, yc_ctx, yc_lat, w_gate, wa, wb, wc)
        x, h2 = _out_proj(xs, merged, w_o, mods, norm_mlp, layer)
        if layer == N_LAYERS - 1:
            y_ctx, y_lat = _mlp(h2, x, w1, w2, mods, norm_final_3d, layer, True)
        else:
            x, h = _mlp(h2, x, w1, w2, mods, norm_mix_3d, layer, False)
            xs = (x,)

    new_win_k, new_win_v, new_diff_k, new_diff_v = states
    return (y_ctx.reshape(N_CTX_SEQ, CTX_LEN, D), y_lat.reshape(N_LAT_SEQ, LAT_LEN, D),
            new_win_k.reshape(N_CTX_SEQ, N_LAYERS, CTX_LEN, WIN_KV, HEAD),
            new_win_v.reshape(N_CTX_SEQ, N_LAYERS, CTX_LEN, WIN_KV, HEAD),
            new_diff_k.reshape(N_CTX_SEQ, N_LAYERS, CTX_LEN, DIFF_HEADS, HEAD),
            new_diff_v.reshape(N_CTX_SEQ, N_LAYERS, CTX_LEN, DIFF_HEADS, HEAD))
```

```python
import functools
import math
from typing import NamedTuple

import jax
import jax.numpy as jnp
import numpy as np
from jax import lax
from jax.experimental import pallas as pl
from jax.experimental.pallas import tpu as pltpu

D = 2048
N_CTX_SEQ = 16
CTX_LEN = 256
N_LAT_SEQ = 2
LAT_LEN = 2048
N_LAYERS = 2
PAST = 512
GRID_W = 64
HEAD = 128
CONV_DIM = 1024
WIN_HEADS = 8
WIN_KV = 2
WIN_GROUP = WIN_HEADS // WIN_KV
WINDOW = 128
QBLK = 128
DIFF_HEADS = 8
DIFF_QK = 64
D_FF = 4 * D
EPS = 1e-6
ROPE_BASE = 10000.0
NEG = -1e30

ROWS_CTX = N_CTX_SEQ * CTX_LEN
ROWS_LAT = N_LAT_SEQ * LAT_LEN
ROWS = ROWS_CTX + ROWS_LAT
N_MOD_ROWS = 8
N_MOD = 6

OFF_CONV = 0
OFF_ATTN = 3 * CONV_DIM
ATTN_COLS = (WIN_HEADS + 2 * WIN_KV) * HEAD + 3 * DIFF_HEADS * HEAD
OFF_GATE = OFF_ATTN + ATTN_COLS
IN_TOTAL = OFF_GATE + 3 * D
IN_CBW = 1536
A_WQ, A_WK, A_WV = 0, WIN_HEADS * HEAD, (WIN_HEADS + WIN_KV) * HEAD
A_DQ = (WIN_HEADS + 2 * WIN_KV) * HEAD
A_DK = A_DQ + DIFF_HEADS * HEAD
A_DV = A_DK + DIFF_HEADS * HEAD

VMEM_LIMIT_V7X = 60 * 1024 * 1024


def _params(sem, vmem=VMEM_LIMIT_V7X):
    return pltpu.CompilerParams(dimension_semantics=sem, vmem_limit_bytes=vmem)


def _bf(x):
    return x.astype(jnp.bfloat16)


def _dot(a, b):
    return jnp.dot(a, b, preferred_element_type=jnp.float32)


def _dot_nt(a, b):
    return lax.dot_general(a, b, (((1,), (1,)), ((), ())), preferred_element_type=jnp.float32)


def _sigmoid(x):
    return 1.0 / (1.0 + jnp.exp(-x))


def _mod_group(i, tm):
    n_ctx = ROWS_CTX // tm
    return jnp.where(i < n_ctx, 0, 1 + (i - n_ctx) // (LAT_LEN // tm))


def _mod_spec(layer, which, tm):
    def imap(i, *_):
        return (layer * N_MOD_ROWS * N_MOD + _mod_group(i, tm) * N_MOD + which, 0, 0)
    return pl.BlockSpec((None, 1, D), imap)


def _vec_spec(layer, width):
    return pl.BlockSpec((None, 1, width), lambda *_: (layer, 0, 0))


def _rms_mod(x, g, scale, shift):
    y = x * lax.rsqrt(jnp.mean(x * x, axis=-1, keepdims=True) + EPS) * g
    return y * (1.0 + scale) + shift


def _mod_kernel(c_ref, w_ref, b_ref, o_ref):
    c = c_ref[...]
    s = c * _sigmoid(c)
    o_ref[...] = _dot(_bf(s), _bf(w_ref[...])) + b_ref[...]


def _modulation(cvec, w_ada, b_ada):
    tn = 1024
    n = N_MOD * D
    return pl.pallas_call(
        _mod_kernel,
        grid=(N_LAYERS, n // tn),
        in_specs=[pl.BlockSpec((N_MOD_ROWS, D), lambda l, j: (0, 0)),
                  pl.BlockSpec((None, D, tn), lambda l, j: (l, 0, j)),
                  pl.BlockSpec((None, 1, tn), lambda l, j: (l, 0, j))],
        out_specs=pl.BlockSpec((None, N_MOD_ROWS, tn), lambda l, j: (l, 0, j)),
        out_shape=jax.ShapeDtypeStruct((N_LAYERS, N_MOD_ROWS, n), jnp.float32),
        compiler_params=_params(("parallel", "parallel")),
        name="modulation",
    )(cvec, w_ada, b_ada.reshape(N_LAYERS, 1, n))


def _split_specs(tm, width):
    n_ctx = ROWS_CTX // tm
    return (pl.BlockSpec((tm, width), lambda i, *_: (jnp.minimum(i, n_ctx - 1), 0)),
            pl.BlockSpec((tm, width), lambda i, *_: (jnp.maximum(i - n_ctx, 0), 0)))


def _pick_rows(tm, ctx_ref, lat_ref):
    return jnp.where(pl.program_id(0) < ROWS_CTX // tm, ctx_ref[...], lat_ref[...])


PRO_TM = 1024


def _prologue_kernel(xc_ref, xl_ref, g_ref, shift_ref, scale_ref, h_ref):
    x = _pick_rows(PRO_TM, xc_ref, xl_ref)
    h_ref[...] = _bf(_rms_mod(x, g_ref[...], scale_ref[...], shift_ref[...]))


def _prologue(x_ctx, x_lat, mods, norm_mix, layer):
    tm = PRO_TM
    return pl.pallas_call(
        _prologue_kernel,
        grid=(ROWS // tm,),
        in_specs=[*_split_specs(tm, D),
                  _vec_spec(layer, D),
                  _mod_spec(layer, 0, tm),
                  _mod_spec(layer, 1, tm)],
        out_specs=pl.BlockSpec((tm, D), lambda i: (i, 0)),
        out_shape=jax.ShapeDtypeStruct((ROWS, D), jnp.bfloat16),
        compiler_params=_params(("arbitrary",)),
        name="prologue",
    )(x_ctx, x_lat, norm_mix.reshape(N_LAYERS, 1, D), mods, mods)


class _CastJob(NamedTuple):
    src: jax.Array
    layer: int
    k0: int
    nb: int
    cbw: int

    @property
    def rows(self):
        return self.src.shape[1]


def _whole(src, layer):
    return _CastJob(src, layer, 0, 1, src.shape[2])


def _cast_plumbing(jobs, n_steps, step_of):
    in_specs, args, out_specs, out_shapes = [], [], [], []
    for job in jobs:
        rows = job.rows // n_steps
        assert rows * n_steps == job.rows and rows % 16 == 0
        for k in range(job.nb):
            in_specs.append(pl.BlockSpec(
                (None, rows, job.cbw), lambda *g, job=job, k=k: (job.layer, step_of(*g), job.k0 + k)))
            args.append(job.src)
        out_specs.append(pl.BlockSpec((rows, job.nb * job.cbw), lambda *g: (step_of(*g), 0)))
        out_shapes.append(jax.ShapeDtypeStruct((job.rows, job.nb * job.cbw), jnp.bfloat16))
    return in_specs, args, out_specs, out_shapes


def _cast_plan(jobs):
    return tuple((job.nb, job.cbw) for job in jobs)


def _run_casts(plan, in_refs, out_refs):
    in_refs = iter(in_refs)
    for (nb, cbw), out_ref in zip(plan, out_refs):
        for k in range(nb):
            out_ref[:, k * cbw:(k + 1) * cbw] = _bf(next(in_refs)[...])


def _n_cast_inputs(plan):
    return sum(nb for nb, _ in plan)


CONV_TM = LAT_LEN
CONV_TN = 256


def _conv_kernel(plan, h_ref, wb_ref, wc_ref, wx_ref, cw_ref, *refs):
    n_in = _n_cast_inputs(plan)
    o_ref = refs[n_in]
    _run_casts(plan, refs[:n_in], refs[n_in + 1:])
    i = pl.program_id(0)
    h = h_ref[...]
    cb = _dot(h, _bf(wb_ref[...]))
    z = _dot(h, _bf(wc_ref[...])) * _dot(h, _bf(wx_ref[...]))
    seq_mask = jnp.where(i < ROWS_CTX // CONV_TM, CTX_LEN - 1, LAT_LEN - 1)
    pos = lax.broadcasted_iota(jnp.int32, z.shape, 0) & seq_mask
    z_prev = jnp.where(pos == 0, 0.0, pltpu.roll(z, 1, 0))
    z_next = jnp.where(pos == seq_mask, 0.0, pltpu.roll(z, CONV_TM - 1, 0))
    cw = cw_ref[...]
    conv = z_prev * cw[0:1, :] + z * cw[1:2, :] + z_next * cw[2:3, :]
    o_ref[...] = _bf(cb * conv)


def _conv_branch(h, w_in, conv_w, layer, jobs):
    nj = CONV_DIM // CONV_TN
    ni = ROWS // CONV_TM

    def wspec(group):
        return pl.BlockSpec((None, D, CONV_TN), lambda i, j: (layer, 0, group * nj + j))

    c_in, c_args, c_out, c_shapes = _cast_plumbing(jobs, ni * nj, lambda i, j: i * nj + j)
    out = pl.pallas_call(
        functools.partial(_conv_kernel, _cast_plan(jobs)),
        grid=(ni, nj),
        in_specs=[pl.BlockSpec((CONV_TM, D), lambda i, j: (i, 0)),
                  wspec(0), wspec(1), wspec(2),
                  pl.BlockSpec((None, 3, CONV_TN), lambda i, j: (layer, 0, j))] + c_in,
        out_specs=[pl.BlockSpec((CONV_TM, CONV_TN), lambda i, j: (i, j))] + c_out,
        out_shape=[jax.ShapeDtypeStruct((ROWS, CONV_DIM), jnp.bfloat16)] + c_shapes,
        compiler_params=_params(("arbitrary", "arbitrary")),
        name="conv_branch",
    )(h, w_in, w_in, w_in, conv_w, *c_args)
    return out[0], out[1:]


ATTN_TM = CTX_LEN
ATTN_CHUNK = 512
ROPE_WIN, ROPE_DIFF, ROPE_NONE = 0, 1, 2
_SLAB_KIND = ([ROPE_WIN] * (WIN_HEADS + WIN_KV) + [ROPE_NONE] * WIN_KV
              + [ROPE_DIFF] * (2 * DIFF_HEADS) + [ROPE_NONE] * DIFF_HEADS)
_ROPE_SHIFT = {ROPE_WIN: HEAD // 4, ROPE_DIFF: DIFF_QK // 4}
_STATE_COLS = ((A_WK, WIN_KV * HEAD), (A_WV, WIN_KV * HEAD),
               (A_DK, DIFF_HEADS * HEAD), (A_DV, DIFF_HEADS * HEAD))
DIFF_Q_SCALE = DIFF_QK ** -0.5 * math.log2(math.e)


def _rope_tables():
    t = np.arange(LAT_LEN, dtype=np.int32)
    rows = (t // GRID_W).astype(np.float32)[:, None]
    cols = (t % GRID_W).astype(np.float32)[:, None]
    lane = np.arange(HEAD, dtype=np.int32)[None, :]
    out = []
    for kind in (ROPE_WIN, ROPE_DIFF):
        s = _ROPE_SHIFT[kind]
        within = lane % (4 * s)
        freq = (within % s).astype(np.float32)
        inv = np.float32(ROPE_BASE) ** (-freq / np.float32(s))
        ang = (np.where(within < 2 * s, rows, cols) * inv).astype(np.float32)
        first = (within % (2 * s)) < s
        cos, sin = np.cos(ang), np.sin(ang)
        out += [cos, np.where(first, -sin, 0.0), np.where(first, 0.0, sin)]
    return jnp.asarray(np.stack(out).astype(np.float32))


def _attn_proj_kernel(first, layer, plan, h_ref, w_ref, tab_ref, sink_ref, lam_ref, g_ref, *refs):
    n_in = _n_cast_inputs(plan)
    n_st = len(_STATE_COLS)
    pos = (0 if first else n_st) + n_in
    cast_in = refs[pos - n_in:pos]
    qkv_ref = refs[pos]
    state_refs = refs[pos + 1:pos + 1 + n_st]
    yb_ref, yc_ref = refs[pos + 1 + n_st:pos + 3 + n_st]
    cast_out = refs[pos + 3 + n_st:-2]
    tiles = refs[-2:]
    i = pl.program_id(0)

    def project(ctx, dst_ref):
        h = h_ref[...]
        if ctx and first:
            for ref in state_refs:
                ref[1:] = jnp.zeros((N_LAYERS - 1,) + ref.shape[1:], jnp.float32)
        for c0 in range(0, ATTN_COLS, ATTN_CHUNK):
            acc = _dot(h, w_ref[:, c0:c0 + ATTN_CHUNK])
            for s0 in range(0, ATTN_CHUNK, HEAD):
                col = c0 + s0
                x = acc[:, s0:s0 + HEAD]
                kind = _SLAB_KIND[col // HEAD]
                if ctx:
                    for ref, (src, width) in zip(state_refs, _STATE_COLS):
                        if src <= col < src + width:
                            dst = col - src
                            if first:
                                ref[0, :, dst:dst + HEAD] = x
                            else:
                                ref[:, dst:dst + HEAD] = x
                elif kind != ROPE_NONE:
                    s = _ROPE_SHIFT[kind]
                    x = (x * tab_ref[3 * kind] + pltpu.roll(x, HEAD - s, 1) * tab_ref[3 * kind + 1]
                         + pltpu.roll(x, s, 1) * tab_ref[3 * kind + 2])
                if A_DQ <= col < A_DK:
                    x = x * DIFF_Q_SCALE
                dst_ref[:, col:col + HEAD] = _bf(x)

    def step(ctx, parity, attend):
        def run():
            _run_casts(plan, cast_in, cast_out)
            if attend:
                _ctx_attention(layer, tiles[1 - parity], sink_ref, lam_ref, g_ref, yb_ref, yc_ref)
            project(ctx, tiles[parity] if ctx else qkv_ref)
        return run

    @pl.when(i == 0)
    def _():
        tiles[1][...] = jnp.zeros_like(tiles[1])

    is_ctx = i < N_CTX_SEQ
    for parity in (0, 1):
        pl.when(is_ctx & (i % 2 == parity))(step(True, parity, True))
    pl.when(i == N_CTX_SEQ)(step(False, N_CTX_SEQ % 2, True))
    pl.when(i > N_CTX_SEQ)(step(False, 0, False))


def _attn_proj(h, w_attn, tables, sink, lam_params, subln, layer, prev_states, jobs):
    tm = ATTN_TM
    per_lat = LAT_LEN // tm
    first = prev_states is None
    last_ctx = N_CTX_SEQ - 1

    def state_spec(width):
        if first:
            return pl.BlockSpec((None, N_LAYERS, CTX_LEN, width), lambda i: (jnp.minimum(i, last_ctx), 0, 0, 0))
        return pl.BlockSpec((None, None, CTX_LEN, width), lambda i: (jnp.minimum(i, last_ctx), layer, 0, 0))

    in_specs = [pl.BlockSpec((tm, D), lambda i: (i, 0)),
                pl.BlockSpec((D, ATTN_COLS), lambda i: (0, 0), pipeline_mode=pl.Buffered(1)),
                pl.BlockSpec((6, tm, HEAD), lambda i: (0, jnp.maximum(i - N_CTX_SEQ, 0) % per_lat, 0)),
                pl.BlockSpec(memory_space=pltpu.SMEM),
                pl.BlockSpec((None, 4, DIFF_QK), lambda i: (layer, 0, 0)),
                _vec_spec(layer, HEAD)]
    args = [h, w_attn, tables, sink, lam_params, subln]
    aliases = {}
    if not first:
        for k, st in enumerate(prev_states):
            in_specs.append(pl.BlockSpec(memory_space=pl.ANY))
            aliases[len(args)] = 1 + k
            args.append(st)
    c_in, c_args, c_out, c_shapes = _cast_plumbing(jobs, ROWS // tm, lambda i: i)
    y_spec = pl.BlockSpec((CTX_LEN, WIN_HEADS * HEAD), lambda i: (jnp.clip(i - 1, 0, last_ctx), 0))
    y_shape = jax.ShapeDtypeStruct((ROWS_CTX, WIN_HEADS * HEAD), jnp.bfloat16)
    out = pl.pallas_call(
        functools.partial(_attn_proj_kernel, first, layer, _cast_plan(jobs)),
        grid=(ROWS // tm,),
        in_specs=in_specs + c_in,
        out_specs=[pl.BlockSpec((tm, ATTN_COLS), lambda i: (jnp.maximum(i - N_CTX_SEQ, 0), 0))]
                  + [state_spec(width) for _, width in _STATE_COLS] + [y_spec, y_spec] + c_out,
        out_shape=[jax.ShapeDtypeStruct((ROWS_LAT, ATTN_COLS), jnp.bfloat16)]
                  + [jax.ShapeDtypeStruct((N_CTX_SEQ, N_LAYERS, CTX_LEN, width), jnp.float32)
                     for _, width in _STATE_COLS] + [y_shape, y_shape] + c_shapes,
        scratch_shapes=[pltpu.VMEM((CTX_LEN, ATTN_COLS), jnp.bfloat16)] * 2,
        input_output_aliases=aliases,
        compiler_params=_params(("arbitrary",)),
        name="attn_proj",
    )(*args, *c_args)
    return out[0], out[1:5], out[5], out[6], out[7:]


def _slab(ref, col0, idx):
    return ref[:, col0 + idx * HEAD:col0 + (idx + 1) * HEAD]


def _with_ones(v):
    return jnp.concatenate([v, jnp.ones_like(v)], axis=1)


def _pipelined(items, scores, finish):
    s = scores(items[0])
    for r, item in enumerate(items):
        s_next = scores(items[r + 1]) if r + 1 < len(items) else None
        finish(item, s)
        s = s_next


WIN_STAGE = WIN_GROUP
WIN_STAGES = tuple(range(0, WIN_HEADS, WIN_STAGE))


def _win_scores(q_ref, col0, h0, k, bias):
    q = jnp.concatenate([_slab(q_ref, col0, h0 + hh) for hh in range(WIN_STAGE)], axis=0)
    s = _dot_nt(q, k) * (HEAD ** -0.5)
    return s if bias is None else s + bias


def _win_finish(layer, h0, s, v, sink_ref, o_ref):
    rows = s.shape[0] // WIN_STAGE
    sink = jnp.concatenate(
        [jnp.full((rows, 1), sink_ref[layer * WIN_HEADS + h0 + hh], jnp.float32)
         for hh in range(WIN_STAGE)], axis=0)
    m = jnp.maximum(jnp.max(s, axis=-1, keepdims=True), sink)
    r = _dot(_bf(jnp.exp(s - m)), _with_ones(v))
    o = r[:, :HEAD] / (r[:, HEAD:HEAD + 1] + jnp.exp(sink - m))
    for hh in range(WIN_STAGE):
        head = h0 + hh
        o_ref[:, head * HEAD:(head + 1) * HEAD] = _bf(o[hh * rows:(hh + 1) * rows])


def _lat_win_kernel(layer, plan, q_ref, kc_ref, vc_ref, kseq_ref, vseq_ref, sink_ref, *refs):
    n_in = _n_cast_inputs(plan)
    o_ref = refs[n_in]
    _run_casts(plan, refs[:n_in], refs[n_in + 1:])
    n = pl.program_id(1)
    start = pl.multiple_of(jnp.clip((n - 1) * QBLK, 0, LAT_LEN - 3 * QBLK), QBLK)
    kb = kseq_ref[pl.ds(start, 3 * QBLK), :]
    vb = vseq_ref[pl.ds(start, 3 * QBLK), :]
    shape = (QBLK, PAST + 3 * QBLK)
    col = lax.broadcasted_iota(jnp.int32, shape, 1)
    qpos = n * QBLK + lax.broadcasted_iota(jnp.int32, shape, 0)
    kpos = start + col - PAST
    bias = jnp.where((col < PAST) | (jnp.abs(qpos - kpos) <= WINDOW), 0.0, NEG)
    bias = jnp.concatenate([bias] * WIN_STAGE, axis=0)

    def keys(ref, band, h0):
        g = h0 // WIN_GROUP
        return jnp.concatenate([_bf(_slab(ref, 0, g)), band[:, g * HEAD:(g + 1) * HEAD]], axis=0)

    _pipelined(WIN_STAGES,
               lambda h0: _win_scores(q_ref, 0, h0, keys(kc_ref, kb, h0), bias),
               lambda h0, s: _win_finish(layer, h0, s, keys(vc_ref, vb, h0), sink_ref, o_ref))


def _lat_win_attn(qkv, cache_k, cache_v, sink, layer, jobs):
    nb = LAT_LEN // QBLK
    kvw = WIN_KV * HEAD
    cache_spec = pl.BlockSpec((None, None, PAST, kvw), lambda b, n: (b, layer, 0, 0))

    def seq_spec(col0):
        return pl.BlockSpec((LAT_LEN, kvw), lambda b, n: (b, col0 // kvw))

    c_in, c_args, c_out, c_shapes = _cast_plumbing(jobs, N_LAT_SEQ * nb, lambda b, n: b * nb + n)
    out = pl.pallas_call(
        functools.partial(_lat_win_kernel, layer, _cast_plan(jobs)),
        grid=(N_LAT_SEQ, nb),
        in_specs=[pl.BlockSpec((QBLK, WIN_HEADS * HEAD), lambda b, n: (b * nb + n, 0)),
                  cache_spec, cache_spec, seq_spec(A_WK), seq_spec(A_WV),
                  pl.BlockSpec(memory_space=pltpu.SMEM)] + c_in,
        out_specs=[pl.BlockSpec((QBLK, WIN_HEADS * HEAD), lambda b, n: (b * nb + n, 0))] + c_out,
        out_shape=[jax.ShapeDtypeStruct((ROWS_LAT, WIN_HEADS * HEAD), jnp.bfloat16)] + c_shapes,
        compiler_params=_params(("arbitrary", "arbitrary")),
        name="lat_win_attn",
    )(qkv, cache_k, cache_v, qkv, qkv, sink, *c_args)
    return out[0], out[1:]


def _lam_init(layer):
    return 0.8 - 0.6 * math.exp(-0.3 * layer)


def _diff_lambda(layer, lam_ref):
    lp = lam_ref[...]
    return (jnp.exp(jnp.sum(lp[0:1] * lp[1:2], axis=-1, keepdims=True))
            - jnp.exp(jnp.sum(lp[2:3] * lp[3:4], axis=-1, keepdims=True)) + _lam_init(layer))


def _diff_scores(q, k):
    first = lax.broadcasted_iota(jnp.int32, q.shape, 1) < DIFF_QK
    zero = jnp.zeros_like(q)
    return _dot_nt(jnp.concatenate([jnp.where(first, q, zero), jnp.where(first, zero, q)], axis=0), k)


def _diff_finish(layer, s, v_ones, lam, g):
    tq = s.shape[0] // 2
    e = _bf(jnp.exp2(s - jnp.max(s, axis=-1, keepdims=True)))
    r = _dot(e, v_ones)
    o = (r[:tq, :HEAD] * (1.0 / r[:tq, HEAD:HEAD + 1])
         - r[tq:, :HEAD] * (lam / r[tq:, HEAD:HEAD + 1]))
    o = o * lax.rsqrt(jnp.mean(o * o, axis=-1, keepdims=True) + EPS) * g
    return _bf(o * (1.0 - _lam_init(layer)))


def _ctx_attention(layer, qkv_ref, sink_ref, lam_ref, g_ref, yb_ref, yc_ref):
    lam = _diff_lambda(layer, lam_ref)
    stages = [("win", h0) for h0 in WIN_STAGES] + [("diff", h) for h in range(DIFF_HEADS)]

    def scores(stage):
        kind, idx = stage
        if kind == "win":
            return _win_scores(qkv_ref, A_WQ, idx, _slab(qkv_ref, A_WK, idx // WIN_GROUP), None)
        return _diff_scores(_slab(qkv_ref, A_DQ, idx), _slab(qkv_ref, A_DK, idx))

    def finish(stage, s):
        kind, idx = stage
        if kind == "win":
            _win_finish(layer, idx, s, _slab(qkv_ref, A_WV, idx // WIN_GROUP), sink_ref, yb_ref)
        else:
            yc_ref[:, idx * HEAD:(idx + 1) * HEAD] = _diff_finish(
                layer, s, _with_ones(_slab(qkv_ref, A_DV, idx)), lam, g_ref[...])

    _pipelined(stages, scores, finish)


LAT_DIFF_TQ = 2048
LAT_DIFF_SUB = 256


def _lat_diff_kernel(layer, plan, q_ref, kc_ref, vc_ref, k_ref, v_ref, lam_ref, g_ref, *refs):
    n_in = _n_cast_inputs(plan)
    o_ref = refs[n_in]
    kall_ref, vall_ref = refs[-2:]

    @pl.when(pl.program_id(2) == 0)
    def _():
        kall_ref[0:PAST] = _bf(kc_ref[...])
        kall_ref[PAST:] = k_ref[...]
        vall_ref[0:PAST] = _with_ones(_bf(vc_ref[...]))
        vall_ref[PAST:] = _with_ones(v_ref[...])

    _run_casts(plan, refs[:n_in], refs[n_in + 1:-2])
    lam = _diff_lambda(layer, lam_ref)

    def finish(r0, s):
        o_ref[r0:r0 + LAT_DIFF_SUB] = _diff_finish(layer, s, vall_ref[...], lam, g_ref[...])

    _pipelined(list(range(0, LAT_DIFF_TQ, LAT_DIFF_SUB)),
               lambda r0: _diff_scores(q_ref[r0:r0 + LAT_DIFF_SUB], kall_ref[...]), finish)


def _lat_diff_attn(qkv, cache_k, cache_v, lam_params, subln, layer, jobs):
    tq = LAT_DIFF_TQ
    nq = LAT_LEN // tq
    cache_spec = pl.BlockSpec((None, None, PAST, HEAD), lambda b, h, t: (b, layer, 0, h))

    def seq_spec(col0):
        return pl.BlockSpec((LAT_LEN, HEAD), lambda b, h, t: (b, col0 // HEAD + h))

    c_in, c_args, c_out, c_shapes = _cast_plumbing(
        jobs, N_LAT_SEQ * DIFF_HEADS * nq, lambda b, h, t: (b * DIFF_HEADS + h) * nq + t)
    out = pl.pallas_call(
        functools.partial(_lat_diff_kernel, layer, _cast_plan(jobs)),
        grid=(N_LAT_SEQ, DIFF_HEADS, nq),
        in_specs=[pl.BlockSpec((tq, HEAD), lambda b, h, t: (b * nq + t, A_DQ // HEAD + h)),
                  cache_spec, cache_spec, seq_spec(A_DK), seq_spec(A_DV),
                  pl.BlockSpec((None, 4, DIFF_QK), lambda b, h, t: (layer, 0, 0)),
                  _vec_spec(layer, HEAD)] + c_in,
        out_specs=[pl.BlockSpec((tq, HEAD), lambda b, h, t: (b * nq + t, h))] + c_out,
        out_shape=[jax.ShapeDtypeStruct((ROWS_LAT, DIFF_HEADS * HEAD), jnp.bfloat16)] + c_shapes,
        scratch_shapes=[pltpu.VMEM((PAST + LAT_LEN, HEAD), jnp.bfloat16),
                        pltpu.VMEM((PAST + LAT_LEN, 2 * HEAD), jnp.bfloat16)],
        compiler_params=_params(("arbitrary", "arbitrary", "arbitrary")),
        name="lat_diff_attn",
    )(qkv, cache_k, cache_v, qkv, qkv, lam_params, subln, *c_args)
    return out[0], out[1:]


MERGE_TM = 1024
MERGE_TN = 512


def _merge_kernel(h_ref, ya_ref, ybc_ref, ybl_ref, ycc_ref, ycl_ref,
                  wga_ref, wgb_ref, wgc_ref, wa_ref, wb_ref, wc_ref, o_ref):
    h = h_ref[...]
    yb = _pick_rows(MERGE_TM, ybc_ref, ybl_ref)
    yc = _pick_rows(MERGE_TM, ycc_ref, ycl_ref)
    m = _sigmoid(_dot(h, wga_ref[...])) * _dot(ya_ref[...], wa_ref[...])
    m += _sigmoid(_dot(h, wgb_ref[...])) * _dot(yb, wb_ref[...])
    m += _sigmoid(_dot(h, wgc_ref[...])) * _dot(yc, wc_ref[...])
    o_ref[...] = _bf(m)


def _merge(h, ya, yb_ctx, yb_lat, yc_ctx, yc_lat, w_gate, wa, wb, wc):
    tm, tn = MERGE_TM, MERGE_TN
    nj = D // tn

    def gate_spec(branch):
        return pl.BlockSpec((D, tn), lambda i, j: (0, branch * nj + j))

    w_spec = pl.BlockSpec((CONV_DIM, tn), lambda i, j: (0, j))
    return pl.pallas_call(
        _merge_kernel,
        grid=(ROWS // tm, nj),
        in_specs=[pl.BlockSpec((tm, D), lambda i, j: (i, 0)),
                  pl.BlockSpec((tm, CONV_DIM), lambda i, j: (i, 0)),
                  *_split_specs(tm, WIN_HEADS * HEAD), *_split_specs(tm, DIFF_HEADS * HEAD),
                  gate_spec(0), gate_spec(1), gate_spec(2), w_spec, w_spec, w_spec],
        out_specs=pl.BlockSpec((tm, tn), lambda i, j: (i, j)),
        out_shape=jax.ShapeDtypeStruct((ROWS, D), jnp.bfloat16),
        compiler_params=_params(("parallel", "arbitrary")),
        name="merge",
    )(h, ya, yb_ctx, yb_lat, yc_ctx, yc_lat, w_gate, w_gate, w_gate, wa, wb, wc)


OUT_TM = 512


def _out_proj_kernel(n_x, *refs):
    x_refs = refs[:n_x]
    m_ref, w_ref, gate_ref, g_ref, shift_ref, scale_ref, xo_ref, h_ref = refs[n_x:]
    x = x_refs[0][...] if n_x == 1 else _pick_rows(OUT_TM, *x_refs)
    x = x + gate_ref[...] * _dot(m_ref[...], w_ref[...])
    xo_ref[...] = x
    h_ref[...] = _bf(_rms_mod(x, g_ref[...], scale_ref[...], shift_ref[...]))


def _out_proj(xs, merged, w_out, mods, norm_mlp, layer):
    tm = OUT_TM
    row = pl.BlockSpec((tm, D), lambda i: (i, 0))
    x_specs = [row] if len(xs) == 1 else list(_split_specs(tm, D))
    return pl.pallas_call(
        functools.partial(_out_proj_kernel, len(xs)),
        grid=(ROWS // tm,),
        in_specs=x_specs + [row,
                            pl.BlockSpec((D, D), lambda i: (0, 0)),
                            _mod_spec(layer, 2, tm), _vec_spec(layer, D),
                            _mod_spec(layer, 3, tm), _mod_spec(layer, 4, tm)],
        out_specs=[row, row],
        out_shape=[jax.ShapeDtypeStruct((ROWS, D), jnp.float32),
                   jax.ShapeDtypeStruct((ROWS, D), jnp.bfloat16)],
        compiler_params=_params(("arbitrary",)),
        name="out_proj",
    )(*xs, merged, w_out, mods, norm_mlp.reshape(N_LAYERS, 1, D), mods, mods)


MLP_TM = 512
MLP_TF = 1024


def _mlp_kernel(final, h_ref, w1_ref, w2_ref, x_ref, gate_ref, g_ref, shift_ref, scale_ref,
                *rest):
    if final:
        yc_ref, yl_ref, acc_ref = rest
    else:
        xo_ref, hn_ref, acc_ref = rest
    i = pl.program_id(0)
    j = pl.program_id(1)

    @pl.when(j == 0)
    def _():
        acc_ref[...] = jnp.zeros_like(acc_ref)

    t = jnp.maximum(_dot(h_ref[...], w1_ref[...]), 0.0)
    acc_ref[...] += _dot(_bf(t * t), w2_ref[...])

    @pl.when(j == pl.num_programs(1) - 1)
    def _():
        x = x_ref[...] + gate_ref[...] * acc_ref[...]
        if final:
            y = x * lax.rsqrt(jnp.mean(x * x, axis=-1, keepdims=True) + EPS) * g_ref[...]

            @pl.when(i < ROWS_CTX // MLP_TM)
            def _():
                yc_ref[...] = y

            @pl.when(i >= ROWS_CTX // MLP_TM)
            def _():
                yl_ref[...] = y
        else:
            xo_ref[...] = x
            hn_ref[...] = _bf(_rms_mod(x, g_ref[...], scale_ref[...], shift_ref[...]))


def _mlp(h2, x, w1, w2, mods, norm_next, layer, final):
    tm, tf = MLP_TM, MLP_TF
    row = pl.BlockSpec((tm, D), lambda i, j: (i, 0))
    nxt = layer if final else layer + 1
    g_spec = pl.BlockSpec((None, 1, D), lambda i, j: (0 if final else nxt, 0, 0))
    if final:
        out_specs = list(_split_specs(tm, D))
        out_shape = [jax.ShapeDtypeStruct((ROWS_CTX, D), jnp.float32),
                     jax.ShapeDtypeStruct((ROWS_LAT, D), jnp.float32)]
    else:
        out_specs = [row, row]
        out_shape = [jax.ShapeDtypeStruct((ROWS, D), jnp.float32),
                     jax.ShapeDtypeStruct((ROWS, D), jnp.bfloat16)]
    return pl.pallas_call(
        functools.partial(_mlp_kernel, final),
        grid=(ROWS // tm, D_FF // tf),
        in_specs=[row,
                  pl.BlockSpec((D, tf), lambda i, j: (0, j)),
                  pl.BlockSpec((tf, D), lambda i, j: (j, 0)),
                  row, _mod_spec(layer, 5, tm), g_spec,
                  _mod_spec(nxt, 0, tm), _mod_spec(nxt, 1, tm)],
        out_specs=out_specs,
        out_shape=out_shape,
        scratch_shapes=[pltpu.VMEM((tm, D), jnp.float32)],
        compiler_params=_params(("arbitrary", "arbitrary")),
        name="mlp_final" if final else "mlp",
    )(h2, w1, w2, x, mods, norm_next, mods, mods)


def kernel(x_prompt, x_sample, cache_win_k, cache_win_v, cache_diff_k, cache_diff_v, c, c_ctx, w_ada, b_ada, norm_mix, norm_mlp, w_in, conv_w, win_sink, lambda_q1, lambda_k1, lambda_q2, lambda_k2, diff_subln, w_branch_conv, w_branch_win, w_branch_diff, w_out, w_mlp1, w_mlp2, norm_final):
    xs = (x_prompt.reshape(ROWS_CTX, D), x_sample.reshape(ROWS_LAT, D))
    cvec = jnp.concatenate([c_ctx[None], c, jnp.zeros((N_MOD_ROWS - 1 - N_LAT_SEQ, D), jnp.float32)], axis=0)
    mods = _modulation(cvec, w_ada, b_ada).reshape(N_LAYERS * N_MOD_ROWS * N_MOD, 1, D)

    tables = _rope_tables()
    sink = win_sink.reshape(N_LAYERS * WIN_HEADS)
    lam_params = jnp.stack([lambda_q1, lambda_k1, lambda_q2, lambda_k2], axis=1)
    subln = diff_subln.reshape(N_LAYERS, 1, HEAD)
    ck_win = cache_win_k.reshape(N_LAT_SEQ, N_LAYERS, PAST, WIN_KV * HEAD)
    cv_win = cache_win_v.reshape(N_LAT_SEQ, N_LAYERS, PAST, WIN_KV * HEAD)
    ck_diff = cache_diff_k.reshape(N_LAT_SEQ, N_LAYERS, PAST, DIFF_HEADS * HEAD)
    cv_diff = cache_diff_v.reshape(N_LAT_SEQ, N_LAYERS, PAST, DIFF_HEADS * HEAD)
    norm_final_3d = norm_final.reshape(1, 1, D)
    norm_mix_3d = norm_mix.reshape(N_LAYERS, 1, D)

    h = _prologue(*xs, mods, norm_mix, 0)
    states = None
    for layer in range(N_LAYERS):
        in_job = functools.partial(_CastJob, w_in, layer, cbw=IN_CBW)
        ya, (w_attn,) = _conv_branch(h, w_in, conv_w, layer,
                                     [in_job(k0=OFF_ATTN // IN_CBW, nb=ATTN_COLS // IN_CBW)])
        qkv, states, yb_ctx, yc_ctx, (w_gate,) = _attn_proj(
            h, w_attn, tables, sink, lam_params, subln, layer, states,
            [in_job(k0=OFF_GATE // IN_CBW, nb=3 * D // IN_CBW)])
        yb_lat, (wa, wb, wc, w_o) = _lat_win_attn(
            qkv, ck_win, cv_win, sink, layer,
            [_whole(w, layer) for w in (w_branch_conv, w_branch_win, w_branch_diff, w_out)])
        yc_lat, (w1, w2) = _lat_diff_attn(qkv, ck_diff, cv_diff, lam_params, subln, layer,
                                          [_whole(w_mlp1, layer), _whole(w_mlp2, layer)])
        merged = _merge(h, ya, yb_ctx, yb_lat, yc_ctx, yc_lat, w_gate, wa, wb, wc)
        x, h2 = _out_proj(xs, merged, w_o, mods, norm_mlp, layer)
        if layer == N_LAYERS - 1:
            y_ctx, y_lat = _mlp(h2, x, w1, w2, mods, norm_final_3d, layer, True)
        else:
            x, h = _mlp(h2, x, w1, w2, mods, norm_mix_3d, layer, False)
            xs = (x,)

    new_win_k, new_win_v, new_diff_k, new_diff_v = states
    return (y_ctx.reshape(N_CTX_SEQ, CTX_LEN, D), y_lat.reshape(N_LAT_SEQ, LAT_LEN, D),
            new_win_k.reshape(N_CTX_SEQ, N_LAYERS, CTX_LEN, WIN_KV, HEAD),
            new_win_v.reshape(N_CTX_SEQ, N_LAYERS, CTX_LEN, WIN_KV, HEAD),
            new_diff_k.reshape(N_CTX_SEQ, N_LAYERS, CTX_LEN, DIFF_HEADS, HEAD),
            new_diff_v.reshape(N_CTX_SEQ, N_LAYERS, CTX_LEN, DIFF_HEADS, HEAD))
```

```python
import functools
import math
from typing import NamedTuple

import jax
import jax.numpy as jnp
import numpy as np
from jax import lax
from jax.experimental import pallas as pl
from jax.experimental.pallas import tpu as pltpu

D = 2048
N_CTX_SEQ = 16
CTX_LEN = 256
N_LAT_SEQ = 2
LAT_LEN = 2048
N_LAYERS = 2
PAST = 512
GRID_W = 64
HEAD = 128
CONV_DIM = 1024
WIN_HEADS = 8
WIN_KV = 2
WIN_GROUP = WIN_HEADS // WIN_KV
WINDOW = 128
QBLK = 128
DIFF_HEADS = 8
DIFF_QK = 64
D_FF = 4 * D
EPS = 1e-6
ROPE_BASE = 10000.0
NEG = -1e30

ROWS_CTX = N_CTX_SEQ * CTX_LEN
ROWS_LAT = N_LAT_SEQ * LAT_LEN
ROWS = ROWS_CTX + ROWS_LAT
N_MOD_ROWS = 8
N_MOD = 6

OFF_CONV = 0
OFF_ATTN = 3 * CONV_DIM
ATTN_COLS = (WIN_HEADS + 2 * WIN_KV) * HEAD + 3 * DIFF_HEADS * HEAD
OFF_GATE = OFF_ATTN + ATTN_COLS
IN_TOTAL = OFF_GATE + 3 * D
IN_CBW = 1536
A_WQ, A_WK, A_WV = 0, WIN_HEADS * HEAD, (WIN_HEADS + WIN_KV) * HEAD
A_DQ = (WIN_HEADS + 2 * WIN_KV) * HEAD
A_DK = A_DQ + DIFF_HEADS * HEAD
A_DV = A_DK + DIFF_HEADS * HEAD

VMEM_LIMIT_V7X = 60 * 1024 * 1024


def _params(sem, vmem=VMEM_LIMIT_V7X):
    return pltpu.CompilerParams(dimension_semantics=sem, vmem_limit_bytes=vmem)


def _bf(x):
    return x.astype(jnp.bfloat16)


def _dot(a, b):
    return jnp.dot(a, b, preferred_element_type=jnp.float32)


def _dot_nt(a, b):
    return lax.dot_general(a, b, (((1,), (1,)), ((), ())), preferred_element_type=jnp.float32)


def _sigmoid(x):
    return 1.0 / (1.0 + jnp.exp(-x))


def _mod_group(i, tm):
    n_ctx = ROWS_CTX // tm
    return jnp.where(i < n_ctx, 0, 1 + (i - n_ctx) // (LAT_LEN // tm))


def _mod_spec(layer, which, tm):
    def imap(i, *_):
        return (layer * N_MOD_ROWS * N_MOD + _mod_group(i, tm) * N_MOD + which, 0, 0)
    return pl.BlockSpec((None, 1, D), imap)


def _vec_spec(layer, width):
    return pl.BlockSpec((None, 1, width), lambda *_: (layer, 0, 0))


def _rms_mod(x, g, scale, shift):
    y = x * lax.rsqrt(jnp.mean(x * x, axis=-1, keepdims=True) + EPS) * g
    return y * (1.0 + scale) + shift


def _mod_kernel(c_ref, w_ref, b_ref, o_ref):
    c = c_ref[...]
    s = c * _sigmoid(c)
    o_ref[...] = _dot(_bf(s), _bf(w_ref[...])) + b_ref[...]


def _modulation(cvec, w_ada, b_ada):
    tn = 1024
    n = N_MOD * D
    return pl.pallas_call(
        _mod_kernel,
        grid=(N_LAYERS, n // tn),
        in_specs=[pl.BlockSpec((N_MOD_ROWS, D), lambda l, j: (0, 0)),
                  pl.BlockSpec((None, D, tn), lambda l, j: (l, 0, j)),
                  pl.BlockSpec((None, 1, tn), lambda l, j: (l, 0, j))],
        out_specs=pl.BlockSpec((None, N_MOD_ROWS, tn), lambda l, j: (l, 0, j)),
        out_shape=jax.ShapeDtypeStruct((N_LAYERS, N_MOD_ROWS, n), jnp.float32),
        compiler_params=_params(("parallel", "parallel")),
        name="modulation",
    )(cvec, w_ada, b_ada.reshape(N_LAYERS, 1, n))


def _split_specs(tm, width):
    n_ctx = ROWS_CTX // tm
    return (pl.BlockSpec((tm, width), lambda i, *_: (jnp.minimum(i, n_ctx - 1), 0)),
            pl.BlockSpec((tm, width), lambda i, *_: (jnp.maximum(i - n_ctx, 0), 0)))


def _pick_rows(tm, ctx_ref, lat_ref):
    return jnp.where(pl.program_id(0) < ROWS_CTX // tm, ctx_ref[...], lat_ref[...])


PRO_TM = 1024


def _prologue_kernel(xc_ref, xl_ref, g_ref, shift_ref, scale_ref, h_ref):
    x = _pick_rows(PRO_TM, xc_ref, xl_ref)
    h_ref[...] = _bf(_rms_mod(x, g_ref[...], scale_ref[...], shift_ref[...]))


def _prologue(x_ctx, x_lat, mods, norm_mix, layer):
    tm = PRO_TM
    return pl.pallas_call(
        _prologue_kernel,
        grid=(ROWS // tm,),
        in_specs=[*_split_specs(tm, D),
                  _vec_spec(layer, D),
                  _mod_spec(layer, 0, tm),
                  _mod_spec(layer, 1, tm)],
        out_specs=pl.BlockSpec((tm, D), lambda i: (i, 0)),
        out_shape=jax.ShapeDtypeStruct((ROWS, D), jnp.bfloat16),
        compiler_params=_params(("arbitrary",)),
        name="prologue",
    )(x_ctx, x_lat, norm_mix.reshape(N_LAYERS, 1, D), mods, mods)


class _CastJob(NamedTuple):
    src: jax.Array
    layer: int
    k0: int
    nb: int
    cbw: int

    @property
    def rows(self):
        return self.src.shape[1]


def _whole(src, layer):
    return _CastJob(src, layer, 0, 1, src.shape[2])


def _cast_plumbing(jobs, n_steps, step_of):
    in_specs, args, out_specs, out_shapes = [], [], [], []
    for job in jobs:
        rows = job.rows // n_steps
        assert rows * n_steps == job.rows and rows % 16 == 0
        for k in range(job.nb):
            in_specs.append(pl.BlockSpec(
                (None, rows, job.cbw), lambda *g, job=job, k=k: (job.layer, step_of(*g), job.k0 + k)))
            args.append(job.src)
        out_specs.append(pl.BlockSpec((rows, job.nb * job.cbw), lambda *g: (step_of(*g), 0)))
        out_shapes.append(jax.ShapeDtypeStruct((job.rows, job.nb * job.cbw), jnp.bfloat16))
    return in_specs, args, out_specs, out_shapes


def _cast_plan(jobs):
    return tuple((job.nb, job.cbw) for job in jobs)


def _run_casts(plan, in_refs, out_refs):
    in_refs = iter(in_refs)
    for (nb, cbw), out_ref in zip(plan, out_refs):
        for k in range(nb):
            out_ref[:, k * cbw:(k + 1) * cbw] = _bf(next(in_refs)[...])


def _n_cast_inputs(plan):
    return sum(nb for nb, _ in plan)


CONV_TM = LAT_LEN
CONV_TN = 256


def _conv_kernel(plan, h_ref, wb_ref, wc_ref, wx_ref, cw_ref, *refs):
    n_in = _n_cast_inputs(plan)
    o_ref = refs[n_in]
    _run_casts(plan, refs[:n_in], refs[n_in + 1:])
    i = pl.program_id(0)
    h = h_ref[...]
    cb = _dot(h, _bf(wb_ref[...]))
    z = _dot(h, _bf(wc_ref[...])) * _dot(h, _bf(wx_ref[...]))
    seq_mask = jnp.where(i < ROWS_CTX // CONV_TM, CTX_LEN - 1, LAT_LEN - 1)
    pos = lax.broadcasted_iota(jnp.int32, z.shape, 0) & seq_mask
    z_prev = jnp.where(pos == 0, 0.0, pltpu.roll(z, 1, 0))
    z_next = jnp.where(pos == seq_mask, 0.0, pltpu.roll(z, CONV_TM - 1, 0))
    cw = cw_ref[...]
    conv = z_prev * cw[0:1, :] + z * cw[1:2, :] + z_next * cw[2:3, :]
    o_ref[...] = _bf(cb * conv)


def _conv_branch(h, w_in, conv_w, layer, jobs):
    nj = CONV_DIM // CONV_TN
    ni = ROWS // CONV_TM

    def wspec(group):
        return pl.BlockSpec((None, D, CONV_TN), lambda i, j: (layer, 0, group * nj + j))

    c_in, c_args, c_out, c_shapes = _cast_plumbing(jobs, ni * nj, lambda i, j: i * nj + j)
    out = pl.pallas_call(
        functools.partial(_conv_kernel, _cast_plan(jobs)),
        grid=(ni, nj),
        in_specs=[pl.BlockSpec((CONV_TM, D), lambda i, j: (i, 0)),
                  wspec(0), wspec(1), wspec(2),
                  pl.BlockSpec((None, 3, CONV_TN), lambda i, j: (layer, 0, j))] + c_in,
        out_specs=[pl.BlockSpec((CONV_TM, CONV_TN), lambda i, j: (i, j))] + c_out,
        out_shape=[jax.ShapeDtypeStruct((ROWS, CONV_DIM), jnp.bfloat16)] + c_shapes,
        compiler_params=_params(("arbitrary", "arbitrary")),
        name="conv_branch",
    )(h, w_in, w_in, w_in, conv_w, *c_args)
    return out[0], out[1:]


ATTN_TM = CTX_LEN
ATTN_CHUNK = 512
ROPE_WIN, ROPE_DIFF, ROPE_NONE = 0, 1, 2
_SLAB_KIND = ([ROPE_WIN] * (WIN_HEADS + WIN_KV) + [ROPE_NONE] * WIN_KV
              + [ROPE_DIFF] * (2 * DIFF_HEADS) + [ROPE_NONE] * DIFF_HEADS)
_ROPE_SHIFT = {ROPE_WIN: HEAD // 4, ROPE_DIFF: DIFF_QK // 4}
_STATE_COLS = ((A_WK, WIN_KV * HEAD), (A_WV, WIN_KV * HEAD),
               (A_DK, DIFF_HEADS * HEAD), (A_DV, DIFF_HEADS * HEAD))
_STATE_ROWS_BY_HEAD = (True, True, False, False)


def _state_block(idx):
    width = _STATE_COLS[idx][1]
    return (CTX_LEN * width // HEAD, HEAD) if _STATE_ROWS_BY_HEAD[idx] else (CTX_LEN, width)
DIFF_Q_SCALE = DIFF_QK ** -0.5 * math.log2(math.e)


def _rope_tables():
    t = np.arange(LAT_LEN, dtype=np.int32)
    rows = (t // GRID_W).astype(np.float32)[:, None]
    cols = (t % GRID_W).astype(np.float32)[:, None]
    lane = np.arange(HEAD, dtype=np.int32)[None, :]
    out = []
    for kind in (ROPE_WIN, ROPE_DIFF):
        s = _ROPE_SHIFT[kind]
        within = lane % (4 * s)
        freq = (within % s).astype(np.float32)
        inv = np.float32(ROPE_BASE) ** (-freq / np.float32(s))
        ang = (np.where(within < 2 * s, rows, cols) * inv).astype(np.float32)
        first = (within % (2 * s)) < s
        cos, sin = np.cos(ang), np.sin(ang)
        out += [cos, np.where(first, -sin, 0.0), np.where(first, 0.0, sin)]
    return jnp.asarray(np.stack(out).astype(np.float32))


def _attn_proj_kernel(first, layer, plan, h_ref, w_ref, tab_ref, sink_ref, lam_ref, g_ref, *refs):
    n_in = _n_cast_inputs(plan)
    n_st = len(_STATE_COLS)
    pos = (0 if first else n_st) + n_in
    cast_in = refs[pos - n_in:pos]
    qkv_ref = refs[pos]
    state_refs = refs[pos + 1:pos + 1 + n_st]
    yb_ref, yc_ref = refs[pos + 1 + n_st:pos + 3 + n_st]
    cast_out = refs[pos + 3 + n_st:-2]
    tiles = refs[-2:]
    i = pl.program_id(0)

    def project(ctx, dst_ref):
        h = h_ref[...]
        if ctx and first:
            for ref in state_refs:
                ref[1:] = jnp.zeros((N_LAYERS - 1,) + ref.shape[1:], jnp.float32)
        for c0 in range(0, ATTN_COLS, ATTN_CHUNK):
            acc = _dot(h, w_ref[:, c0:c0 + ATTN_CHUNK])
            for s0 in range(0, ATTN_CHUNK, HEAD):
                col = c0 + s0
                x = acc[:, s0:s0 + HEAD]
                kind = _SLAB_KIND[col // HEAD]
                if ctx:
                    for ref, (src, width), by_head in zip(state_refs, _STATE_COLS, _STATE_ROWS_BY_HEAD):
                        if src <= col < src + width:
                            dst = col - src
                            if by_head:
                                where = (pl.ds(dst // HEAD, CTX_LEN, stride=width // HEAD), slice(None))
                            else:
                                where = (slice(None), slice(dst, dst + HEAD))
                            if first:
                                ref[(0,) + where] = x
                            else:
                                ref[where] = x
                elif kind != ROPE_NONE:
                    s = _ROPE_SHIFT[kind]
                    x = (x * tab_ref[3 * kind] + pltpu.roll(x, HEAD - s, 1) * tab_ref[3 * kind + 1]
                         + pltpu.roll(x, s, 1) * tab_ref[3 * kind + 2])
                if A_DQ <= col < A_DK:
                    x = x * DIFF_Q_SCALE
                dst_ref[:, col:col + HEAD] = _bf(x)

    def step(ctx, parity, attend):
        def run():
            _run_casts(plan, cast_in, cast_out)
            if attend:
                _ctx_attention(layer, tiles[1 - parity], sink_ref, lam_ref, g_ref, yb_ref, yc_ref)
            project(ctx, tiles[parity] if ctx else qkv_ref)
        return run

    @pl.when(i == 0)
    def _():
        tiles[1][...] = jnp.zeros_like(tiles[1])

    is_ctx = i < N_CTX_SEQ
    for parity in (0, 1):
        pl.when(is_ctx & (i % 2 == parity))(step(True, parity, True))
    pl.when(i == N_CTX_SEQ)(step(False, N_CTX_SEQ % 2, True))
    pl.when(i > N_CTX_SEQ)(step(False, 0, False))


def _attn_proj(h, w_attn, tables, sink, lam_params, subln, layer, prev_states, jobs):
    tm = ATTN_TM
    per_lat = LAT_LEN // tm
    first = prev_states is None
    last_ctx = N_CTX_SEQ - 1

    def state_spec(idx):
        if first:
            return pl.BlockSpec((None, N_LAYERS) + _state_block(idx), lambda i: (jnp.minimum(i, last_ctx), 0, 0, 0))
        return pl.BlockSpec((None, None) + _state_block(idx), lambda i: (jnp.minimum(i, last_ctx), layer, 0, 0))

    in_specs = [pl.BlockSpec((tm, D), lambda i: (i, 0)),
                pl.BlockSpec((D, ATTN_COLS), lambda i: (0, 0), pipeline_mode=pl.Buffered(1)),
                pl.BlockSpec((6, tm, HEAD), lambda i: (0, jnp.maximum(i - N_CTX_SEQ, 0) % per_lat, 0)),
                pl.BlockSpec(memory_space=pltpu.SMEM),
                pl.BlockSpec((None, 4, DIFF_QK), lambda i: (layer, 0, 0)),
                _vec_spec(layer, HEAD)]
    args = [h, w_attn, tables, sink, lam_params, subln]
    aliases = {}
    if not first:
        for k, st in enumerate(prev_states):
            in_specs.append(pl.BlockSpec(memory_space=pl.ANY))
            aliases[len(args)] = 1 + k
            args.append(st)
    c_in, c_args, c_out, c_shapes = _cast_plumbing(jobs, ROWS // tm, lambda i: i)
    y_spec = pl.BlockSpec((CTX_LEN, WIN_HEADS * HEAD), lambda i: (jnp.clip(i - 1, 0, last_ctx), 0))
    y_shape = jax.ShapeDtypeStruct((ROWS_CTX, WIN_HEADS * HEAD), jnp.bfloat16)
    out = pl.pallas_call(
        functools.partial(_attn_proj_kernel, first, layer, _cast_plan(jobs)),
        grid=(ROWS // tm,),
        in_specs=in_specs + c_in,
        out_specs=[pl.BlockSpec((tm, ATTN_COLS), lambda i: (jnp.maximum(i - N_CTX_SEQ, 0), 0))]
                  + [state_spec(idx) for idx in range(len(_STATE_COLS))] + [y_spec, y_spec] + c_out,
        out_shape=[jax.ShapeDtypeStruct((ROWS_LAT, ATTN_COLS), jnp.bfloat16)]
                  + [jax.ShapeDtypeStruct((N_CTX_SEQ, N_LAYERS) + _state_block(idx), jnp.float32)
                     for idx in range(len(_STATE_COLS))] + [y_shape, y_shape] + c_shapes,
        scratch_shapes=[pltpu.VMEM((CTX_LEN, ATTN_COLS), jnp.bfloat16)] * 2,
        input_output_aliases=aliases,
        compiler_params=_params(("arbitrary",)),
        name="attn_proj",
    )(*args, *c_args)
    return out[0], out[1:5], out[5], out[6], out[7:]


def _slab(ref, col0, idx):
    return ref[:, col0 + idx * HEAD:col0 + (idx + 1) * HEAD]


def _with_ones(v):
    return jnp.concatenate([v, jnp.ones_like(v)], axis=1)


def _pipelined(items, scores, finish):
    s = scores(items[0])
    for r, item in enumerate(items):
        s_next = scores(items[r + 1]) if r + 1 < len(items) else None
        finish(item, s)
        s = s_next


WIN_STAGE = WIN_GROUP
WIN_STAGES = tuple(range(0, WIN_HEADS, WIN_STAGE))


def _win_scores(q_ref, col0, h0, k, bias):
    q = jnp.concatenate([_slab(q_ref, col0, h0 + hh) for hh in range(WIN_STAGE)], axis=0)
    s = _dot_nt(q, k) * (HEAD ** -0.5)
    return s if bias is None else s + bias


def _win_finish(layer, h0, s, v, sink_ref, o_ref):
    rows = s.shape[0] // WIN_STAGE
    sink = jnp.concatenate(
        [jnp.full((rows, 1), sink_ref[layer * WIN_HEADS + h0 + hh], jnp.float32)
         for hh in range(WIN_STAGE)], axis=0)
    m = jnp.maximum(jnp.max(s, axis=-1, keepdims=True), sink)
    r = _dot(_bf(jnp.exp(s - m)), _with_ones(v))
    o = r[:, :HEAD] / (r[:, HEAD:HEAD + 1] + jnp.exp(sink - m))
    for hh in range(WIN_STAGE):
        head = h0 + hh
        o_ref[:, head * HEAD:(head + 1) * HEAD] = _bf(o[hh * rows:(hh + 1) * rows])


def _lat_win_kernel(layer, plan, q_ref, kc_ref, vc_ref, kseq_ref, vseq_ref, sink_ref, *refs):
    n_in = _n_cast_inputs(plan)
    o_ref = refs[n_in]
    _run_casts(plan, refs[:n_in], refs[n_in + 1:])
    n = pl.program_id(1)
    start = pl.multiple_of(jnp.clip((n - 1) * QBLK, 0, LAT_LEN - 3 * QBLK), QBLK)
    kb = kseq_ref[pl.ds(start, 3 * QBLK), :]
    vb = vseq_ref[pl.ds(start, 3 * QBLK), :]
    shape = (QBLK, PAST + 3 * QBLK)
    col = lax.broadcasted_iota(jnp.int32, shape, 1)
    qpos = n * QBLK + lax.broadcasted_iota(jnp.int32, shape, 0)
    kpos = start + col - PAST
    bias = jnp.where((col < PAST) | (jnp.abs(qpos - kpos) <= WINDOW), 0.0, NEG)
    bias = jnp.concatenate([bias] * WIN_STAGE, axis=0)

    def keys(ref, band, h0):
        g = h0 // WIN_GROUP
        cached = ref[pl.ds(g, PAST, stride=WIN_KV), :]
        return jnp.concatenate([_bf(cached), band[:, g * HEAD:(g + 1) * HEAD]], axis=0)

    _pipelined(WIN_STAGES,
               lambda h0: _win_scores(q_ref, 0, h0, keys(kc_ref, kb, h0), bias),
               lambda h0, s: _win_finish(layer, h0, s, keys(vc_ref, vb, h0), sink_ref, o_ref))


def _lat_win_attn(qkv, cache_k, cache_v, sink, layer, jobs):
    nb = LAT_LEN // QBLK
    kvw = WIN_KV * HEAD
    cache_spec = pl.BlockSpec((None, None, PAST * WIN_KV, HEAD), lambda b, n: (b, layer, 0, 0))

    def seq_spec(col0):
        return pl.BlockSpec((LAT_LEN, kvw), lambda b, n: (b, col0 // kvw))

    c_in, c_args, c_out, c_shapes = _cast_plumbing(jobs, N_LAT_SEQ * nb, lambda b, n: b * nb + n)
    out = pl.pallas_call(
        functools.partial(_lat_win_kernel, layer, _cast_plan(jobs)),
        grid=(N_LAT_SEQ, nb),
        in_specs=[pl.BlockSpec((QBLK, WIN_HEADS * HEAD), lambda b, n: (b * nb + n, 0)),
                  cache_spec, cache_spec, seq_spec(A_WK), seq_spec(A_WV),
                  pl.BlockSpec(memory_space=pltpu.SMEM)] + c_in,
        out_specs=[pl.BlockSpec((QBLK, WIN_HEADS * HEAD), lambda b, n: (b * nb + n, 0))] + c_out,
        out_shape=[jax.ShapeDtypeStruct((ROWS_LAT, WIN_HEADS * HEAD), jnp.bfloat16)] + c_shapes,
        compiler_params=_params(("arbitrary", "arbitrary")),
        name="lat_win_attn",
    )(qkv, cache_k, cache_v, qkv, qkv, sink, *c_args)
    return out[0], out[1:]


def _lam_init(layer):
    return 0.8 - 0.6 * math.exp(-0.3 * layer)


def _diff_lambda(layer, lam_ref):
    lp = lam_ref[...]
    return (jnp.exp(jnp.sum(lp[0:1] * lp[1:2], axis=-1, keepdims=True))
            - jnp.exp(jnp.sum(lp[2:3] * lp[3:4], axis=-1, keepdims=True)) + _lam_init(layer))


def _diff_scores(q, k):
    first = lax.broadcasted_iota(jnp.int32, q.shape, 1) < DIFF_QK
    zero = jnp.zeros_like(q)
    return _dot_nt(jnp.concatenate([jnp.where(first, q, zero), jnp.where(first, zero, q)], axis=0), k)


def _diff_finish(layer, s, v_ones, lam, g):
    tq = s.shape[0] // 2
    e = _bf(jnp.exp2(s - jnp.max(s, axis=-1, keepdims=True)))
    r = _dot(e, v_ones)
    o = (r[:tq, :HEAD] * (1.0 / r[:tq, HEAD:HEAD + 1])
         - r[tq:, :HEAD] * (lam / r[tq:, HEAD:HEAD + 1]))
    o = o * lax.rsqrt(jnp.mean(o * o, axis=-1, keepdims=True) + EPS) * g
    return _bf(o * (1.0 - _lam_init(layer)))


def _ctx_attention(layer, qkv_ref, sink_ref, lam_ref, g_ref, yb_ref, yc_ref):
    lam = _diff_lambda(layer, lam_ref)
    stages = [("win", h0) for h0 in WIN_STAGES] + [("diff", h) for h in range(DIFF_HEADS)]

    def scores(stage):
        kind, idx = stage
        if kind == "win":
            return _win_scores(qkv_ref, A_WQ, idx, _slab(qkv_ref, A_WK, idx // WIN_GROUP), None)
        return _diff_scores(_slab(qkv_ref, A_DQ, idx), _slab(qkv_ref, A_DK, idx))

    def finish(stage, s):
        kind, idx = stage
        if kind == "win":
            _win_finish(layer, idx, s, _slab(qkv_ref, A_WV, idx // WIN_GROUP), sink_ref, yb_ref)
        else:
            yc_ref[:, idx * HEAD:(idx + 1) * HEAD] = _diff_finish(
                layer, s, _with_ones(_slab(qkv_ref, A_DV, idx)), lam, g_ref[...])

    _pipelined(stages, scores, finish)


LAT_DIFF_TQ = 2048
LAT_DIFF_SUB = 256


def _lat_diff_kernel(layer, plan, q_ref, kc_ref, vc_ref, k_ref, v_ref, lam_ref, g_ref, *refs):
    n_in = _n_cast_inputs(plan)
    o_ref = refs[n_in]
    kall_ref, vall_ref = refs[-2:]

    @pl.when(pl.program_id(2) == 0)
    def _():
        head_rows = pl.ds(pl.program_id(1), PAST, stride=DIFF_HEADS)
        kall_ref[0:PAST] = _bf(kc_ref[head_rows, :])
        kall_ref[PAST:] = k_ref[...]
        vall_ref[0:PAST] = _with_ones(_bf(vc_ref[head_rows, :]))
        vall_ref[PAST:] = _with_ones(v_ref[...])

    _run_casts(plan, refs[:n_in], refs[n_in + 1:-2])
    lam = _diff_lambda(layer, lam_ref)

    def finish(r0, s):
        o_ref[r0:r0 + LAT_DIFF_SUB] = _diff_finish(layer, s, vall_ref[...], lam, g_ref[...])

    _pipelined(list(range(0, LAT_DIFF_TQ, LAT_DIFF_SUB)),
               lambda r0: _diff_scores(q_ref[r0:r0 + LAT_DIFF_SUB], kall_ref[...]), finish)


def _lat_diff_attn(qkv, cache_k, cache_v, lam_params, subln, layer, jobs):
    tq = LAT_DIFF_TQ
    nq = LAT_LEN // tq
    cache_spec = pl.BlockSpec((None, None, PAST * DIFF_HEADS, HEAD), lambda b, h, t: (b, layer, 0, 0))

    def seq_spec(col0):
        return pl.BlockSpec((LAT_LEN, HEAD), lambda b, h, t: (b, col0 // HEAD + h))

    c_in, c_args, c_out, c_shapes = _cast_plumbing(
        jobs, N_LAT_SEQ * DIFF_HEADS * nq, lambda b, h, t: (b * DIFF_HEADS + h) * nq + t)
    out = pl.pallas_call(
        functools.partial(_lat_diff_kernel, layer, _cast_plan(jobs)),
        grid=(N_LAT_SEQ, DIFF_HEADS, nq),
        in_specs=[pl.BlockSpec((tq, HEAD), lambda b, h, t: (b * nq + t, A_DQ // HEAD + h)),
                  cache_spec, cache_spec, seq_spec(A_DK), seq_spec(A_DV),
                  pl.BlockSpec((None, 4, DIFF_QK), lambda b, h, t: (layer, 0, 0)),
                  _vec_spec(layer, HEAD)] + c_in,
        out_specs=[pl.BlockSpec((tq, HEAD), lambda b, h, t: (b * nq + t, h))] + c_out,
        out_shape=[jax.ShapeDtypeStruct((ROWS_LAT, DIFF_HEADS * HEAD), jnp.bfloat16)] + c_shapes,
        scratch_shapes=[pltpu.VMEM((PAST + LAT_LEN, HEAD), jnp.bfloat16),
                        pltpu.VMEM((PAST + LAT_LEN, 2 * HEAD), jnp.bfloat16)],
        compiler_params=_params(("arbitrary", "arbitrary", "arbitrary")),
        name="lat_diff_attn",
    )(qkv, cache_k, cache_v, qkv, qkv, lam_params, subln, *c_args)
    return out[0], out[1:]


MERGE_TM = 1024
MERGE_TN = 512


def _merge_kernel(h_ref, ya_ref, ybc_ref, ybl_ref, ycc_ref, ycl_ref,
                  wga_ref, wgb_ref, wgc_ref, wa_ref, wb_ref, wc_ref, o_ref):
    h = h_ref[...]
    yb = _pick_rows(MERGE_TM, ybc_ref, ybl_ref)
    yc = _pick_rows(MERGE_TM, ycc_ref, ycl_ref)
    m = _sigmoid(_dot(h, wga_ref[...])) * _dot(ya_ref[...], wa_ref[...])
    m += _sigmoid(_dot(h, wgb_ref[...])) * _dot(yb, wb_ref[...])
    m += _sigmoid(_dot(h, wgc_ref[...])) * _dot(yc, wc_ref[...])
    o_ref[...] = _bf(m)


def _merge(h, ya, yb_ctx, yb_lat, yc_ctx, yc_lat, w_gate, wa, wb, wc):
    tm, tn = MERGE_TM, MERGE_TN
    nj = D // tn

    def gate_spec(branch):
        return pl.BlockSpec((D, tn), lambda i, j: (0, branch * nj + j))

    w_spec = pl.BlockSpec((CONV_DIM, tn), lambda i, j: (0, j))
    return pl.pallas_call(
        _merge_kernel,
        grid=(ROWS // tm, nj),
        in_specs=[pl.BlockSpec((tm, D), lambda i, j: (i, 0)),
                  pl.BlockSpec((tm, CONV_DIM), lambda i, j: (i, 0)),
                  *_split_specs(tm, WIN_HEADS * HEAD), *_split_specs(tm, DIFF_HEADS * HEAD),
                  gate_spec(0), gate_spec(1), gate_spec(2), w_spec, w_spec, w_spec],
        out_specs=pl.BlockSpec((tm, tn), lambda i, j: (i, j)),
        out_shape=jax.ShapeDtypeStruct((ROWS, D), jnp.bfloat16),
        compiler_params=_params(("parallel", "arbitrary")),
        name="merge",
    )(h, ya, yb_ctx, yb_lat, yc_ctx, yc_lat, w_gate, w_gate, w_gate, wa, wb, wc)


OUT_TM = 512


def _out_proj_kernel(n_x, *refs):
    x_refs = refs[:n_x]
    m_ref, w_ref, gate_ref, g_ref, shift_ref, scale_ref, xo_ref, h_ref = refs[n_x:]
    x = x_refs[0][...] if n_x == 1 else _pick_rows(OUT_TM, *x_refs)
    x = x + gate_ref[...] * _dot(m_ref[...], w_ref[...])
    xo_ref[...] = x
    h_ref[...] = _bf(_rms_mod(x, g_ref[...], scale_ref[...], shift_ref[...]))


def _out_proj(xs, merged, w_out, mods, norm_mlp, layer):
    tm = OUT_TM
    row = pl.BlockSpec((tm, D), lambda i: (i, 0))
    x_specs = [row] if len(xs) == 1 else list(_split_specs(tm, D))
    return pl.pallas_call(
        functools.partial(_out_proj_kernel, len(xs)),
        grid=(ROWS // tm,),
        in_specs=x_specs + [row,
                            pl.BlockSpec((D, D), lambda i: (0, 0)),
                            _mod_spec(layer, 2, tm), _vec_spec(layer, D),
                            _mod_spec(layer, 3, tm), _mod_spec(layer, 4, tm)],
        out_specs=[row, row],
        out_shape=[jax.ShapeDtypeStruct((ROWS, D), jnp.float32),
                   jax.ShapeDtypeStruct((ROWS, D), jnp.bfloat16)],
        compiler_params=_params(("arbitrary",)),
        name="out_proj",
    )(*xs, merged, w_out, mods, norm_mlp.reshape(N_LAYERS, 1, D), mods, mods)


MLP_TM = 512
MLP_TF = 1024


def _mlp_kernel(final, h_ref, w1_ref, w2_ref, x_ref, gate_ref, g_ref, shift_ref, scale_ref,
                *rest):
    if final:
        yc_ref, yl_ref, acc_ref = rest
    else:
        xo_ref, hn_ref, acc_ref = rest
    i = pl.program_id(0)
    j = pl.program_id(1)

    @pl.when(j == 0)
    def _():
        acc_ref[...] = jnp.zeros_like(acc_ref)

    t = jnp.maximum(_dot(h_ref[...], w1_ref[...]), 0.0)
    acc_ref[...] += _dot(_bf(t * t), w2_ref[...])

    @pl.when(j == pl.num_programs(1) - 1)
    def _():
        x = x_ref[...] + gate_ref[...] * acc_ref[...]
        if final:
            y = x * lax.rsqrt(jnp.mean(x * x, axis=-1, keepdims=True) + EPS) * g_ref[...]

            @pl.when(i < ROWS_CTX // MLP_TM)
            def _():
                yc_ref[...] = y

            @pl.when(i >= ROWS_CTX // MLP_TM)
            def _():
                yl_ref[...] = y
        else:
            xo_ref[...] = x
            hn_ref[...] = _bf(_rms_mod(x, g_ref[...], scale_ref[...], shift_ref[...]))


def _mlp(h2, x, w1, w2, mods, norm_next, layer, final):
    tm, tf = MLP_TM, MLP_TF
    row = pl.BlockSpec((tm, D), lambda i, j: (i, 0))
    nxt = layer if final else layer + 1
    g_spec = pl.BlockSpec((None, 1, D), lambda i, j: (0 if final else nxt, 0, 0))
    if final:
        out_specs = list(_split_specs(tm, D))
        out_shape = [jax.ShapeDtypeStruct((ROWS_CTX, D), jnp.float32),
                     jax.ShapeDtypeStruct((ROWS_LAT, D), jnp.float32)]
    else:
        out_specs = [row, row]
        out_shape = [jax.ShapeDtypeStruct((ROWS, D), jnp.float32),
                     jax.ShapeDtypeStruct((ROWS, D), jnp.bfloat16)]
    return pl.pallas_call(
        functools.partial(_mlp_kernel, final),
        grid=(ROWS // tm, D_FF // tf),
        in_specs=[row,
                  pl.BlockSpec((D, tf), lambda i, j: (0, j)),
                  pl.BlockSpec((tf, D), lambda i, j: (j, 0)),
                  row, _mod_spec(layer, 5, tm), g_spec,
                  _mod_spec(nxt, 0, tm), _mod_spec(nxt, 1, tm)],
        out_specs=out_specs,
        out_shape=out_shape,
        scratch_shapes=[pltpu.VMEM((tm, D), jnp.float32)],
        compiler_params=_params(("arbitrary", "arbitrary")),
        name="mlp_final" if final else "mlp",
    )(h2, w1, w2, x, mods, norm_next, mods, mods)


def kernel(x_prompt, x_sample, cache_win_k, cache_win_v, cache_diff_k, cache_diff_v, c, c_ctx, w_ada, b_ada, norm_mix, norm_mlp, w_in, conv_w, win_sink, lambda_q1, lambda_k1, lambda_q2, lambda_k2, diff_subln, w_branch_conv, w_branch_win, w_branch_diff, w_out, w_mlp1, w_mlp2, norm_final):
    xs = (x_prompt.reshape(ROWS_CTX, D), x_sample.reshape(ROWS_LAT, D))
    cvec = jnp.concatenate([c_ctx[None], c, jnp.zeros((N_MOD_ROWS - 1 - N_LAT_SEQ, D), jnp.float32)], axis=0)
    mods = _modulation(cvec, w_ada, b_ada).reshape(N_LAYERS * N_MOD_ROWS * N_MOD, 1, D)

    tables = _rope_tables()
    sink = win_sink.reshape(N_LAYERS * WIN_HEADS)
    lam_params = jnp.stack([lambda_q1, lambda_k1, lambda_q2, lambda_k2], axis=1)
    subln = diff_subln.reshape(N_LAYERS, 1, HEAD)
    ck_win = cache_win_k.reshape(N_LAT_SEQ, N_LAYERS, PAST * WIN_KV, HEAD)
    cv_win = cache_win_v.reshape(N_LAT_SEQ, N_LAYERS, PAST * WIN_KV, HEAD)
    ck_diff = cache_diff_k.reshape(N_LAT_SEQ, N_LAYERS, PAST * DIFF_HEADS, HEAD)
    cv_diff = cache_diff_v.reshape(N_LAT_SEQ, N_LAYERS, PAST * DIFF_HEADS, HEAD)
    norm_final_3d = norm_final.reshape(1, 1, D)
    norm_mix_3d = norm_mix.reshape(N_LAYERS, 1, D)

    h = _prologue(*xs, mods, norm_mix, 0)
    states = None
    for layer in range(N_LAYERS):
        in_job = functools.partial(_CastJob, w_in, layer, cbw=IN_CBW)
        ya, (w_attn,) = _conv_branch(h, w_in, conv_w, layer,
                                     [in_job(k0=OFF_ATTN // IN_CBW, nb=ATTN_COLS // IN_CBW)])
        qkv, states, yb_ctx, yc_ctx, (w_gate,) = _attn_proj(
            h, w_attn, tables, sink, lam_params, subln, layer, states,
            [in_job(k0=OFF_GATE // IN_CBW, nb=3 * D // IN_CBW)])
        yb_lat, (wa, wb, wc, w_o) = _lat_win_attn(
            qkv, ck_win, cv_win, sink, layer,
            [_whole(w, layer) for w in (w_branch_conv, w_branch_win, w_branch_diff, w_out)])
        yc_lat, (w1, w2) = _lat_diff_attn(qkv, ck_diff, cv_diff, lam_params, subln, layer,
                                          [_whole(w_mlp1, layer), _whole(w_mlp2, layer)])
        merged = _merge(h, ya, yb_ctx, yb_lat, yc_ctx, yc_lat, w_gate, wa, wb, wc)
        x, h2 = _out_proj(xs, merged, w_o, mods, norm_mlp, layer)
        if layer == N_LAYERS - 1:
            y_ctx, y_lat = _mlp(h2, x, w1, w2, mods, norm_final_3d, layer, True)
        else:
            x, h = _mlp(h2, x, w1, w2, mods, norm_mix_3d, layer, False)
            xs = (x,)

    new_win_k, new_win_v, new_diff_k, new_diff_v = states
    return (y_ctx.reshape(N_CTX_SEQ, CTX_LEN, D), y_lat.reshape(N_LAT_SEQ, LAT_LEN, D),
            new_win_k.reshape(N_CTX_SEQ, N_LAYERS, CTX_LEN, WIN_KV, HEAD),
            new_win_v.reshape(N_CTX_SEQ, N_LAYERS, CTX_LEN, WIN_KV, HEAD),
            new_diff_k.reshape(N_CTX_SEQ, N_LAYERS, CTX_LEN, DIFF_HEADS, HEAD),
            new_diff_v.reshape(N_CTX_SEQ, N_LAYERS, CTX_LEN, DIFF_HEADS, HEAD))
```

```python
import functools
import math
from typing import NamedTuple

import jax
import jax.numpy as jnp
import numpy as np
from jax import lax
from jax.experimental import pallas as pl
from jax.experimental.pallas import tpu as pltpu

D = 2048
N_CTX_SEQ = 16
CTX_LEN = 256
N_LAT_SEQ = 2
LAT_LEN = 2048
N_LAYERS = 2
PAST = 512
GRID_W = 64
HEAD = 128
CONV_DIM = 1024
WIN_HEADS = 8
WIN_KV = 2
WIN_GROUP = WIN_HEADS // WIN_KV
WINDOW = 128
QBLK = 128
DIFF_HEADS = 8
DIFF_QK = 64
D_FF = 4 * D
EPS = 1e-6
ROPE_BASE = 10000.0
NEG = -1e30

ROWS_CTX = N_CTX_SEQ * CTX_LEN
ROWS_LAT = N_LAT_SEQ * LAT_LEN
ROWS = ROWS_CTX + ROWS_LAT
N_MOD_ROWS = 8
N_MOD = 6

OFF_CONV = 0
OFF_ATTN = 3 * CONV_DIM
ATTN_COLS = (WIN_HEADS + 2 * WIN_KV) * HEAD + 3 * DIFF_HEADS * HEAD
OFF_GATE = OFF_ATTN + ATTN_COLS
IN_TOTAL = OFF_GATE + 3 * D
IN_CBW = 1536
A_WQ, A_WK, A_WV = 0, WIN_HEADS * HEAD, (WIN_HEADS + WIN_KV) * HEAD
A_DQ = (WIN_HEADS + 2 * WIN_KV) * HEAD
A_DK = A_DQ + DIFF_HEADS * HEAD
A_DV = A_DK + DIFF_HEADS * HEAD

VMEM_LIMIT_V7X = 60 * 1024 * 1024


def _params(sem, vmem=VMEM_LIMIT_V7X):
    return pltpu.CompilerParams(dimension_semantics=sem, vmem_limit_bytes=vmem)


def _bf(x):
    return x.astype(jnp.bfloat16)


def _dot(a, b):
    return jnp.dot(a, b, preferred_element_type=jnp.float32)


def _dot_nt(a, b):
    return lax.dot_general(a, b, (((1,), (1,)), ((), ())), preferred_element_type=jnp.float32)


def _sigmoid(x):
    return 1.0 / (1.0 + jnp.exp(-x))


def _mod_group(i, tm):
    n_ctx = ROWS_CTX // tm
    return jnp.where(i < n_ctx, 0, 1 + (i - n_ctx) // (LAT_LEN // tm))


def _mod_spec(layer, which, tm):
    def imap(i, *_):
        return (layer * N_MOD_ROWS * N_MOD + _mod_group(i, tm) * N_MOD + which, 0, 0)
    return pl.BlockSpec((None, 1, D), imap)


def _vec_spec(layer, width):
    return pl.BlockSpec((None, 1, width), lambda *_: (layer, 0, 0))


def _rms_mod(x, g, scale, shift):
    y = x * lax.rsqrt(jnp.mean(x * x, axis=-1, keepdims=True) + EPS) * g
    return y * (1.0 + scale) + shift


def _mod_kernel(c_ref, w_ref, b_ref, o_ref):
    c = c_ref[...]
    s = c * _sigmoid(c)
    o_ref[...] = _dot(_bf(s), _bf(w_ref[...])) + b_ref[...]


def _modulation(cvec, w_ada, b_ada):
    tn = 1024
    n = N_MOD * D
    return pl.pallas_call(
        _mod_kernel,
        grid=(N_LAYERS, n // tn),
        in_specs=[pl.BlockSpec((N_MOD_ROWS, D), lambda l, j: (0, 0)),
                  pl.BlockSpec((None, D, tn), lambda l, j: (l, 0, j)),
                  pl.BlockSpec((None, 1, tn), lambda l, j: (l, 0, j))],
        out_specs=pl.BlockSpec((None, N_MOD_ROWS, tn), lambda l, j: (l, 0, j)),
        out_shape=jax.ShapeDtypeStruct((N_LAYERS, N_MOD_ROWS, n), jnp.float32),
        compiler_params=_params(("parallel", "parallel")),
        name="modulation",
    )(cvec, w_ada, b_ada.reshape(N_LAYERS, 1, n))


def _split_specs(tm, width):
    n_ctx = ROWS_CTX // tm
    return (pl.BlockSpec((tm, width), lambda i, *_: (jnp.minimum(i, n_ctx - 1), 0)),
            pl.BlockSpec((tm, width), lambda i, *_: (jnp.maximum(i - n_ctx, 0), 0)))


def _pick_rows(tm, ctx_ref, lat_ref):
    return jnp.where(pl.program_id(0) < ROWS_CTX // tm, ctx_ref[...], lat_ref[...])


PRO_TM = 1024


def _prologue_kernel(xc_ref, xl_ref, g_ref, shift_ref, scale_ref, h_ref):
    x = _pick_rows(PRO_TM, xc_ref, xl_ref)
    h_ref[...] = _bf(_rms_mod(x, g_ref[...], scale_ref[...], shift_ref[...]))


def _prologue(x_ctx, x_lat, mods, norm_mix, layer):
    tm = PRO_TM
    return pl.pallas_call(
        _prologue_kernel,
        grid=(ROWS // tm,),
        in_specs=[*_split_specs(tm, D),
                  _vec_spec(layer, D),
                  _mod_spec(layer, 0, tm),
                  _mod_spec(layer, 1, tm)],
        out_specs=pl.BlockSpec((tm, D), lambda i: (i, 0)),
        out_shape=jax.ShapeDtypeStruct((ROWS, D), jnp.bfloat16),
        compiler_params=_params(("arbitrary",)),
        name="prologue",
    )(x_ctx, x_lat, norm_mix.reshape(N_LAYERS, 1, D), mods, mods)


class _CastJob(NamedTuple):
    src: jax.Array
    layer: int
    k0: int
    nb: int
    cbw: int

    @property
    def rows(self):
        return self.src.shape[1]


def _whole(src, layer):
    return _CastJob(src, layer, 0, 1, src.shape[2])


def _cast_plumbing(jobs, n_steps, step_of):
    in_specs, args, out_specs, out_shapes = [], [], [], []
    for job in jobs:
        rows = job.rows // n_steps
        assert rows * n_steps == job.rows and rows % 16 == 0
        for k in range(job.nb):
            in_specs.append(pl.BlockSpec(
                (None, rows, job.cbw), lambda *g, job=job, k=k: (job.layer, step_of(*g), job.k0 + k)))
            args.append(job.src)
        out_specs.append(pl.BlockSpec((rows, job.nb * job.cbw), lambda *g: (step_of(*g), 0)))
        out_shapes.append(jax.ShapeDtypeStruct((job.rows, job.nb * job.cbw), jnp.bfloat16))
    return in_specs, args, out_specs, out_shapes


def _cast_plan(jobs):
    return tuple((job.nb, job.cbw) for job in jobs)


def _run_casts(plan, in_refs, out_refs):
    in_refs = iter(in_refs)
    for (nb, cbw), out_ref in zip(plan, out_refs):
        for k in range(nb):
            out_ref[:, k * cbw:(k + 1) * cbw] = _bf(next(in_refs)[...])


def _n_cast_inputs(plan):
    return sum(nb for nb, _ in plan)


CONV_TM = LAT_LEN
CONV_TN = 256


def _conv_kernel(plan, h_ref, wb_ref, wc_ref, wx_ref, cw_ref, *refs):
    n_in = _n_cast_inputs(plan)
    o_ref = refs[n_in]
    _run_casts(plan, refs[:n_in], refs[n_in + 1:])
    i = pl.program_id(0)
    h = h_ref[...]
    cb = _dot(h, _bf(wb_ref[...]))
    z = _dot(h, _bf(wc_ref[...])) * _dot(h, _bf(wx_ref[...]))
    seq_mask = jnp.where(i < ROWS_CTX // CONV_TM, CTX_LEN - 1, LAT_LEN - 1)
    pos = lax.broadcasted_iota(jnp.int32, z.shape, 0) & seq_mask
    z_prev = jnp.where(pos == 0, 0.0, pltpu.roll(z, 1, 0))
    z_next = jnp.where(pos == seq_mask, 0.0, pltpu.roll(z, CONV_TM - 1, 0))
    cw = cw_ref[...]
    conv = z_prev * cw[0:1, :] + z * cw[1:2, :] + z_next * cw[2:3, :]
    o_ref[...] = _bf(cb * conv)


def _conv_branch(h, w_in, conv_w, layer, jobs):
    nj = CONV_DIM // CONV_TN
    ni = ROWS // CONV_TM

    def wspec(group):
        return pl.BlockSpec((None, D, CONV_TN), lambda i, j: (layer, 0, group * nj + j))

    c_in, c_args, c_out, c_shapes = _cast_plumbing(jobs, ni * nj, lambda i, j: i * nj + j)
    out = pl.pallas_call(
        functools.partial(_conv_kernel, _cast_plan(jobs)),
        grid=(ni, nj),
        in_specs=[pl.BlockSpec((CONV_TM, D), lambda i, j: (i, 0)),
                  wspec(0), wspec(1), wspec(2),
                  pl.BlockSpec((None, 3, CONV_TN), lambda i, j: (layer, 0, j))] + c_in,
        out_specs=[pl.BlockSpec((CONV_TM, CONV_TN), lambda i, j: (i, j))] + c_out,
        out_shape=[jax.ShapeDtypeStruct((ROWS, CONV_DIM), jnp.bfloat16)] + c_shapes,
        compiler_params=_params(("arbitrary", "arbitrary")),
        name="conv_branch",
    )(h, w_in, w_in, w_in, conv_w, *c_args)
    return out[0], out[1:]


ATTN_TM = CTX_LEN
ATTN_CHUNK = 512
ROPE_WIN, ROPE_DIFF, ROPE_NONE = 0, 1, 2
_SLAB_KIND = ([ROPE_WIN] * (WIN_HEADS + WIN_KV) + [ROPE_NONE] * WIN_KV
              + [ROPE_DIFF] * (2 * DIFF_HEADS) + [ROPE_NONE] * DIFF_HEADS)
_ROPE_SHIFT = {ROPE_WIN: HEAD // 4, ROPE_DIFF: DIFF_QK // 4}
_STATE_COLS = ((A_WK, WIN_KV * HEAD), (A_WV, WIN_KV * HEAD),
               (A_DK, DIFF_HEADS * HEAD), (A_DV, DIFF_HEADS * HEAD))
_STATE_ROWS_BY_HEAD = (True, True, False, False)


def _state_block(idx):
    width = _STATE_COLS[idx][1]
    return (CTX_LEN * width // HEAD, HEAD) if _STATE_ROWS_BY_HEAD[idx] else (CTX_LEN, width)
DIFF_Q_SCALE = DIFF_QK ** -0.5 * math.log2(math.e)


def _rope_tables():
    t = np.arange(LAT_LEN, dtype=np.int32)
    rows = (t // GRID_W).astype(np.float32)[:, None]
    cols = (t % GRID_W).astype(np.float32)[:, None]
    lane = np.arange(HEAD, dtype=np.int32)[None, :]
    out = []
    for kind in (ROPE_WIN, ROPE_DIFF):
        s = _ROPE_SHIFT[kind]
        within = lane % (4 * s)
        freq = (within % s).astype(np.float32)
        inv = np.float32(ROPE_BASE) ** (-freq / np.float32(s))
        ang = (np.where(within < 2 * s, rows, cols) * inv).astype(np.float32)
        first = (within % (2 * s)) < s
        cos, sin = np.cos(ang), np.sin(ang)
        out += [cos, np.where(first, -sin, 0.0), np.where(first, 0.0, sin)]
    return jnp.asarray(np.stack(out).astype(np.float32))


def _attn_proj_kernel(first, layer, plan, h_ref, w_ref, tab_ref, sink_ref, lam_ref, g_ref, *refs):
    n_in = _n_cast_inputs(plan)
    n_st = len(_STATE_COLS)
    pos = (0 if first else n_st) + n_in
    cast_in = refs[pos - n_in:pos]
    qkv_ref = refs[pos]
    state_refs = refs[pos + 1:pos + 1 + n_st]
    yb_ref, yc_ref = refs[pos + 1 + n_st:pos + 3 + n_st]
    cast_out = refs[pos + 3 + n_st:-2]
    tiles = refs[-2:]
    i = pl.program_id(0)

    def project(ctx, dst_ref):
        h = h_ref[...]
        if ctx and first:
            for ref in state_refs:
                ref[1:] = jnp.zeros((N_LAYERS - 1,) + ref.shape[1:], jnp.float32)
        for c0 in range(0, ATTN_COLS, ATTN_CHUNK):
            acc = _dot(h, w_ref[:, c0:c0 + ATTN_CHUNK])
            for s0 in range(0, ATTN_CHUNK, HEAD):
                col = c0 + s0
                x = acc[:, s0:s0 + HEAD]
                kind = _SLAB_KIND[col // HEAD]
                if ctx:
                    for ref, (src, width), by_head in zip(state_refs, _STATE_COLS, _STATE_ROWS_BY_HEAD):
                        if src <= col < src + width:
                            dst = col - src
                            if by_head:
                                where = (pl.ds(dst // HEAD, CTX_LEN, stride=width // HEAD), slice(None))
                            else:
                                where = (slice(None), slice(dst, dst + HEAD))
                            if first:
                                ref[(0,) + where] = x
                            else:
                                ref[where] = x
                elif kind != ROPE_NONE:
                    s = _ROPE_SHIFT[kind]
                    x = (x * tab_ref[3 * kind] + pltpu.roll(x, HEAD - s, 1) * tab_ref[3 * kind + 1]
                         + pltpu.roll(x, s, 1) * tab_ref[3 * kind + 2])
                if A_DQ <= col < A_DK:
                    x = x * DIFF_Q_SCALE
                dst_ref[:, col:col + HEAD] = _bf(x)

    def step(ctx, parity, attend):
        def run():
            _run_casts(plan, cast_in, cast_out)
            if attend:
                _ctx_attention(layer, tiles[1 - parity], sink_ref, lam_ref, g_ref, yb_ref, yc_ref)
            project(ctx, tiles[parity] if ctx else qkv_ref)
        return run

    @pl.when(i == 0)
    def _():
        tiles[1][...] = jnp.zeros_like(tiles[1])

    is_ctx = i < N_CTX_SEQ
    for parity in (0, 1):
        pl.when(is_ctx & (i % 2 == parity))(step(True, parity, True))
    pl.when(i == N_CTX_SEQ)(step(False, N_CTX_SEQ % 2, True))
    pl.when(i > N_CTX_SEQ)(step(False, 0, False))


def _attn_proj(h, w_attn, tables, sink, lam_params, subln, layer, prev_states, jobs):
    tm = ATTN_TM
    per_lat = LAT_LEN // tm
    first = prev_states is None
    last_ctx = N_CTX_SEQ - 1

    def state_spec(idx):
        if first:
            return pl.BlockSpec((None, N_LAYERS) + _state_block(idx), lambda i: (jnp.minimum(i, last_ctx), 0, 0, 0))
        return pl.BlockSpec((None, None) + _state_block(idx), lambda i: (jnp.minimum(i, last_ctx), layer, 0, 0))

    in_specs = [pl.BlockSpec((tm, D), lambda i: (i, 0)),
                pl.BlockSpec((D, ATTN_COLS), lambda i: (0, 0), pipeline_mode=pl.Buffered(1)),
                pl.BlockSpec((6, tm, HEAD), lambda i: (0, jnp.maximum(i - N_CTX_SEQ, 0) % per_lat, 0)),
                pl.BlockSpec(memory_space=pltpu.SMEM),
                pl.BlockSpec((None, 4, DIFF_QK), lambda i: (layer, 0, 0)),
                _vec_spec(layer, HEAD)]
    args = [h, w_attn, tables, sink, lam_params, subln]
    aliases = {}
    if not first:
        for k, st in enumerate(prev_states):
            in_specs.append(pl.BlockSpec(memory_space=pl.ANY))
            aliases[len(args)] = 1 + k
            args.append(st)
    c_in, c_args, c_out, c_shapes = _cast_plumbing(jobs, ROWS // tm, lambda i: i)
    y_spec = pl.BlockSpec((CTX_LEN, WIN_HEADS * HEAD), lambda i: (jnp.clip(i - 1, 0, last_ctx), 0))
    y_shape = jax.ShapeDtypeStruct((ROWS_CTX, WIN_HEADS * HEAD), jnp.bfloat16)
    out = pl.pallas_call(
        functools.partial(_attn_proj_kernel, first, layer, _cast_plan(jobs)),
        grid=(ROWS // tm,),
        in_specs=in_specs + c_in,
        out_specs=[pl.BlockSpec((tm, ATTN_COLS), lambda i: (jnp.maximum(i - N_CTX_SEQ, 0), 0))]
                  + [state_spec(idx) for idx in range(len(_STATE_COLS))] + [y_spec, y_spec] + c_out,
        out_shape=[jax.ShapeDtypeStruct((ROWS_LAT, ATTN_COLS), jnp.bfloat16)]
                  + [jax.ShapeDtypeStruct((N_CTX_SEQ, N_LAYERS) + _state_block(idx), jnp.float32)
                     for idx in range(len(_STATE_COLS))] + [y_shape, y_shape] + c_shapes,
        scratch_shapes=[pltpu.VMEM((CTX_LEN, ATTN_COLS), jnp.bfloat16)] * 2,
        input_output_aliases=aliases,
        compiler_params=_params(("arbitrary",)),
        name="attn_proj",
    )(*args, *c_args)
    return out[0], out[1:5], out[5], out[6], out[7:]


def _slab(ref, col0, idx):
    return ref[:, col0 + idx * HEAD:col0 + (idx + 1) * HEAD]


def _with_ones(v):
    return jnp.concatenate([v, jnp.ones_like(v)], axis=1)


def _pipelined(items, scores, finish):
    s = scores(items[0])
    for r, item in enumerate(items):
        s_next = scores(items[r + 1]) if r + 1 < len(items) else None
        finish(item, s)
        s = s_next


WIN_STAGE = WIN_GROUP
WIN_STAGES = tuple(range(0, WIN_HEADS, WIN_STAGE))


def _win_scores(q_ref, col0, h0, k, bias):
    q = jnp.concatenate([_slab(q_ref, col0, h0 + hh) for hh in range(WIN_STAGE)], axis=0)
    s = _dot_nt(q, k) * (HEAD ** -0.5)
    return s if bias is None else s + bias


def _win_finish(layer, h0, s, v, sink_ref, o_ref):
    rows = s.shape[0] // WIN_STAGE
    sink = jnp.concatenate(
        [jnp.full((rows, 1), sink_ref[layer * WIN_HEADS + h0 + hh], jnp.float32)
         for hh in range(WIN_STAGE)], axis=0)
    m = jnp.maximum(jnp.max(s, axis=-1, keepdims=True), sink)
    r = _dot(_bf(jnp.exp(s - m)), _with_ones(v))
    o = r[:, :HEAD] / (r[:, HEAD:HEAD + 1] + jnp.exp(sink - m))
    for hh in range(WIN_STAGE):
        head = h0 + hh
        o_ref[:, head * HEAD:(head + 1) * HEAD] = _bf(o[hh * rows:(hh + 1) * rows])


def _win_band(n, kseq_ref, vseq_ref):
    start = pl.multiple_of(jnp.clip((n - 1) * QBLK, 0, LAT_LEN - 3 * QBLK), QBLK)
    kb = kseq_ref[pl.ds(start, 3 * QBLK), :]
    vb = vseq_ref[pl.ds(start, 3 * QBLK), :]
    shape = (QBLK, PAST + 3 * QBLK)
    col = lax.broadcasted_iota(jnp.int32, shape, 1)
    qpos = n * QBLK + lax.broadcasted_iota(jnp.int32, shape, 0)
    kpos = start + col - PAST
    bias = jnp.where((col < PAST) | (jnp.abs(qpos - kpos) <= WINDOW), 0.0, NEG)
    return kb, vb, jnp.concatenate([bias] * WIN_STAGE, axis=0)


def _win_keys(cache_ref, band, h0):
    g = h0 // WIN_GROUP
    cached = cache_ref[pl.ds(g, PAST, stride=WIN_KV), :]
    return jnp.concatenate([_bf(cached), band[:, g * HEAD:(g + 1) * HEAD]], axis=0)


def _lam_init(layer):
    return 0.8 - 0.6 * math.exp(-0.3 * layer)


def _diff_lambda(layer, lam_ref):
    lp = lam_ref[...]
    return (jnp.exp(jnp.sum(lp[0:1] * lp[1:2], axis=-1, keepdims=True))
            - jnp.exp(jnp.sum(lp[2:3] * lp[3:4], axis=-1, keepdims=True)) + _lam_init(layer))


def _diff_scores(q, k):
    first = lax.broadcasted_iota(jnp.int32, q.shape, 1) < DIFF_QK
    zero = jnp.zeros_like(q)
    return _dot_nt(jnp.concatenate([jnp.where(first, q, zero), jnp.where(first, zero, q)], axis=0), k)


def _diff_finish(layer, s, v_ones, lam, g):
    tq = s.shape[0] // 2
    e = _bf(jnp.exp2(s - jnp.max(s, axis=-1, keepdims=True)))
    r = _dot(e, v_ones)
    o = (r[:tq, :HEAD] * (1.0 / r[:tq, HEAD:HEAD + 1])
         - r[tq:, :HEAD] * (lam / r[tq:, HEAD:HEAD + 1]))
    o = o * lax.rsqrt(jnp.mean(o * o, axis=-1, keepdims=True) + EPS) * g
    return _bf(o * (1.0 - _lam_init(layer)))


def _ctx_attention(layer, qkv_ref, sink_ref, lam_ref, g_ref, yb_ref, yc_ref):
    lam = _diff_lambda(layer, lam_ref)
    stages = [("win", h0) for h0 in WIN_STAGES] + [("diff", h) for h in range(DIFF_HEADS)]

    def scores(stage):
        kind, idx = stage
        if kind == "win":
            return _win_scores(qkv_ref, A_WQ, idx, _slab(qkv_ref, A_WK, idx // WIN_GROUP), None)
        return _diff_scores(_slab(qkv_ref, A_DQ, idx), _slab(qkv_ref, A_DK, idx))

    def finish(stage, s):
        kind, idx = stage
        if kind == "win":
            _win_finish(layer, idx, s, _slab(qkv_ref, A_WV, idx // WIN_GROUP), sink_ref, yb_ref)
        else:
            yc_ref[:, idx * HEAD:(idx + 1) * HEAD] = _diff_finish(
                layer, s, _with_ones(_slab(qkv_ref, A_DV, idx)), lam, g_ref[...])

    _pipelined(stages, scores, finish)


LAT_DIFF_SUB = 256
LAT_WIN_BLOCKS = LAT_LEN // QBLK // DIFF_HEADS


def _lat_attn_kernel(layer, plan, dq_ref, dkc_ref, dvc_ref, dk_ref, dv_ref, lam_ref, g_ref,
                     wq_ref, wkc_ref, wvc_ref, wk_ref, wv_ref, sink_ref, *refs):
    n_in = _n_cast_inputs(plan)
    yc_ref, yb_ref = refs[n_in:n_in + 2]
    kall_ref, vall_ref = refs[-2:]
    _run_casts(plan, refs[:n_in], refs[n_in + 2:-2])
    h = pl.program_id(1)

    head_rows = pl.ds(h, PAST, stride=DIFF_HEADS)
    kall_ref[0:PAST] = _bf(dkc_ref[head_rows, :])
    kall_ref[PAST:] = dk_ref[...]
    vall_ref[0:PAST] = _with_ones(_bf(dvc_ref[head_rows, :]))
    vall_ref[PAST:] = _with_ones(dv_ref[...])
    lam = _diff_lambda(layer, lam_ref)
    bands = [_win_band(h * LAT_WIN_BLOCKS + blk, wk_ref, wv_ref) for blk in range(LAT_WIN_BLOCKS)]

    diff_stages = [("diff", r0, None) for r0 in range(0, LAT_LEN, LAT_DIFF_SUB)]
    win_stages = [("win", blk, h0) for blk in range(LAT_WIN_BLOCKS) for h0 in WIN_STAGES]
    every = len(diff_stages) // len(win_stages)
    stages = []
    for k, stage in enumerate(diff_stages):
        stages.append(stage)
        if k % every == 0 and k // every < len(win_stages):
            stages.append(win_stages[k // every])

    def rows(ref, blk):
        return ref.at[pl.ds(blk * QBLK, QBLK)]

    def scores(stage):
        kind, a, h0 = stage
        if kind == "diff":
            return _diff_scores(dq_ref[a:a + LAT_DIFF_SUB], kall_ref[...])
        kb, _, bias = bands[a]
        return _win_scores(rows(wq_ref, a), 0, h0, _win_keys(wkc_ref, kb, h0), bias)

    def finish(stage, s):
        kind, a, h0 = stage
        if kind == "diff":
            yc_ref[a:a + LAT_DIFF_SUB] = _diff_finish(layer, s, vall_ref[...], lam, g_ref[...])
        else:
            _win_finish(layer, h0, s, _win_keys(wvc_ref, bands[a][1], h0), sink_ref, rows(yb_ref, a))

    _pipelined(stages, scores, finish)


def _lat_attn(qkv, win_ck, win_cv, diff_ck, diff_cv, sink, lam_params, subln, layer, jobs):
    kvw = WIN_KV * HEAD
    wrows = LAT_WIN_BLOCKS * QBLK
    step = lambda b, h: b * DIFF_HEADS + h

    def cache_spec(heads):
        return pl.BlockSpec((None, None, PAST * heads, HEAD), lambda b, h: (b, layer, 0, 0))

    def head_spec(col0):
        return pl.BlockSpec((LAT_LEN, HEAD), lambda b, h: (b, col0 // HEAD + h))

    def seq_spec(col0):
        return pl.BlockSpec((LAT_LEN, kvw), lambda b, h: (b, col0 // kvw))

    c_in, c_args, c_out, c_shapes = _cast_plumbing(jobs, N_LAT_SEQ * DIFF_HEADS, step)
    out = pl.pallas_call(
        functools.partial(_lat_attn_kernel, layer, _cast_plan(jobs)),
        grid=(N_LAT_SEQ, DIFF_HEADS),
        in_specs=[head_spec(A_DQ), cache_spec(DIFF_HEADS), cache_spec(DIFF_HEADS),
                  head_spec(A_DK), head_spec(A_DV),
                  pl.BlockSpec((None, 4, DIFF_QK), lambda b, h: (layer, 0, 0)),
                  _vec_spec(layer, HEAD),
                  pl.BlockSpec((wrows, WIN_HEADS * HEAD), lambda b, h: (step(b, h), 0)),
                  cache_spec(WIN_KV), cache_spec(WIN_KV), seq_spec(A_WK), seq_spec(A_WV),
                  pl.BlockSpec(memory_space=pltpu.SMEM)] + c_in,
        out_specs=[pl.BlockSpec((LAT_LEN, HEAD), lambda b, h: (b, h)),
                   pl.BlockSpec((wrows, WIN_HEADS * HEAD), lambda b, h: (step(b, h), 0))] + c_out,
        out_shape=[jax.ShapeDtypeStruct((ROWS_LAT, DIFF_HEADS * HEAD), jnp.bfloat16),
                   jax.ShapeDtypeStruct((ROWS_LAT, WIN_HEADS * HEAD), jnp.bfloat16)] + c_shapes,
        scratch_shapes=[pltpu.VMEM((PAST + LAT_LEN, HEAD), jnp.bfloat16),
                        pltpu.VMEM((PAST + LAT_LEN, 2 * HEAD), jnp.bfloat16)],
        compiler_params=_params(("arbitrary", "arbitrary")),
        name="lat_attn",
    )(qkv, diff_ck, diff_cv, qkv, qkv, lam_params, subln, qkv, win_ck, win_cv, qkv, qkv, sink, *c_args)
    return out[1], out[0], out[2:]


MERGE_TM = 1024
MERGE_TN = 512


def _merge_kernel(h_ref, ya_ref, ybc_ref, ybl_ref, ycc_ref, ycl_ref,
                  wga_ref, wgb_ref, wgc_ref, wa_ref, wb_ref, wc_ref, o_ref):
    h = h_ref[...]
    yb = _pick_rows(MERGE_TM, ybc_ref, ybl_ref)
    yc = _pick_rows(MERGE_TM, ycc_ref, ycl_ref)
    m = _sigmoid(_dot(h, wga_ref[...])) * _dot(ya_ref[...], wa_ref[...])
    m += _sigmoid(_dot(h, wgb_ref[...])) * _dot(yb, wb_ref[...])
    m += _sigmoid(_dot(h, wgc_ref[...])) * _dot(yc, wc_ref[...])
    o_ref[...] = _bf(m)


def _merge(h, ya, yb_ctx, yb_lat, yc_ctx, yc_lat, w_gate, wa, wb, wc):
    tm, tn = MERGE_TM, MERGE_TN
    nj = D // tn

    def gate_spec(branch):
        return pl.BlockSpec((D, tn), lambda i, j: (0, branch * nj + j))

    w_spec = pl.BlockSpec((CONV_DIM, tn), lambda i, j: (0, j))
    return pl.pallas_call(
        _merge_kernel,
        grid=(ROWS // tm, nj),
        in_specs=[pl.BlockSpec((tm, D), lambda i, j: (i, 0)),
                  pl.BlockSpec((tm, CONV_DIM), lambda i, j: (i, 0)),
                  *_split_specs(tm, WIN_HEADS * HEAD), *_split_specs(tm, DIFF_HEADS * HEAD),
                  gate_spec(0), gate_spec(1), gate_spec(2), w_spec, w_spec, w_spec],
        out_specs=pl.BlockSpec((tm, tn), lambda i, j: (i, j)),
        out_shape=jax.ShapeDtypeStruct((ROWS, D), jnp.bfloat16),
        compiler_params=_params(("parallel", "arbitrary")),
        name="merge",
    )(h, ya, yb_ctx, yb_lat, yc_ctx, yc_lat, w_gate, w_gate, w_gate, wa, wb, wc)


OUT_TM = 512


def _out_proj_kernel(n_x, *refs):
    x_refs = refs[:n_x]
    m_ref, w_ref, gate_ref, g_ref, shift_ref, scale_ref, xo_ref, h_ref = refs[n_x:]
    x = x_refs[0][...] if n_x == 1 else _pick_rows(OUT_TM, *x_refs)
    x = x + gate_ref[...] * _dot(m_ref[...], w_ref[...])
    xo_ref[...] = x
    h_ref[...] = _bf(_rms_mod(x, g_ref[...], scale_ref[...], shift_ref[...]))


def _out_proj(xs, merged, w_out, mods, norm_mlp, layer):
    tm = OUT_TM
    row = pl.BlockSpec((tm, D), lambda i: (i, 0))
    x_specs = [row] if len(xs) == 1 else list(_split_specs(tm, D))
    return pl.pallas_call(
        functools.partial(_out_proj_kernel, len(xs)),
        grid=(ROWS // tm,),
        in_specs=x_specs + [row,
                            pl.BlockSpec((D, D), lambda i: (0, 0)),
                            _mod_spec(layer, 2, tm), _vec_spec(layer, D),
                            _mod_spec(layer, 3, tm), _mod_spec(layer, 4, tm)],
        out_specs=[row, row],
        out_shape=[jax.ShapeDtypeStruct((ROWS, D), jnp.float32),
                   jax.ShapeDtypeStruct((ROWS, D), jnp.bfloat16)],
        compiler_params=_params(("arbitrary",)),
        name="out_proj",
    )(*xs, merged, w_out, mods, norm_mlp.reshape(N_LAYERS, 1, D), mods, mods)


MLP_TM = 512
MLP_TF = 1024


def _mlp_kernel(final, h_ref, w1_ref, w2_ref, x_ref, gate_ref, g_ref, shift_ref, scale_ref,
                *rest):
    if final:
        yc_ref, yl_ref, acc_ref = rest
    else:
        xo_ref, hn_ref, acc_ref = rest
    i = pl.program_id(0)
    j = pl.program_id(1)

    @pl.when(j == 0)
    def _():
        acc_ref[...] = jnp.zeros_like(acc_ref)

    t = jnp.maximum(_dot(h_ref[...], w1_ref[...]), 0.0)
    acc_ref[...] += _dot(_bf(t * t), w2_ref[...])

    @pl.when(j == pl.num_programs(1) - 1)
    def _():
        x = x_ref[...] + gate_ref[...] * acc_ref[...]
        if final:
            y = x * lax.rsqrt(jnp.mean(x * x, axis=-1, keepdims=True) + EPS) * g_ref[...]

            @pl.when(i < ROWS_CTX // MLP_TM)
            def _():
                yc_ref[...] = y

            @pl.when(i >= ROWS_CTX // MLP_TM)
            def _():
                yl_ref[...] = y
        else:
            xo_ref[...] = x
            hn_ref[...] = _bf(_rms_mod(x, g_ref[...], scale_ref[...], shift_ref[...]))


def _mlp(h2, x, w1, w2, mods, norm_next, layer, final):
    tm, tf = MLP_TM, MLP_TF
    row = pl.BlockSpec((tm, D), lambda i, j: (i, 0))
    nxt = layer if final else layer + 1
    g_spec = pl.BlockSpec((None, 1, D), lambda i, j: (0 if final else nxt, 0, 0))
    if final:
        out_specs = list(_split_specs(tm, D))
        out_shape = [jax.ShapeDtypeStruct((ROWS_CTX, D), jnp.float32),
                     jax.ShapeDtypeStruct((ROWS_LAT, D), jnp.float32)]
    else:
        out_specs = [row, row]
        out_shape = [jax.ShapeDtypeStruct((ROWS, D), jnp.float32),
                     jax.ShapeDtypeStruct((ROWS, D), jnp.bfloat16)]
    return pl.pallas_call(
        functools.partial(_mlp_kernel, final),
        grid=(ROWS // tm, D_FF // tf),
        in_specs=[row,
                  pl.BlockSpec((D, tf), lambda i, j: (0, j)),
                  pl.BlockSpec((tf, D), lambda i, j: (j, 0)),
                  row, _mod_spec(layer, 5, tm), g_spec,
                  _mod_spec(nxt, 0, tm), _mod_spec(nxt, 1, tm)],
        out_specs=out_specs,
        out_shape=out_shape,
        scratch_shapes=[pltpu.VMEM((tm, D), jnp.float32)],
        compiler_params=_params(("arbitrary", "arbitrary")),
        name="mlp_final" if final else "mlp",
    )(h2, w1, w2, x, mods, norm_next, mods, mods)


def kernel(x_prompt, x_sample, cache_win_k, cache_win_v, cache_diff_k, cache_diff_v, c, c_ctx, w_ada, b_ada, norm_mix, norm_mlp, w_in, conv_w, win_sink, lambda_q1, lambda_k1, lambda_q2, lambda_k2, diff_subln, w_branch_conv, w_branch_win, w_branch_diff, w_out, w_mlp1, w_mlp2, norm_final):
    xs = (x_prompt.reshape(ROWS_CTX, D), x_sample.reshape(ROWS_LAT, D))
    cvec = jnp.concatenate([c_ctx[None], c, jnp.zeros((N_MOD_ROWS - 1 - N_LAT_SEQ, D), jnp.float32)], axis=0)
    mods = _modulation(cvec, w_ada, b_ada).reshape(N_LAYERS * N_MOD_ROWS * N_MOD, 1, D)

    tables = _rope_tables()
    sink = win_sink.reshape(N_LAYERS * WIN_HEADS)
    lam_params = jnp.stack([lambda_q1, lambda_k1, lambda_q2, lambda_k2], axis=1)
    subln = diff_subln.reshape(N_LAYERS, 1, HEAD)
    ck_win = cache_win_k.reshape(N_LAT_SEQ, N_LAYERS, PAST * WIN_KV, HEAD)
    cv_win = cache_win_v.reshape(N_LAT_SEQ, N_LAYERS, PAST * WIN_KV, HEAD)
    ck_diff = cache_diff_k.reshape(N_LAT_SEQ, N_LAYERS, PAST * DIFF_HEADS, HEAD)
    cv_diff = cache_diff_v.reshape(N_LAT_SEQ, N_LAYERS, PAST * DIFF_HEADS, HEAD)
    norm_final_3d = norm_final.reshape(1, 1, D)
    norm_mix_3d = norm_mix.reshape(N_LAYERS, 1, D)

    h = _prologue(*xs, mods, norm_mix, 0)
    states = None
    for layer in range(N_LAYERS):
        in_job = functools.partial(_CastJob, w_in, layer, cbw=IN_CBW)
        ya, (w_attn, wa, wb, wc, w_o) = _conv_branch(
            h, w_in, conv_w, layer,
            [in_job(k0=OFF_ATTN // IN_CBW, nb=ATTN_COLS // IN_CBW)]
            + [_whole(w, layer) for w in (w_branch_conv, w_branch_win, w_branch_diff, w_out)])
        qkv, states, yb_ctx, yc_ctx, (w_gate,) = _attn_proj(
            h, w_attn, tables, sink, lam_params, subln, layer, states,
            [in_job(k0=OFF_GATE // IN_CBW, nb=3 * D // IN_CBW)])
        yb_lat, yc_lat, (w1, w2) = _lat_attn(qkv, ck_win, cv_win, ck_diff, cv_diff, sink, lam_params, subln,
                                             layer, [_whole(w_mlp1, layer), _whole(w_mlp2, layer)])
        merged = _merge(h, ya, yb_ctx, yb_lat, yc_ctx, yc_lat, w_gate, wa, wb, wc)
        x, h2 = _out_proj(xs, merged, w_o, mods, norm_mlp, layer)
        if layer == N_LAYERS - 1:
            y_ctx, y_lat = _mlp(h2, x, w1, w2, mods, norm_final_3d, layer, True)
        else:
            x, h = _mlp(h2, x, w1, w2, mods, norm_mix_3d, layer, False)
            xs = (x,)

    new_win_k, new_win_v, new_diff_k, new_diff_v = states
    return (y_ctx.reshape(N_CTX_SEQ, CTX_LEN, D), y_lat.reshape(N_LAT_SEQ, LAT_LEN, D),
            new_win_k.reshape(N_CTX_SEQ, N_LAYERS, CTX_LEN, WIN_KV, HEAD),
            new_win_v.reshape(N_CTX_SEQ, N_LAYERS, CTX_LEN, WIN_KV, HEAD),
            new_diff_k.reshape(N_CTX_SEQ, N_LAYERS, CTX_LEN, DIFF_HEADS, HEAD),
            new_diff_v.reshape(N_CTX_SEQ, N_LAYERS, CTX_LEN, DIFF_HEADS, HEAD))
```

```python
import functools
import math
from typing import NamedTuple

import jax
import jax.numpy as jnp
import numpy as np
from jax import lax
from jax.experimental import pallas as pl
from jax.experimental.pallas import tpu as pltpu

D = 2048
N_CTX_SEQ = 16
CTX_LEN = 256
N_LAT_SEQ = 2
LAT_LEN = 2048
N_LAYERS = 2
PAST = 512
GRID_W = 64
HEAD = 128
CONV_DIM = 1024
WIN_HEADS = 8
WIN_KV = 2
WIN_GROUP = WIN_HEADS // WIN_KV
WINDOW = 128
QBLK = 128
DIFF_HEADS = 8
DIFF_QK = 64
D_FF = 4 * D
EPS = 1e-6
ROPE_BASE = 10000.0
NEG = -1e30

ROWS_CTX = N_CTX_SEQ * CTX_LEN
ROWS_LAT = N_LAT_SEQ * LAT_LEN
ROWS = ROWS_CTX + ROWS_LAT
N_MOD_ROWS = 8
N_MOD = 6

OFF_CONV = 0
OFF_ATTN = 3 * CONV_DIM
ATTN_COLS = (WIN_HEADS + 2 * WIN_KV) * HEAD + 3 * DIFF_HEADS * HEAD
OFF_GATE = OFF_ATTN + ATTN_COLS
IN_TOTAL = OFF_GATE + 3 * D
IN_CBW = 1536
A_WQ, A_WK, A_WV = 0, WIN_HEADS * HEAD, (WIN_HEADS + WIN_KV) * HEAD
A_DQ = (WIN_HEADS + 2 * WIN_KV) * HEAD
A_DK = A_DQ + DIFF_HEADS * HEAD
A_DV = A_DK + DIFF_HEADS * HEAD

VMEM_LIMIT_V7X = 60 * 1024 * 1024


def _params(sem, vmem=VMEM_LIMIT_V7X):
    return pltpu.CompilerParams(dimension_semantics=sem, vmem_limit_bytes=vmem)


def _bf(x):
    return x.astype(jnp.bfloat16)


def _dot(a, b):
    return jnp.dot(a, b, preferred_element_type=jnp.float32)


def _dot_nt(a, b):
    return lax.dot_general(a, b, (((1,), (1,)), ((), ())), preferred_element_type=jnp.float32)


def _sigmoid(x):
    return 1.0 / (1.0 + jnp.exp(-x))


def _mod_group(i, tm):
    n_ctx = ROWS_CTX // tm
    return jnp.where(i < n_ctx, 0, 1 + (i - n_ctx) // (LAT_LEN // tm))


def _mod_spec(layer, which, tm, lag=0):
    def imap(i, *_):
        group = _mod_group(jnp.maximum(i - lag, 0), tm)
        return (layer * N_MOD_ROWS * N_MOD + group * N_MOD + which, 0, 0)
    return pl.BlockSpec((None, 1, D), imap)


def _vec_spec(layer, width):
    return pl.BlockSpec((None, 1, width), lambda *_: (layer, 0, 0))


def _rms_mod(x, g, scale, shift):
    y = x * lax.rsqrt(jnp.mean(x * x, axis=-1, keepdims=True) + EPS) * g
    return y * (1.0 + scale) + shift


def _mod_kernel(c_ref, w_ref, b_ref, o_ref):
    c = c_ref[...]
    s = c * _sigmoid(c)
    o_ref[...] = _dot(_bf(s), _bf(w_ref[...])) + b_ref[...]


def _modulation(cvec, w_ada, b_ada):
    tn = 1024
    n = N_MOD * D
    return pl.pallas_call(
        _mod_kernel,
        grid=(N_LAYERS, n // tn),
        in_specs=[pl.BlockSpec((N_MOD_ROWS, D), lambda l, j: (0, 0)),
                  pl.BlockSpec((None, D, tn), lambda l, j: (l, 0, j)),
                  pl.BlockSpec((None, 1, tn), lambda l, j: (l, 0, j))],
        out_specs=pl.BlockSpec((None, N_MOD_ROWS, tn), lambda l, j: (l, 0, j)),
        out_shape=jax.ShapeDtypeStruct((N_LAYERS, N_MOD_ROWS, n), jnp.float32),
        compiler_params=_params(("parallel", "parallel")),
        name="modulation",
    )(cvec, w_ada, b_ada.reshape(N_LAYERS, 1, n))


def _split_specs(tm, width):
    n_ctx = ROWS_CTX // tm
    return (pl.BlockSpec((tm, width), lambda i, *_: (jnp.minimum(i, n_ctx - 1), 0)),
            pl.BlockSpec((tm, width), lambda i, *_: (jnp.maximum(i - n_ctx, 0), 0)))


def _pick_rows(tm, ctx_ref, lat_ref):
    return jnp.where(pl.program_id(0) < ROWS_CTX // tm, ctx_ref[...], lat_ref[...])


PRO_TM = 1024


def _prologue_kernel(xc_ref, xl_ref, g_ref, shift_ref, scale_ref, h_ref):
    x = _pick_rows(PRO_TM, xc_ref, xl_ref)
    h_ref[...] = _bf(_rms_mod(x, g_ref[...], scale_ref[...], shift_ref[...]))


def _prologue(x_ctx, x_lat, mods, norm_mix, layer):
    tm = PRO_TM
    return pl.pallas_call(
        _prologue_kernel,
        grid=(ROWS // tm,),
        in_specs=[*_split_specs(tm, D),
                  _vec_spec(layer, D),
                  _mod_spec(layer, 0, tm),
                  _mod_spec(layer, 1, tm)],
        out_specs=pl.BlockSpec((tm, D), lambda i: (i, 0)),
        out_shape=jax.ShapeDtypeStruct((ROWS, D), jnp.bfloat16),
        compiler_params=_params(("arbitrary",)),
        name="prologue",
    )(x_ctx, x_lat, norm_mix.reshape(N_LAYERS, 1, D), mods, mods)


class _CastJob(NamedTuple):
    src: jax.Array
    layer: int
    k0: int
    nb: int
    cbw: int

    @property
    def rows(self):
        return self.src.shape[1]


def _whole(src, layer):
    return _CastJob(src, layer, 0, 1, src.shape[2])


def _cast_plumbing(jobs, n_steps, step_of):
    in_specs, args, out_specs, out_shapes = [], [], [], []
    for job in jobs:
        rows = job.rows // n_steps
        assert rows * n_steps == job.rows and rows % 16 == 0
        for k in range(job.nb):
            in_specs.append(pl.BlockSpec(
                (None, rows, job.cbw), lambda *g, job=job, k=k: (job.layer, step_of(*g), job.k0 + k)))
            args.append(job.src)
        out_specs.append(pl.BlockSpec((rows, job.nb * job.cbw), lambda *g: (step_of(*g), 0)))
        out_shapes.append(jax.ShapeDtypeStruct((job.rows, job.nb * job.cbw), jnp.bfloat16))
    return in_specs, args, out_specs, out_shapes


def _cast_plan(jobs):
    return tuple((job.nb, job.cbw) for job in jobs)


def _run_casts(plan, in_refs, out_refs):
    in_refs = iter(in_refs)
    for (nb, cbw), out_ref in zip(plan, out_refs):
        for k in range(nb):
            out_ref[:, k * cbw:(k + 1) * cbw] = _bf(next(in_refs)[...])


def _n_cast_inputs(plan):
    return sum(nb for nb, _ in plan)


CONV_TM = LAT_LEN
CONV_TN = 256


def _conv_kernel(plan, h_ref, wb_ref, wc_ref, wx_ref, cw_ref, *refs):
    n_in = _n_cast_inputs(plan)
    o_ref = refs[n_in]
    _run_casts(plan, refs[:n_in], refs[n_in + 1:])
    i = pl.program_id(0)
    h = h_ref[...]
    cb = _dot(h, _bf(wb_ref[...]))
    z = _dot(h, _bf(wc_ref[...])) * _dot(h, _bf(wx_ref[...]))
    seq_mask = jnp.where(i < ROWS_CTX // CONV_TM, CTX_LEN - 1, LAT_LEN - 1)
    pos = lax.broadcasted_iota(jnp.int32, z.shape, 0) & seq_mask
    z_prev = jnp.where(pos == 0, 0.0, pltpu.roll(z, 1, 0))
    z_next = jnp.where(pos == seq_mask, 0.0, pltpu.roll(z, CONV_TM - 1, 0))
    cw = cw_ref[...]
    conv = z_prev * cw[0:1, :] + z * cw[1:2, :] + z_next * cw[2:3, :]
    o_ref[...] = _bf(cb * conv)


def _conv_branch(h, w_in, conv_w, layer, jobs):
    nj = CONV_DIM // CONV_TN
    ni = ROWS // CONV_TM

    def wspec(group):
        return pl.BlockSpec((None, D, CONV_TN), lambda i, j: (layer, 0, group * nj + j))

    c_in, c_args, c_out, c_shapes = _cast_plumbing(jobs, ni * nj, lambda i, j: i * nj + j)
    out = pl.pallas_call(
        functools.partial(_conv_kernel, _cast_plan(jobs)),
        grid=(ni, nj),
        in_specs=[pl.BlockSpec((CONV_TM, D), lambda i, j: (i, 0)),
                  wspec(0), wspec(1), wspec(2),
                  pl.BlockSpec((None, 3, CONV_TN), lambda i, j: (layer, 0, j))] + c_in,
        out_specs=[pl.BlockSpec((CONV_TM, CONV_TN), lambda i, j: (i, j))] + c_out,
        out_shape=[jax.ShapeDtypeStruct((ROWS, CONV_DIM), jnp.bfloat16)] + c_shapes,
        compiler_params=_params(("arbitrary", "arbitrary")),
        name="conv_branch",
    )(h, w_in, w_in, w_in, conv_w, *c_args)
    return out[0], out[1:]


ATTN_TM = CTX_LEN
ATTN_CHUNK = 512
ROPE_WIN, ROPE_DIFF, ROPE_NONE = 0, 1, 2
_SLAB_KIND = ([ROPE_WIN] * (WIN_HEADS + WIN_KV) + [ROPE_NONE] * WIN_KV
              + [ROPE_DIFF] * (2 * DIFF_HEADS) + [ROPE_NONE] * DIFF_HEADS)
_ROPE_SHIFT = {ROPE_WIN: HEAD // 4, ROPE_DIFF: DIFF_QK // 4}
_STATE_COLS = ((A_WK, WIN_KV * HEAD), (A_WV, WIN_KV * HEAD),
               (A_DK, DIFF_HEADS * HEAD), (A_DV, DIFF_HEADS * HEAD))
_STATE_ROWS_BY_HEAD = (True, True, False, False)


def _state_block(idx):
    width = _STATE_COLS[idx][1]
    return (CTX_LEN * width // HEAD, HEAD) if _STATE_ROWS_BY_HEAD[idx] else (CTX_LEN, width)
DIFF_Q_SCALE = DIFF_QK ** -0.5 * math.log2(math.e)


def _rope_tables():
    t = np.arange(LAT_LEN, dtype=np.int32)
    rows = (t // GRID_W).astype(np.float32)[:, None]
    cols = (t % GRID_W).astype(np.float32)[:, None]
    lane = np.arange(HEAD, dtype=np.int32)[None, :]
    out = []
    for kind in (ROPE_WIN, ROPE_DIFF):
        s = _ROPE_SHIFT[kind]
        within = lane % (4 * s)
        freq = (within % s).astype(np.float32)
        inv = np.float32(ROPE_BASE) ** (-freq / np.float32(s))
        ang = (np.where(within < 2 * s, rows, cols) * inv).astype(np.float32)
        first = (within % (2 * s)) < s
        cos, sin = np.cos(ang), np.sin(ang)
        out += [cos, np.where(first, -sin, 0.0), np.where(first, 0.0, sin)]
    return jnp.asarray(np.stack(out).astype(np.float32))


def _attn_proj_kernel(first, layer, plan, h_ref, w_ref, tab_ref, sink_ref, lam_ref, g_ref, *refs):
    n_in = _n_cast_inputs(plan)
    n_st = len(_STATE_COLS)
    pos = (0 if first else n_st) + n_in
    cast_in = refs[pos - n_in:pos]
    qkv_ref = refs[pos]
    state_refs = refs[pos + 1:pos + 1 + n_st]
    yb_ref, yc_ref = refs[pos + 1 + n_st:pos + 3 + n_st]
    cast_out = refs[pos + 3 + n_st:-2]
    tiles = refs[-2:]
    i = pl.program_id(0)

    def project(ctx, dst_ref):
        h = h_ref[...]
        if ctx and first:
            for ref in state_refs:
                ref[1:] = jnp.zeros((N_LAYERS - 1,) + ref.shape[1:], jnp.float32)
        for c0 in range(0, ATTN_COLS, ATTN_CHUNK):
            acc = _dot(h, w_ref[:, c0:c0 + ATTN_CHUNK])
            for s0 in range(0, ATTN_CHUNK, HEAD):
                col = c0 + s0
                x = acc[:, s0:s0 + HEAD]
                kind = _SLAB_KIND[col // HEAD]
                if ctx:
                    for ref, (src, width), by_head in zip(state_refs, _STATE_COLS, _STATE_ROWS_BY_HEAD):
                        if src <= col < src + width:
                            dst = col - src
                            if by_head:
                                where = (pl.ds(dst // HEAD, CTX_LEN, stride=width // HEAD), slice(None))
                            else:
                                where = (slice(None), slice(dst, dst + HEAD))
                            if first:
                                ref[(0,) + where] = x
                            else:
                                ref[where] = x
                elif kind != ROPE_NONE:
                    s = _ROPE_SHIFT[kind]
                    x = (x * tab_ref[3 * kind] + pltpu.roll(x, HEAD - s, 1) * tab_ref[3 * kind + 1]
                         + pltpu.roll(x, s, 1) * tab_ref[3 * kind + 2])
                if A_DQ <= col < A_DK:
                    x = x * DIFF_Q_SCALE
                dst_ref[:, col:col + HEAD] = _bf(x)

    def step(ctx, parity, attend):
        def run():
            _run_casts(plan, cast_in, cast_out)
            if attend:
                _ctx_attention(layer, tiles[1 - parity], sink_ref, lam_ref, g_ref, yb_ref, yc_ref)
            project(ctx, tiles[parity] if ctx else qkv_ref)
        return run

    @pl.when(i == 0)
    def _():
        tiles[1][...] = jnp.zeros_like(tiles[1])

    is_ctx = i < N_CTX_SEQ
    for parity in (0, 1):
        pl.when(is_ctx & (i % 2 == parity))(step(True, parity, True))
    pl.when(i == N_CTX_SEQ)(step(False, N_CTX_SEQ % 2, True))
    pl.when(i > N_CTX_SEQ)(step(False, 0, False))


def _attn_proj(h, w_attn, tables, sink, lam_params, subln, layer, prev_states, jobs):
    tm = ATTN_TM
    per_lat = LAT_LEN // tm
    first = prev_states is None
    last_ctx = N_CTX_SEQ - 1

    def state_spec(idx):
        if first:
            return pl.BlockSpec((None, N_LAYERS) + _state_block(idx), lambda i: (jnp.minimum(i, last_ctx), 0, 0, 0))
        return pl.BlockSpec((None, None) + _state_block(idx), lambda i: (jnp.minimum(i, last_ctx), layer, 0, 0))

    in_specs = [pl.BlockSpec((tm, D), lambda i: (i, 0)),
                pl.BlockSpec((D, ATTN_COLS), lambda i: (0, 0), pipeline_mode=pl.Buffered(1)),
                pl.BlockSpec((6, tm, HEAD), lambda i: (0, jnp.maximum(i - N_CTX_SEQ, 0) % per_lat, 0)),
                pl.BlockSpec(memory_space=pltpu.SMEM),
                pl.BlockSpec((None, 4, DIFF_QK), lambda i: (layer, 0, 0)),
                _vec_spec(layer, HEAD)]
    args = [h, w_attn, tables, sink, lam_params, subln]
    aliases = {}
    if not first:
        for k, st in enumerate(prev_states):
            in_specs.append(pl.BlockSpec(memory_space=pl.ANY))
            aliases[len(args)] = 1 + k
            args.append(st)
    c_in, c_args, c_out, c_shapes = _cast_plumbing(jobs, ROWS // tm, lambda i: i)
    y_spec = pl.BlockSpec((CTX_LEN, WIN_HEADS * HEAD), lambda i: (jnp.clip(i - 1, 0, last_ctx), 0))
    y_shape = jax.ShapeDtypeStruct((ROWS_CTX, WIN_HEADS * HEAD), jnp.bfloat16)
    out = pl.pallas_call(
        functools.partial(_attn_proj_kernel, first, layer, _cast_plan(jobs)),
        grid=(ROWS // tm,),
        in_specs=in_specs + c_in,
        out_specs=[pl.BlockSpec((tm, ATTN_COLS), lambda i: (jnp.maximum(i - N_CTX_SEQ, 0), 0))]
                  + [state_spec(idx) for idx in range(len(_STATE_COLS))] + [y_spec, y_spec] + c_out,
        out_shape=[jax.ShapeDtypeStruct((ROWS_LAT, ATTN_COLS), jnp.bfloat16)]
                  + [jax.ShapeDtypeStruct((N_CTX_SEQ, N_LAYERS) + _state_block(idx), jnp.float32)
                     for idx in range(len(_STATE_COLS))] + [y_shape, y_shape] + c_shapes,
        scratch_shapes=[pltpu.VMEM((CTX_LEN, ATTN_COLS), jnp.bfloat16)] * 2,
        input_output_aliases=aliases,
        compiler_params=_params(("arbitrary",)),
        name="attn_proj",
    )(*args, *c_args)
    return out[0], out[1:5], out[5], out[6], out[7:]


def _slab(ref, col0, idx):
    return ref[:, col0 + idx * HEAD:col0 + (idx + 1) * HEAD]


def _with_ones(v):
    return jnp.concatenate([v, jnp.ones_like(v)], axis=1)


def _pipelined(items, scores, finish):
    s = scores(items[0])
    for r, item in enumerate(items):
        s_next = scores(items[r + 1]) if r + 1 < len(items) else None
        finish(item, s)
        s = s_next


WIN_STAGE = WIN_GROUP
WIN_STAGES = tuple(range(0, WIN_HEADS, WIN_STAGE))


def _win_scores(q_ref, col0, h0, k, bias):
    q = jnp.concatenate([_slab(q_ref, col0, h0 + hh) for hh in range(WIN_STAGE)], axis=0)
    s = _dot_nt(q, k) * (HEAD ** -0.5)
    return s if bias is None else s + bias


def _win_finish(layer, h0, s, v, sink_ref, o_ref):
    rows = s.shape[0] // WIN_STAGE
    sink = jnp.concatenate(
        [jnp.full((rows, 1), sink_ref[layer * WIN_HEADS + h0 + hh], jnp.float32)
         for hh in range(WIN_STAGE)], axis=0)
    m = jnp.maximum(jnp.max(s, axis=-1, keepdims=True), sink)
    r = _dot(_bf(jnp.exp(s - m)), _with_ones(v))
    o = r[:, :HEAD] / (r[:, HEAD:HEAD + 1] + jnp.exp(sink - m))
    for hh in range(WIN_STAGE):
        head = h0 + hh
        o_ref[:, head * HEAD:(head + 1) * HEAD] = _bf(o[hh * rows:(hh + 1) * rows])


def _win_band(n, kseq_ref, vseq_ref):
    start = pl.multiple_of(jnp.clip((n - 1) * QBLK, 0, LAT_LEN - 3 * QBLK), QBLK)
    kb = kseq_ref[pl.ds(start, 3 * QBLK), :]
    vb = vseq_ref[pl.ds(start, 3 * QBLK), :]
    shape = (QBLK, PAST + 3 * QBLK)
    col = lax.broadcasted_iota(jnp.int32, shape, 1)
    qpos = n * QBLK + lax.broadcasted_iota(jnp.int32, shape, 0)
    kpos = start + col - PAST
    bias = jnp.where((col < PAST) | (jnp.abs(qpos - kpos) <= WINDOW), 0.0, NEG)
    return kb, vb, jnp.concatenate([bias] * WIN_STAGE, axis=0)


def _win_keys(cache_ref, band, h0):
    g = h0 // WIN_GROUP
    cached = cache_ref[pl.ds(g, PAST, stride=WIN_KV), :]
    return jnp.concatenate([_bf(cached), band[:, g * HEAD:(g + 1) * HEAD]], axis=0)


def _lam_init(layer):
    return 0.8 - 0.6 * math.exp(-0.3 * layer)


def _diff_lambda(layer, lam_ref):
    lp = lam_ref[...]
    return (jnp.exp(jnp.sum(lp[0:1] * lp[1:2], axis=-1, keepdims=True))
            - jnp.exp(jnp.sum(lp[2:3] * lp[3:4], axis=-1, keepdims=True)) + _lam_init(layer))


def _diff_scores(q, k):
    first = lax.broadcasted_iota(jnp.int32, q.shape, 1) < DIFF_QK
    zero = jnp.zeros_like(q)
    return _dot_nt(jnp.concatenate([jnp.where(first, q, zero), jnp.where(first, zero, q)], axis=0), k)


def _diff_finish(layer, s, v_ones, lam, g):
    tq = s.shape[0] // 2
    e = _bf(jnp.exp2(s - jnp.max(s, axis=-1, keepdims=True)))
    r = _dot(e, v_ones)
    o = (r[:tq, :HEAD] * (1.0 / r[:tq, HEAD:HEAD + 1])
         - r[tq:, :HEAD] * (lam / r[tq:, HEAD:HEAD + 1]))
    o = o * lax.rsqrt(jnp.mean(o * o, axis=-1, keepdims=True) + EPS) * g
    return _bf(o * (1.0 - _lam_init(layer)))


def _ctx_attention(layer, qkv_ref, sink_ref, lam_ref, g_ref, yb_ref, yc_ref):
    lam = _diff_lambda(layer, lam_ref)
    stages = [("win", h0) for h0 in WIN_STAGES] + [("diff", h) for h in range(DIFF_HEADS)]

    def scores(stage):
        kind, idx = stage
        if kind == "win":
            return _win_scores(qkv_ref, A_WQ, idx, _slab(qkv_ref, A_WK, idx // WIN_GROUP), None)
        return _diff_scores(_slab(qkv_ref, A_DQ, idx), _slab(qkv_ref, A_DK, idx))

    def finish(stage, s):
        kind, idx = stage
        if kind == "win":
            _win_finish(layer, idx, s, _slab(qkv_ref, A_WV, idx // WIN_GROUP), sink_ref, yb_ref)
        else:
            yc_ref[:, idx * HEAD:(idx + 1) * HEAD] = _diff_finish(
                layer, s, _with_ones(_slab(qkv_ref, A_DV, idx)), lam, g_ref[...])

    _pipelined(stages, scores, finish)


LAT_DIFF_SUB = 256
LAT_WIN_BLOCKS = LAT_LEN // QBLK // DIFF_HEADS


def _lat_attn_kernel(layer, plan, dq_ref, dkc_ref, dvc_ref, dk_ref, dv_ref, lam_ref, g_ref,
                     wq_ref, wkc_ref, wvc_ref, wk_ref, wv_ref, sink_ref, *refs):
    n_in = _n_cast_inputs(plan)
    yc_ref, yb_ref = refs[n_in:n_in + 2]
    kall_ref, vall_ref = refs[-2:]
    _run_casts(plan, refs[:n_in], refs[n_in + 2:-2])
    h = pl.program_id(1)

    head_rows = pl.ds(h, PAST, stride=DIFF_HEADS)
    kall_ref[0:PAST] = _bf(dkc_ref[head_rows, :])
    kall_ref[PAST:] = dk_ref[...]
    vall_ref[0:PAST] = _with_ones(_bf(dvc_ref[head_rows, :]))
    vall_ref[PAST:] = _with_ones(dv_ref[...])
    lam = _diff_lambda(layer, lam_ref)
    bands = [_win_band(h * LAT_WIN_BLOCKS + blk, wk_ref, wv_ref) for blk in range(LAT_WIN_BLOCKS)]

    diff_stages = [("diff", r0, None) for r0 in range(0, LAT_LEN, LAT_DIFF_SUB)]
    win_stages = [("win", blk, h0) for blk in range(LAT_WIN_BLOCKS) for h0 in WIN_STAGES]
    every = len(diff_stages) // len(win_stages)
    stages = []
    for k, stage in enumerate(diff_stages):
        stages.append(stage)
        if k % every == 0 and k // every < len(win_stages):
            stages.append(win_stages[k // every])

    def rows(ref, blk):
        return ref.at[pl.ds(blk * QBLK, QBLK)]

    def scores(stage):
        kind, a, h0 = stage
        if kind == "diff":
            return _diff_scores(dq_ref[a:a + LAT_DIFF_SUB], kall_ref[...])
        kb, _, bias = bands[a]
        return _win_scores(rows(wq_ref, a), 0, h0, _win_keys(wkc_ref, kb, h0), bias)

    def finish(stage, s):
        kind, a, h0 = stage
        if kind == "diff":
            yc_ref[a:a + LAT_DIFF_SUB] = _diff_finish(layer, s, vall_ref[...], lam, g_ref[...])
        else:
            _win_finish(layer, h0, s, _win_keys(wvc_ref, bands[a][1], h0), sink_ref, rows(yb_ref, a))

    _pipelined(stages, scores, finish)


def _lat_attn(qkv, win_ck, win_cv, diff_ck, diff_cv, sink, lam_params, subln, layer, jobs):
    kvw = WIN_KV * HEAD
    wrows = LAT_WIN_BLOCKS * QBLK
    step = lambda b, h: b * DIFF_HEADS + h

    def cache_spec(heads):
        return pl.BlockSpec((None, None, PAST * heads, HEAD), lambda b, h: (b, layer, 0, 0))

    def head_spec(col0):
        return pl.BlockSpec((LAT_LEN, HEAD), lambda b, h: (b, col0 // HEAD + h))

    def seq_spec(col0):
        return pl.BlockSpec((LAT_LEN, kvw), lambda b, h: (b, col0 // kvw))

    c_in, c_args, c_out, c_shapes = _cast_plumbing(jobs, N_LAT_SEQ * DIFF_HEADS, step)
    out = pl.pallas_call(
        functools.partial(_lat_attn_kernel, layer, _cast_plan(jobs)),
        grid=(N_LAT_SEQ, DIFF_HEADS),
        in_specs=[head_spec(A_DQ), cache_spec(DIFF_HEADS), cache_spec(DIFF_HEADS),
                  head_spec(A_DK), head_spec(A_DV),
                  pl.BlockSpec((None, 4, DIFF_QK), lambda b, h: (layer, 0, 0)),
                  _vec_spec(layer, HEAD),
                  pl.BlockSpec((wrows, WIN_HEADS * HEAD), lambda b, h: (step(b, h), 0)),
                  cache_spec(WIN_KV), cache_spec(WIN_KV), seq_spec(A_WK), seq_spec(A_WV),
                  pl.BlockSpec(memory_space=pltpu.SMEM)] + c_in,
        out_specs=[pl.BlockSpec((LAT_LEN, HEAD), lambda b, h: (b, h)),
                   pl.BlockSpec((wrows, WIN_HEADS * HEAD), lambda b, h: (step(b, h), 0))] + c_out,
        out_shape=[jax.ShapeDtypeStruct((ROWS_LAT, DIFF_HEADS * HEAD), jnp.bfloat16),
                   jax.ShapeDtypeStruct((ROWS_LAT, WIN_HEADS * HEAD), jnp.bfloat16)] + c_shapes,
        scratch_shapes=[pltpu.VMEM((PAST + LAT_LEN, HEAD), jnp.bfloat16),
                        pltpu.VMEM((PAST + LAT_LEN, 2 * HEAD), jnp.bfloat16)],
        compiler_params=_params(("arbitrary", "arbitrary")),
        name="lat_attn",
    )(qkv, diff_ck, diff_cv, qkv, qkv, lam_params, subln, qkv, win_ck, win_cv, qkv, qkv, sink, *c_args)
    return out[1], out[0], out[2:]


MERGE_TM = 1024
MERGE_TN = 512


def _merge_kernel(h_ref, ya_ref, ybc_ref, ybl_ref, ycc_ref, ycl_ref,
                  wga_ref, wgb_ref, wgc_ref, wa_ref, wb_ref, wc_ref, o_ref):
    h = h_ref[...]
    yb = _pick_rows(MERGE_TM, ybc_ref, ybl_ref)
    yc = _pick_rows(MERGE_TM, ycc_ref, ycl_ref)
    m = _sigmoid(_dot(h, wga_ref[...])) * _dot(ya_ref[...], wa_ref[...])
    m += _sigmoid(_dot(h, wgb_ref[...])) * _dot(yb, wb_ref[...])
    m += _sigmoid(_dot(h, wgc_ref[...])) * _dot(yc, wc_ref[...])
    o_ref[...] = _bf(m)


def _merge(h, ya, yb_ctx, yb_lat, yc_ctx, yc_lat, w_gate, wa, wb, wc):
    tm, tn = MERGE_TM, MERGE_TN
    nj = D // tn

    def gate_spec(branch):
        return pl.BlockSpec((D, tn), lambda i, j: (0, branch * nj + j))

    w_spec = pl.BlockSpec((CONV_DIM, tn), lambda i, j: (0, j))
    return pl.pallas_call(
        _merge_kernel,
        grid=(ROWS // tm, nj),
        in_specs=[pl.BlockSpec((tm, D), lambda i, j: (i, 0)),
                  pl.BlockSpec((tm, CONV_DIM), lambda i, j: (i, 0)),
                  *_split_specs(tm, WIN_HEADS * HEAD), *_split_specs(tm, DIFF_HEADS * HEAD),
                  gate_spec(0), gate_spec(1), gate_spec(2), w_spec, w_spec, w_spec],
        out_specs=pl.BlockSpec((tm, tn), lambda i, j: (i, j)),
        out_shape=jax.ShapeDtypeStruct((ROWS, D), jnp.bfloat16),
        compiler_params=_params(("parallel", "arbitrary")),
        name="merge",
    )(h, ya, yb_ctx, yb_lat, yc_ctx, yc_lat, w_gate, w_gate, w_gate, wa, wb, wc)


OUT_TM = 512


def _out_proj_kernel(n_x, *refs):
    x_refs = refs[:n_x]
    m_ref, w_ref, gate_ref, g_ref, shift_ref, scale_ref, xo_ref, h_ref = refs[n_x:]
    x = x_refs[0][...] if n_x == 1 else _pick_rows(OUT_TM, *x_refs)
    x = x + gate_ref[...] * _dot(m_ref[...], w_ref[...])
    xo_ref[...] = x
    h_ref[...] = _bf(_rms_mod(x, g_ref[...], scale_ref[...], shift_ref[...]))


def _out_proj(xs, merged, w_out, mods, norm_mlp, layer):
    tm = OUT_TM
    row = pl.BlockSpec((tm, D), lambda i: (i, 0))
    x_specs = [row] if len(xs) == 1 else list(_split_specs(tm, D))
    return pl.pallas_call(
        functools.partial(_out_proj_kernel, len(xs)),
        grid=(ROWS // tm,),
        in_specs=x_specs + [row,
                            pl.BlockSpec((D, D), lambda i: (0, 0)),
                            _mod_spec(layer, 2, tm), _vec_spec(layer, D),
                            _mod_spec(layer, 3, tm), _mod_spec(layer, 4, tm)],
        out_specs=[row, row],
        out_shape=[jax.ShapeDtypeStruct((ROWS, D), jnp.float32),
                   jax.ShapeDtypeStruct((ROWS, D), jnp.bfloat16)],
        compiler_params=_params(("arbitrary",)),
        name="out_proj",
    )(*xs, merged, w_out, mods, norm_mlp.reshape(N_LAYERS, 1, D), mods, mods)


MLP_TM = 512
MLP_TF = 1024


MLP_TILES = ROWS // MLP_TM


def _mlp_kernel(final, h_ref, w1_ref, w2_ref, x_ref, gate_ref, g_ref, shift_ref, scale_ref,
                *rest):
    outs, accs = rest[:-2], rest[-2:]
    i = pl.program_id(0)
    j = pl.program_id(1)

    def matmuls(acc_ref, first_chunk):
        t = jnp.maximum(_dot(h_ref[...], w1_ref[...]), 0.0)
        upd = _dot(_bf(t * t), w2_ref[...])
        acc_ref[...] = upd if first_chunk else acc_ref[...] + upd

    def epilogue(acc_ref):
        x = x_ref[...] + gate_ref[...] * acc_ref[...]
        if final:
            yc_ref, yl_ref = outs
            y = x * lax.rsqrt(jnp.mean(x * x, axis=-1, keepdims=True) + EPS) * g_ref[...]

            @pl.when(i - 1 < ROWS_CTX // MLP_TM)
            def _():
                yc_ref[...] = y

            @pl.when(i - 1 >= ROWS_CTX // MLP_TM)
            def _():
                yl_ref[...] = y
        else:
            xo_ref, hn_ref = outs
            xo_ref[...] = x
            hn_ref[...] = _bf(_rms_mod(x, g_ref[...], scale_ref[...], shift_ref[...]))

    def step(parity, first_chunk, finish_previous):
        def run():
            matmuls(accs[parity], first_chunk)
            if finish_previous:
                epilogue(accs[1 - parity])
        return run

    has_tile = i < MLP_TILES
    pl.when((i == 0) & (j == 0))(step(0, True, False))
    for parity in (0, 1):
        mine = has_tile & (i % 2 == parity)
        pl.when(mine & (i > 0) & (j == 0))(step(parity, True, True))
        pl.when(mine & (j > 0))(step(parity, False, False))
    pl.when((i == MLP_TILES) & (j == 0))(lambda: epilogue(accs[(MLP_TILES - 1) % 2]))


def _mlp(h2, x, w1, w2, mods, norm_next, layer, final):
    tm, tf = MLP_TM, MLP_TF
    last = MLP_TILES - 1
    n_ctx = ROWS_CTX // tm

    def done(i):
        return jnp.clip(i - 1, 0, last)

    def chunk(i, j):
        return jnp.where(i == MLP_TILES, 0, j)

    lagged = pl.BlockSpec((tm, D), lambda i, j: (done(i), 0))
    nxt = layer if final else layer + 1
    g_spec = pl.BlockSpec((None, 1, D), lambda i, j: (0 if final else nxt, 0, 0))
    if final:
        out_specs = [pl.BlockSpec((tm, D), lambda i, j: (jnp.minimum(done(i), n_ctx - 1), 0)),
                     pl.BlockSpec((tm, D), lambda i, j: (jnp.maximum(done(i) - n_ctx, 0), 0))]
        out_shape = [jax.ShapeDtypeStruct((ROWS_CTX, D), jnp.float32),
                     jax.ShapeDtypeStruct((ROWS_LAT, D), jnp.float32)]
    else:
        out_specs = [lagged, lagged]
        out_shape = [jax.ShapeDtypeStruct((ROWS, D), jnp.float32),
                     jax.ShapeDtypeStruct((ROWS, D), jnp.bfloat16)]
    return pl.pallas_call(
        functools.partial(_mlp_kernel, final),
        grid=(MLP_TILES + 1, D_FF // tf),
        in_specs=[pl.BlockSpec((tm, D), lambda i, j: (jnp.minimum(i, last), 0)),
                  pl.BlockSpec((D, tf), lambda i, j: (0, chunk(i, j))),
                  pl.BlockSpec((tf, D), lambda i, j: (chunk(i, j), 0)),
                  lagged, _mod_spec(layer, 5, tm, lag=1), g_spec,
                  _mod_spec(nxt, 0, tm, lag=1), _mod_spec(nxt, 1, tm, lag=1)],
        out_specs=out_specs,
        out_shape=out_shape,
        scratch_shapes=[pltpu.VMEM((tm, D), jnp.float32)] * 2,
        compiler_params=_params(("arbitrary", "arbitrary")),
        name="mlp_final" if final else "mlp",
    )(h2, w1, w2, x, mods, norm_next, mods, mods)


def kernel(x_prompt, x_sample, cache_win_k, cache_win_v, cache_diff_k, cache_diff_v, c, c_ctx, w_ada, b_ada, norm_mix, norm_mlp, w_in, conv_w, win_sink, lambda_q1, lambda_k1, lambda_q2, lambda_k2, diff_subln, w_branch_conv, w_branch_win, w_branch_diff, w_out, w_mlp1, w_mlp2, norm_final):
    xs = (x_prompt.reshape(ROWS_CTX, D), x_sample.reshape(ROWS_LAT, D))
    cvec = jnp.concatenate([c_ctx[None], c, jnp.zeros((N_MOD_ROWS - 1 - N_LAT_SEQ, D), jnp.float32)], axis=0)
    mods = _modulation(cvec, w_ada, b_ada).reshape(N_LAYERS * N_MOD_ROWS * N_MOD, 1, D)

    tables = _rope_tables()
    sink = win_sink.reshape(N_LAYERS * WIN_HEADS)
    lam_params = jnp.stack([lambda_q1, lambda_k1, lambda_q2, lambda_k2], axis=1)
    subln = diff_subln.reshape(N_LAYERS, 1, HEAD)
    ck_win = cache_win_k.reshape(N_LAT_SEQ, N_LAYERS, PAST * WIN_KV, HEAD)
    cv_win = cache_win_v.reshape(N_LAT_SEQ, N_LAYERS, PAST * WIN_KV, HEAD)
    ck_diff = cache_diff_k.reshape(N_LAT_SEQ, N_LAYERS, PAST * DIFF_HEADS, HEAD)
    cv_diff = cache_diff_v.reshape(N_LAT_SEQ, N_LAYERS, PAST * DIFF_HEADS, HEAD)
    norm_final_3d = norm_final.reshape(1, 1, D)
    norm_mix_3d = norm_mix.reshape(N_LAYERS, 1, D)

    h = _prologue(*xs, mods, norm_mix, 0)
    states = None
    for layer in range(N_LAYERS):
        in_job = functools.partial(_CastJob, w_in, layer, cbw=IN_CBW)
        ya, (w_attn, wa, wb, wc, w_o) = _conv_branch(
            h, w_in, conv_w, layer,
            [in_job(k0=OFF_ATTN // IN_CBW, nb=ATTN_COLS // IN_CBW)]
            + [_whole(w, layer) for w in (w_branch_conv, w_branch_win, w_branch_diff, w_out)])
        qkv, states, yb_ctx, yc_ctx, (w_gate,) = _attn_proj(
            h, w_attn, tables, sink, lam_params, subln, layer, states,
            [in_job(k0=OFF_GATE // IN_CBW, nb=3 * D // IN_CBW)])
        yb_lat, yc_lat, (w1, w2) = _lat_attn(qkv, ck_win, cv_win, ck_diff, cv_diff, sink, lam_params, subln,
                                             layer, [_whole(w_mlp1, layer), _whole(w_mlp2, layer)])
        merged = _merge(h, ya, yb_ctx, yb_lat, yc_ctx, yc_lat, w_gate, wa, wb, wc)
        x, h2 = _out_proj(xs, merged, w_o, mods, norm_mlp, layer)
        if layer == N_LAYERS - 1:
            y_ctx, y_lat = _mlp(h2, x, w1, w2, mods, norm_final_3d, layer, True)
        else:
            x, h = _mlp(h2, x, w1, w2, mods, norm_mix_3d, layer, False)
            xs = (x,)

    new_win_k, new_win_v, new_diff_k, new_diff_v = states
    return (y_ctx.reshape(N_CTX_SEQ, CTX_LEN, D), y_lat.reshape(N_LAT_SEQ, LAT_LEN, D),
            new_win_k.reshape(N_CTX_SEQ, N_LAYERS, CTX_LEN, WIN_KV, HEAD),
            new_win_v.reshape(N_CTX_SEQ, N_LAYERS, CTX_LEN, WIN_KV, HEAD),
            new_diff_k.reshape(N_CTX_SEQ, N_LAYERS, CTX_LEN, DIFF_HEADS, HEAD),
            new_diff_v.reshape(N_CTX_SEQ, N_LAYERS, CTX_LEN, DIFF_HEADS, HEAD))
```

```python
import functools
import math
from typing import NamedTuple

import jax
import jax.numpy as jnp
import numpy as np
from jax import lax
from jax.experimental import pallas as pl
from jax.experimental.pallas import tpu as pltpu

D = 2048
N_CTX_SEQ = 16
CTX_LEN = 256
N_LAT_SEQ = 2
LAT_LEN = 2048
N_LAYERS = 2
PAST = 512
GRID_W = 64
HEAD = 128
CONV_DIM = 1024
WIN_HEADS = 8
WIN_KV = 2
WIN_GROUP = WIN_HEADS // WIN_KV
WINDOW = 128
QBLK = 128
DIFF_HEADS = 8
DIFF_QK = 64
D_FF = 4 * D
EPS = 1e-6
ROPE_BASE = 10000.0
NEG = -1e30

ROWS_CTX = N_CTX_SEQ * CTX_LEN
ROWS_LAT = N_LAT_SEQ * LAT_LEN
ROWS = ROWS_CTX + ROWS_LAT
N_MOD_ROWS = 8
N_MOD = 6

OFF_ATTN = 3 * CONV_DIM
ATTN_COLS = (WIN_HEADS + 2 * WIN_KV) * HEAD + 3 * DIFF_HEADS * HEAD
OFF_GATE = OFF_ATTN + ATTN_COLS
IN_CBW = 1536
A_WQ, A_WK, A_WV = 0, WIN_HEADS * HEAD, (WIN_HEADS + WIN_KV) * HEAD
A_DQ = (WIN_HEADS + 2 * WIN_KV) * HEAD
A_DK = A_DQ + DIFF_HEADS * HEAD
A_DV = A_DK + DIFF_HEADS * HEAD

VMEM_LIMIT_V7X = 60 * 1024 * 1024


def _params(sem, vmem=VMEM_LIMIT_V7X):
    return pltpu.CompilerParams(dimension_semantics=sem, vmem_limit_bytes=vmem)


def _bf(x):
    return x.astype(jnp.bfloat16)


def _dot(a, b):
    return jnp.dot(a, b, preferred_element_type=jnp.float32)


def _dot_nt(a, b):
    return lax.dot_general(a, b, (((1,), (1,)), ((), ())), preferred_element_type=jnp.float32)


def _sigmoid(x):
    return 1.0 / (1.0 + jnp.exp(-x))


def _mod_group(i, tm):
    n_ctx = ROWS_CTX // tm
    return jnp.where(i < n_ctx, 0, 1 + (i - n_ctx) // (LAT_LEN // tm))


def _mod_spec(layer, which, tm, lag=0):
    def imap(i, *_):
        group = _mod_group(jnp.maximum(i - lag, 0), tm)
        return (layer * N_MOD_ROWS * N_MOD + group * N_MOD + which, 0, 0)
    return pl.BlockSpec((None, 1, D), imap)


def _vec_spec(layer, width):
    return pl.BlockSpec((None, 1, width), lambda *_: (layer, 0, 0))


def _rms_mod(x, g, scale, shift):
    y = x * lax.rsqrt(jnp.mean(x * x, axis=-1, keepdims=True) + EPS) * g
    return y * (1.0 + scale) + shift


def _mod_kernel(c_ref, w_ref, b_ref, o_ref):
    c = c_ref[...]
    s = c * _sigmoid(c)
    o_ref[...] = _dot(_bf(s), _bf(w_ref[...])) + b_ref[...]


def _modulation(cvec, w_ada, b_ada):
    tn = 1024
    n = N_MOD * D
    return pl.pallas_call(
        _mod_kernel,
        grid=(N_LAYERS, n // tn),
        in_specs=[pl.BlockSpec((N_MOD_ROWS, D), lambda l, j: (0, 0)),
                  pl.BlockSpec((None, D, tn), lambda l, j: (l, 0, j)),
                  pl.BlockSpec((None, 1, tn), lambda l, j: (l, 0, j))],
        out_specs=pl.BlockSpec((None, N_MOD_ROWS, tn), lambda l, j: (l, 0, j)),
        out_shape=jax.ShapeDtypeStruct((N_LAYERS, N_MOD_ROWS, n), jnp.float32),
        compiler_params=_params(("parallel", "parallel")),
        name="modulation",
    )(cvec, w_ada, b_ada.reshape(N_LAYERS, 1, n))


def _split_specs(tm, width):
    n_ctx = ROWS_CTX // tm
    return (pl.BlockSpec((tm, width), lambda i, *_: (jnp.minimum(i, n_ctx - 1), 0)),
            pl.BlockSpec((tm, width), lambda i, *_: (jnp.maximum(i - n_ctx, 0), 0)))


def _pick_rows(tm, ctx_ref, lat_ref):
    return jnp.where(pl.program_id(0) < ROWS_CTX // tm, ctx_ref[...], lat_ref[...])


PRO_TM = 1024


def _prologue_kernel(xc_ref, xl_ref, g_ref, shift_ref, scale_ref, h_ref):
    x = _pick_rows(PRO_TM, xc_ref, xl_ref)
    h_ref[...] = _bf(_rms_mod(x, g_ref[...], scale_ref[...], shift_ref[...]))


def _prologue(x_ctx, x_lat, mods, norm_mix, layer):
    tm = PRO_TM
    return pl.pallas_call(
        _prologue_kernel,
        grid=(ROWS // tm,),
        in_specs=[*_split_specs(tm, D),
                  _vec_spec(layer, D),
                  _mod_spec(layer, 0, tm),
                  _mod_spec(layer, 1, tm)],
        out_specs=pl.BlockSpec((tm, D), lambda i: (i, 0)),
        out_shape=jax.ShapeDtypeStruct((ROWS, D), jnp.bfloat16),
        compiler_params=_params(("arbitrary",)),
        name="prologue",
    )(x_ctx, x_lat, norm_mix.reshape(N_LAYERS, 1, D), mods, mods)


class _CastJob(NamedTuple):
    src: jax.Array
    layer: int
    k0: int
    nb: int
    cbw: int

    @property
    def rows(self):
        return self.src.shape[1]


def _whole(src, layer):
    return _CastJob(src, layer, 0, 1, src.shape[2])


def _cast_plumbing(jobs, n_steps, step_of):
    in_specs, args, out_specs, out_shapes = [], [], [], []
    for job in jobs:
        rows = job.rows // n_steps
        assert rows * n_steps == job.rows and rows % 16 == 0
        for k in range(job.nb):
            in_specs.append(pl.BlockSpec(
                (None, rows, job.cbw), lambda *g, job=job, k=k: (job.layer, step_of(*g), job.k0 + k)))
            args.append(job.src)
        out_specs.append(pl.BlockSpec((rows, job.nb * job.cbw), lambda *g: (step_of(*g), 0)))
        out_shapes.append(jax.ShapeDtypeStruct((job.rows, job.nb * job.cbw), jnp.bfloat16))
    return in_specs, args, out_specs, out_shapes


def _cast_plan(jobs):
    return tuple((job.nb, job.cbw) for job in jobs)


def _run_casts(plan, in_refs, out_refs):
    in_refs = iter(in_refs)
    for (nb, cbw), out_ref in zip(plan, out_refs):
        for k in range(nb):
            out_ref[:, k * cbw:(k + 1) * cbw] = _bf(next(in_refs)[...])


def _n_cast_inputs(plan):
    return sum(nb for nb, _ in plan)


CONV_TM = LAT_LEN
CONV_TN = 256


def _conv_kernel(plan, h_ref, wb_ref, wc_ref, wx_ref, cw_ref, *refs):
    n_in = _n_cast_inputs(plan)
    o_ref = refs[n_in]
    _run_casts(plan, refs[:n_in], refs[n_in + 1:])
    i = pl.program_id(0)
    h = h_ref[...]
    cb = _dot(h, _bf(wb_ref[...]))
    z = _dot(h, _bf(wc_ref[...])) * _dot(h, _bf(wx_ref[...]))
    seq_mask = jnp.where(i < ROWS_CTX // CONV_TM, CTX_LEN - 1, LAT_LEN - 1)
    pos = lax.broadcasted_iota(jnp.int32, z.shape, 0) & seq_mask
    z_prev = jnp.where(pos == 0, 0.0, pltpu.roll(z, 1, 0))
    z_next = jnp.where(pos == seq_mask, 0.0, pltpu.roll(z, CONV_TM - 1, 0))
    cw = cw_ref[...]
    conv = z_prev * cw[0:1, :] + z * cw[1:2, :] + z_next * cw[2:3, :]
    o_ref[...] = _bf(cb * conv)


def _conv_branch(h, w_in, conv_w, layer, jobs):
    nj = CONV_DIM // CONV_TN
    ni = ROWS // CONV_TM

    def wspec(group):
        return pl.BlockSpec((None, D, CONV_TN), lambda i, j: (layer, 0, group * nj + j))

    c_in, c_args, c_out, c_shapes = _cast_plumbing(jobs, ni * nj, lambda i, j: i * nj + j)
    out = pl.pallas_call(
        functools.partial(_conv_kernel, _cast_plan(jobs)),
        grid=(ni, nj),
        in_specs=[pl.BlockSpec((CONV_TM, D), lambda i, j: (i, 0)),
                  wspec(0), wspec(1), wspec(2),
                  pl.BlockSpec((None, 3, CONV_TN), lambda i, j: (layer, 0, j))] + c_in,
        out_specs=[pl.BlockSpec((CONV_TM, CONV_TN), lambda i, j: (i, j))] + c_out,
        out_shape=[jax.ShapeDtypeStruct((ROWS, CONV_DIM), jnp.bfloat16)] + c_shapes,
        compiler_params=_params(("arbitrary", "arbitrary")),
        name="conv_branch",
    )(h, w_in, w_in, w_in, conv_w, *c_args)
    return out[0], out[1:]


ATTN_TM = CTX_LEN
ATTN_CHUNK = 512
ROPE_WIN, ROPE_DIFF, ROPE_NONE = 0, 1, 2
_SLAB_KIND = ([ROPE_WIN] * (WIN_HEADS + WIN_KV) + [ROPE_NONE] * WIN_KV
              + [ROPE_DIFF] * (2 * DIFF_HEADS) + [ROPE_NONE] * DIFF_HEADS)
_ROPE_SHIFT = {ROPE_WIN: HEAD // 4, ROPE_DIFF: DIFF_QK // 4}
_STATE_COLS = ((A_WK, WIN_KV * HEAD), (A_WV, WIN_KV * HEAD),
               (A_DK, DIFF_HEADS * HEAD), (A_DV, DIFF_HEADS * HEAD))
_STATE_ROWS_BY_HEAD = (True, True, False, False)


def _state_block(idx):
    width = _STATE_COLS[idx][1]
    return (CTX_LEN * width // HEAD, HEAD) if _STATE_ROWS_BY_HEAD[idx] else (CTX_LEN, width)
DIFF_Q_SCALE = DIFF_QK ** -0.5 * math.log2(math.e)


def _rope_tables():
    t = np.arange(LAT_LEN, dtype=np.int32)
    rows = (t // GRID_W).astype(np.float32)[:, None]
    cols = (t % GRID_W).astype(np.float32)[:, None]
    lane = np.arange(HEAD, dtype=np.int32)[None, :]
    out = []
    for kind in (ROPE_WIN, ROPE_DIFF):
        s = _ROPE_SHIFT[kind]
        within = lane % (4 * s)
        freq = (within % s).astype(np.float32)
        inv = np.float32(ROPE_BASE) ** (-freq / np.float32(s))
        ang = (np.where(within < 2 * s, rows, cols) * inv).astype(np.float32)
        first = (within % (2 * s)) < s
        cos, sin = np.cos(ang), np.sin(ang)
        out += [cos, np.where(first, -sin, 0.0), np.where(first, 0.0, sin)]
    return jnp.asarray(np.stack(out).astype(np.float32))


def _attn_proj_kernel(first, layer, plan, h_ref, w_ref, tab_ref, sink_ref, lam_ref, g_ref, *refs):
    n_in = _n_cast_inputs(plan)
    n_st = len(_STATE_COLS)
    pos = (0 if first else n_st) + n_in
    cast_in = refs[pos - n_in:pos]
    qkv_ref = refs[pos]
    state_refs = refs[pos + 1:pos + 1 + n_st]
    yb_ref, yc_ref = refs[pos + 1 + n_st:pos + 3 + n_st]
    cast_out = refs[pos + 3 + n_st:-2]
    tiles = refs[-2:]
    i = pl.program_id(0)

    def project(ctx, dst_ref):
        h = h_ref[...]
        if ctx and first:
            for ref in state_refs:
                ref[1:] = jnp.zeros((N_LAYERS - 1,) + ref.shape[1:], jnp.float32)
        for c0 in range(0, ATTN_COLS, ATTN_CHUNK):
            acc = _dot(h, w_ref[:, c0:c0 + ATTN_CHUNK])
            for s0 in range(0, ATTN_CHUNK, HEAD):
                col = c0 + s0
                x = acc[:, s0:s0 + HEAD]
                kind = _SLAB_KIND[col // HEAD]
                if ctx:
                    for ref, (src, width), by_head in zip(state_refs, _STATE_COLS, _STATE_ROWS_BY_HEAD):
                        if src <= col < src + width:
                            dst = col - src
                            if by_head:
                                where = (pl.ds(dst // HEAD, CTX_LEN, stride=width // HEAD), slice(None))
                            else:
                                where = (slice(None), slice(dst, dst + HEAD))
                            if first:
                                ref[(0,) + where] = x
                            else:
                                ref[where] = x
                elif kind != ROPE_NONE:
                    s = _ROPE_SHIFT[kind]
                    x = (x * tab_ref[3 * kind] + pltpu.roll(x, HEAD - s, 1) * tab_ref[3 * kind + 1]
                         + pltpu.roll(x, s, 1) * tab_ref[3 * kind + 2])
                if A_DQ <= col < A_DK:
                    x = x * DIFF_Q_SCALE
                dst_ref[:, col:col + HEAD] = _bf(x)

    def step(ctx, parity, attend):
        def run():
            _run_casts(plan, cast_in, cast_out)
            if attend:
                _ctx_attention(layer, tiles[1 - parity], sink_ref, lam_ref, g_ref, yb_ref, yc_ref)
            project(ctx, tiles[parity] if ctx else qkv_ref)
        return run

    @pl.when(i == 0)
    def _():
        tiles[1][...] = jnp.zeros_like(tiles[1])

    is_ctx = i < N_CTX_SEQ
    for parity in (0, 1):
        pl.when(is_ctx & (i % 2 == parity))(step(True, parity, True))
    pl.when(i == N_CTX_SEQ)(step(False, N_CTX_SEQ % 2, True))
    pl.when(i > N_CTX_SEQ)(step(False, 0, False))


def _attn_proj(h, w_attn, tables, sink, lam_params, subln, layer, prev_states, jobs):
    tm = ATTN_TM
    per_lat = LAT_LEN // tm
    first = prev_states is None
    last_ctx = N_CTX_SEQ - 1

    def state_spec(idx):
        if first:
            return pl.BlockSpec((None, N_LAYERS) + _state_block(idx), lambda i: (jnp.minimum(i, last_ctx), 0, 0, 0))
        return pl.BlockSpec((None, None) + _state_block(idx), lambda i: (jnp.minimum(i, last_ctx), layer, 0, 0))

    in_specs = [pl.BlockSpec((tm, D), lambda i: (i, 0)),
                pl.BlockSpec((D, ATTN_COLS), lambda i: (0, 0), pipeline_mode=pl.Buffered(1)),
                pl.BlockSpec((6, tm, HEAD), lambda i: (0, jnp.maximum(i - N_CTX_SEQ, 0) % per_lat, 0)),
                pl.BlockSpec(memory_space=pltpu.SMEM),
                pl.BlockSpec((None, 4, DIFF_QK), lambda i: (layer, 0, 0)),
                _vec_spec(layer, HEAD)]
    args = [h, w_attn, tables, sink, lam_params, subln]
    aliases = {}
    if not first:
        for k, st in enumerate(prev_states):
            in_specs.append(pl.BlockSpec(memory_space=pl.ANY))
            aliases[len(args)] = 1 + k
            args.append(st)
    c_in, c_args, c_out, c_shapes = _cast_plumbing(jobs, ROWS // tm, lambda i: i)
    y_spec = pl.BlockSpec((CTX_LEN, WIN_HEADS * HEAD), lambda i: (jnp.clip(i - 1, 0, last_ctx), 0))
    y_shape = jax.ShapeDtypeStruct((ROWS_CTX, WIN_HEADS * HEAD), jnp.bfloat16)
    out = pl.pallas_call(
        functools.partial(_attn_proj_kernel, first, layer, _cast_plan(jobs)),
        grid=(ROWS // tm,),
        in_specs=in_specs + c_in,
        out_specs=[pl.BlockSpec((tm, ATTN_COLS), lambda i: (jnp.maximum(i - N_CTX_SEQ, 0), 0))]
                  + [state_spec(idx) for idx in range(len(_STATE_COLS))] + [y_spec, y_spec] + c_out,
        out_shape=[jax.ShapeDtypeStruct((ROWS_LAT, ATTN_COLS), jnp.bfloat16)]
                  + [jax.ShapeDtypeStruct((N_CTX_SEQ, N_LAYERS) + _state_block(idx), jnp.float32)
                     for idx in range(len(_STATE_COLS))] + [y_shape, y_shape] + c_shapes,
        scratch_shapes=[pltpu.VMEM((CTX_LEN, ATTN_COLS), jnp.bfloat16)] * 2,
        input_output_aliases=aliases,
        compiler_params=_params(("arbitrary",)),
        name="attn_proj",
    )(*args, *c_args)
    return out[0], out[1:5], out[5], out[6], out[7:]


def _slab(ref, col0, idx):
    return ref[:, col0 + idx * HEAD:col0 + (idx + 1) * HEAD]


def _with_ones(v):
    return jnp.concatenate([v, jnp.ones_like(v)], axis=1)


def _pipelined(items, scores, finish):
    s = scores(items[0])
    for r, item in enumerate(items):
        s_next = scores(items[r + 1]) if r + 1 < len(items) else None
        finish(item, s)
        s = s_next


WIN_STAGE = WIN_GROUP
WIN_STAGES = tuple(range(0, WIN_HEADS, WIN_STAGE))


def _win_scores(q_ref, col0, h0, k, bias):
    q = jnp.concatenate([_slab(q_ref, col0, h0 + hh) for hh in range(WIN_STAGE)], axis=0)
    s = _dot_nt(q, k) * (HEAD ** -0.5)
    return s if bias is None else s + bias


def _win_finish(layer, h0, s, v, sink_ref, o_ref):
    rows = s.shape[0] // WIN_STAGE
    sink = jnp.concatenate(
        [jnp.full((rows, 1), sink_ref[layer * WIN_HEADS + h0 + hh], jnp.float32)
         for hh in range(WIN_STAGE)], axis=0)
    m = jnp.maximum(jnp.max(s, axis=-1, keepdims=True), sink)
    r = _dot(_bf(jnp.exp(s - m)), _with_ones(v))
    o = r[:, :HEAD] / (r[:, HEAD:HEAD + 1] + jnp.exp(sink - m))
    for hh in range(WIN_STAGE):
        head = h0 + hh
        o_ref[:, head * HEAD:(head + 1) * HEAD] = _bf(o[hh * rows:(hh + 1) * rows])


def _win_band(n, kseq_ref, vseq_ref):
    start = pl.multiple_of(jnp.clip((n - 1) * QBLK, 0, LAT_LEN - 3 * QBLK), QBLK)
    kb = kseq_ref[pl.ds(start, 3 * QBLK), :]
    vb = vseq_ref[pl.ds(start, 3 * QBLK), :]
    shape = (QBLK, PAST + 3 * QBLK)
    col = lax.broadcasted_iota(jnp.int32, shape, 1)
    qpos = n * QBLK + lax.broadcasted_iota(jnp.int32, shape, 0)
    kpos = start + col - PAST
    bias = jnp.where((col < PAST) | (jnp.abs(qpos - kpos) <= WINDOW), 0.0, NEG)
    return kb, vb, jnp.concatenate([bias] * WIN_STAGE, axis=0)


def _win_keys(cache_ref, band, h0):
    g = h0 // WIN_GROUP
    cached = cache_ref[pl.ds(g, PAST, stride=WIN_KV), :]
    return jnp.concatenate([_bf(cached), band[:, g * HEAD:(g + 1) * HEAD]], axis=0)


def _lam_init(layer):
    return 0.8 - 0.6 * math.exp(-0.3 * layer)


def _diff_lambda(layer, lam_ref):
    lp = lam_ref[...]
    return (jnp.exp(jnp.sum(lp[0:1] * lp[1:2], axis=-1, keepdims=True))
            - jnp.exp(jnp.sum(lp[2:3] * lp[3:4], axis=-1, keepdims=True)) + _lam_init(layer))


def _diff_scores(q, k):
    first = lax.broadcasted_iota(jnp.int32, q.shape, 1) < DIFF_QK
    zero = jnp.zeros_like(q)
    return _dot_nt(jnp.concatenate([jnp.where(first, q, zero), jnp.where(first, zero, q)], axis=0), k)


def _diff_finish(layer, s, v_ones, lam, g):
    tq = s.shape[0] // 2
    e = _bf(jnp.exp2(s - jnp.max(s, axis=-1, keepdims=True)))
    r = _dot(e, v_ones)
    o = (r[:tq, :HEAD] * (1.0 / r[:tq, HEAD:HEAD + 1])
         - r[tq:, :HEAD] * (lam / r[tq:, HEAD:HEAD + 1]))
    o = o * lax.rsqrt(jnp.mean(o * o, axis=-1, keepdims=True) + EPS) * g
    return _bf(o * (1.0 - _lam_init(layer)))


def _ctx_attention(layer, qkv_ref, sink_ref, lam_ref, g_ref, yb_ref, yc_ref):
    lam = _diff_lambda(layer, lam_ref)
    stages = [("win", h0) for h0 in WIN_STAGES] + [("diff", h) for h in range(DIFF_HEADS)]

    def scores(stage):
        kind, idx = stage
        if kind == "win":
            return _win_scores(qkv_ref, A_WQ, idx, _slab(qkv_ref, A_WK, idx // WIN_GROUP), None)
        return _diff_scores(_slab(qkv_ref, A_DQ, idx), _slab(qkv_ref, A_DK, idx))

    def finish(stage, s):
        kind, idx = stage
        if kind == "win":
            _win_finish(layer, idx, s, _slab(qkv_ref, A_WV, idx // WIN_GROUP), sink_ref, yb_ref)
        else:
            yc_ref[:, idx * HEAD:(idx + 1) * HEAD] = _diff_finish(
                layer, s, _with_ones(_slab(qkv_ref, A_DV, idx)), lam, g_ref[...])

    _pipelined(stages, scores, finish)


LAT_DIFF_SUB = 128
LAT_WIN_BLOCKS = LAT_LEN // QBLK // DIFF_HEADS


def _lat_attn_kernel(layer, plan, dq_ref, dkc_ref, dvc_ref, dk_ref, dv_ref, lam_ref, g_ref,
                     wq_ref, wkc_ref, wvc_ref, wk_ref, wv_ref, sink_ref, *refs):
    n_in = _n_cast_inputs(plan)
    yc_ref, yb_ref = refs[n_in:n_in + 2]
    kall_ref, vall_ref = refs[-2:]
    _run_casts(plan, refs[:n_in], refs[n_in + 2:-2])
    h = pl.program_id(1)

    head_rows = pl.ds(h, PAST, stride=DIFF_HEADS)
    kall_ref[0:PAST] = _bf(dkc_ref[head_rows, :])
    kall_ref[PAST:] = dk_ref[...]
    vall_ref[0:PAST] = _with_ones(_bf(dvc_ref[head_rows, :]))
    vall_ref[PAST:] = _with_ones(dv_ref[...])
    lam = _diff_lambda(layer, lam_ref)
    bands = [_win_band(h * LAT_WIN_BLOCKS + blk, wk_ref, wv_ref) for blk in range(LAT_WIN_BLOCKS)]

    diff_stages = [("diff", r0, None) for r0 in range(0, LAT_LEN, LAT_DIFF_SUB)]
    win_stages = [("win", blk, h0) for blk in range(LAT_WIN_BLOCKS) for h0 in WIN_STAGES]
    every = len(diff_stages) // len(win_stages)
    stages = []
    for k, stage in enumerate(diff_stages):
        stages.append(stage)
        if k % every == 0 and k // every < len(win_stages):
            stages.append(win_stages[k // every])

    def rows(ref, blk):
        return ref.at[pl.ds(blk * QBLK, QBLK)]

    def scores(stage):
        kind, a, h0 = stage
        if kind == "diff":
            return _diff_scores(dq_ref[a:a + LAT_DIFF_SUB], kall_ref[...])
        kb, _, bias = bands[a]
        return _win_scores(rows(wq_ref, a), 0, h0, _win_keys(wkc_ref, kb, h0), bias)

    def finish(stage, s):
        kind, a, h0 = stage
        if kind == "diff":
            yc_ref[a:a + LAT_DIFF_SUB] = _diff_finish(layer, s, vall_ref[...], lam, g_ref[...])
        else:
            _win_finish(layer, h0, s, _win_keys(wvc_ref, bands[a][1], h0), sink_ref, rows(yb_ref, a))

    _pipelined(stages, scores, finish)


def _lat_attn(qkv, win_ck, win_cv, diff_ck, diff_cv, sink, lam_params, subln, layer, jobs):
    kvw = WIN_KV * HEAD
    wrows = LAT_WIN_BLOCKS * QBLK

    def step(b, h):
        return b * DIFF_HEADS + h

    def cache_spec(heads):
        return pl.BlockSpec((None, None, PAST * heads, HEAD), lambda b, h: (b, layer, 0, 0))

    def head_spec(col0):
        return pl.BlockSpec((LAT_LEN, HEAD), lambda b, h: (b, col0 // HEAD + h))

    def seq_spec(col0):
        return pl.BlockSpec((LAT_LEN, kvw), lambda b, h: (b, col0 // kvw))

    c_in, c_args, c_out, c_shapes = _cast_plumbing(jobs, N_LAT_SEQ * DIFF_HEADS, step)
    out = pl.pallas_call(
        functools.partial(_lat_attn_kernel, layer, _cast_plan(jobs)),
        grid=(N_LAT_SEQ, DIFF_HEADS),
        in_specs=[head_spec(A_DQ), cache_spec(DIFF_HEADS), cache_spec(DIFF_HEADS),
                  head_spec(A_DK), head_spec(A_DV),
                  pl.BlockSpec((None, 4, DIFF_QK), lambda b, h: (layer, 0, 0)),
                  _vec_spec(layer, HEAD),
                  pl.BlockSpec((wrows, WIN_HEADS * HEAD), lambda b, h: (step(b, h), 0)),
                  cache_spec(WIN_KV), cache_spec(WIN_KV), seq_spec(A_WK), seq_spec(A_WV),
                  pl.BlockSpec(memory_space=pltpu.SMEM)] + c_in,
        out_specs=[pl.BlockSpec((LAT_LEN, HEAD), lambda b, h: (b, h)),
                   pl.BlockSpec((wrows, WIN_HEADS * HEAD), lambda b, h: (step(b, h), 0))] + c_out,
        out_shape=[jax.ShapeDtypeStruct((ROWS_LAT, DIFF_HEADS * HEAD), jnp.bfloat16),
                   jax.ShapeDtypeStruct((ROWS_LAT, WIN_HEADS * HEAD), jnp.bfloat16)] + c_shapes,
        scratch_shapes=[pltpu.VMEM((PAST + LAT_LEN, HEAD), jnp.bfloat16),
                        pltpu.VMEM((PAST + LAT_LEN, 2 * HEAD), jnp.bfloat16)],
        compiler_params=_params(("arbitrary", "arbitrary")),
        name="lat_attn",
    )(qkv, diff_ck, diff_cv, qkv, qkv, lam_params, subln, qkv, win_ck, win_cv, qkv, qkv, sink, *c_args)
    return out[1], out[0], out[2:]


MERGE_TM = 1024
MERGE_TN = 512


def _merge_kernel(h_ref, ya_ref, ybc_ref, ybl_ref, ycc_ref, ycl_ref,
                  wga_ref, wgb_ref, wgc_ref, wa_ref, wb_ref, wc_ref, o_ref):
    h = h_ref[...]
    yb = _pick_rows(MERGE_TM, ybc_ref, ybl_ref)
    yc = _pick_rows(MERGE_TM, ycc_ref, ycl_ref)
    m = _sigmoid(_dot(h, wga_ref[...])) * _dot(ya_ref[...], wa_ref[...])
    m += _sigmoid(_dot(h, wgb_ref[...])) * _dot(yb, wb_ref[...])
    m += _sigmoid(_dot(h, wgc_ref[...])) * _dot(yc, wc_ref[...])
    o_ref[...] = _bf(m)


def _merge(h, ya, yb_ctx, yb_lat, yc_ctx, yc_lat, w_gate, wa, wb, wc):
    tm, tn = MERGE_TM, MERGE_TN
    nj = D // tn

    def gate_spec(branch):
        return pl.BlockSpec((D, tn), lambda i, j: (0, branch * nj + j))

    w_spec = pl.BlockSpec((CONV_DIM, tn), lambda i, j: (0, j))
    return pl.pallas_call(
        _merge_kernel,
        grid=(ROWS // tm, nj),
        in_specs=[pl.BlockSpec((tm, D), lambda i, j: (i, 0)),
                  pl.BlockSpec((tm, CONV_DIM), lambda i, j: (i, 0)),
                  *_split_specs(tm, WIN_HEADS * HEAD), *_split_specs(tm, DIFF_HEADS * HEAD),
                  gate_spec(0), gate_spec(1), gate_spec(2), w_spec, w_spec, w_spec],
        out_specs=pl.BlockSpec((tm, tn), lambda i, j: (i, j)),
        out_shape=jax.ShapeDtypeStruct((ROWS, D), jnp.bfloat16),
        compiler_params=_params(("parallel", "arbitrary")),
        name="merge",
    )(h, ya, yb_ctx, yb_lat, yc_ctx, yc_lat, w_gate, w_gate, w_gate, wa, wb, wc)


OUT_TM = 512


def _out_proj_kernel(n_x, *refs):
    x_refs = refs[:n_x]
    m_ref, w_ref, gate_ref, g_ref, shift_ref, scale_ref, xo_ref, h_ref = refs[n_x:]
    x = x_refs[0][...] if n_x == 1 else _pick_rows(OUT_TM, *x_refs)
    x = x + gate_ref[...] * _dot(m_ref[...], w_ref[...])
    xo_ref[...] = x
    h_ref[...] = _bf(_rms_mod(x, g_ref[...], scale_ref[...], shift_ref[...]))


def _out_proj(xs, merged, w_out, mods, norm_mlp, layer):
    tm = OUT_TM
    row = pl.BlockSpec((tm, D), lambda i: (i, 0))
    x_specs = [row] if len(xs) == 1 else list(_split_specs(tm, D))
    return pl.pallas_call(
        functools.partial(_out_proj_kernel, len(xs)),
        grid=(ROWS // tm,),
        in_specs=x_specs + [row,
                            pl.BlockSpec((D, D), lambda i: (0, 0)),
                            _mod_spec(layer, 2, tm), _vec_spec(layer, D),
                            _mod_spec(layer, 3, tm), _mod_spec(layer, 4, tm)],
        out_specs=[row, row],
        out_shape=[jax.ShapeDtypeStruct((ROWS, D), jnp.float32),
                   jax.ShapeDtypeStruct((ROWS, D), jnp.bfloat16)],
        compiler_params=_params(("arbitrary",)),
        name="out_proj",
    )(*xs, merged, w_out, mods, norm_mlp.reshape(N_LAYERS, 1, D), mods, mods)


MLP_TM = 512
MLP_TF = 1024


MLP_TILES = ROWS // MLP_TM


def _mlp_kernel(final, h_ref, w1_ref, w2_ref, x_ref, gate_ref, g_ref, shift_ref, scale_ref,
                *rest):
    outs, accs = rest[:-2], rest[-2:]
    i = pl.program_id(0)
    j = pl.program_id(1)

    def matmuls(acc_ref, first_chunk):
        t = jnp.maximum(_dot(h_ref[...], w1_ref[...]), 0.0)
        upd = _dot(_bf(t * t), w2_ref[...])
        acc_ref[...] = upd if first_chunk else acc_ref[...] + upd

    def epilogue(acc_ref):
        x = x_ref[...] + gate_ref[...] * acc_ref[...]
        if final:
            yc_ref, yl_ref = outs
            y = x * lax.rsqrt(jnp.mean(x * x, axis=-1, keepdims=True) + EPS) * g_ref[...]

            @pl.when(i - 1 < ROWS_CTX // MLP_TM)
            def _():
                yc_ref[...] = y

            @pl.when(i - 1 >= ROWS_CTX // MLP_TM)
            def _():
                yl_ref[...] = y
        else:
            xo_ref, hn_ref = outs
            xo_ref[...] = x
            hn_ref[...] = _bf(_rms_mod(x, g_ref[...], scale_ref[...], shift_ref[...]))

    def step(parity, first_chunk, finish_previous):
        def run():
            matmuls(accs[parity], first_chunk)
            if finish_previous:
                epilogue(accs[1 - parity])
        return run

    has_tile = i < MLP_TILES
    pl.when((i == 0) & (j == 0))(step(0, True, False))
    for parity in (0, 1):
        mine = has_tile & (i % 2 == parity)
        pl.when(mine & (i > 0) & (j == 0))(step(parity, True, True))
        pl.when(mine & (j > 0))(step(parity, False, False))
    pl.when((i == MLP_TILES) & (j == 0))(lambda: epilogue(accs[(MLP_TILES - 1) % 2]))


def _mlp(h2, x, w1, w2, mods, norm_next, layer, final):
    tm, tf = MLP_TM, MLP_TF
    last = MLP_TILES - 1
    n_ctx = ROWS_CTX // tm

    def done(i):
        return jnp.clip(i - 1, 0, last)

    def chunk(i, j):
        return jnp.where(i == MLP_TILES, 0, j)

    lagged = pl.BlockSpec((tm, D), lambda i, j: (done(i), 0))
    nxt = layer if final else layer + 1
    g_spec = pl.BlockSpec((None, 1, D), lambda i, j: (0 if final else nxt, 0, 0))
    if final:
        out_specs = [pl.BlockSpec((tm, D), lambda i, j: (jnp.minimum(done(i), n_ctx - 1), 0)),
                     pl.BlockSpec((tm, D), lambda i, j: (jnp.maximum(done(i) - n_ctx, 0), 0))]
        out_shape = [jax.ShapeDtypeStruct((ROWS_CTX, D), jnp.float32),
                     jax.ShapeDtypeStruct((ROWS_LAT, D), jnp.float32)]
    else:
        out_specs = [lagged, lagged]
        out_shape = [jax.ShapeDtypeStruct((ROWS, D), jnp.float32),
                     jax.ShapeDtypeStruct((ROWS, D), jnp.bfloat16)]
    return pl.pallas_call(
        functools.partial(_mlp_kernel, final),
        grid=(MLP_TILES + 1, D_FF // tf),
        in_specs=[pl.BlockSpec((tm, D), lambda i, j: (jnp.minimum(i, last), 0)),
                  pl.BlockSpec((D, tf), lambda i, j: (0, chunk(i, j))),
                  pl.BlockSpec((tf, D), lambda i, j: (chunk(i, j), 0)),
                  lagged, _mod_spec(layer, 5, tm, lag=1), g_spec,
                  _mod_spec(nxt, 0, tm, lag=1), _mod_spec(nxt, 1, tm, lag=1)],
        out_specs=out_specs,
        out_shape=out_shape,
        scratch_shapes=[pltpu.VMEM((tm, D), jnp.float32)] * 2,
        compiler_params=_params(("arbitrary", "arbitrary")),
        name="mlp_final" if final else "mlp",
    )(h2, w1, w2, x, mods, norm_next, mods, mods)


def kernel(x_prompt, x_sample, cache_win_k, cache_win_v, cache_diff_k, cache_diff_v, c, c_ctx, w_ada, b_ada, norm_mix, norm_mlp, w_in, conv_w, win_sink, lambda_q1, lambda_k1, lambda_q2, lambda_k2, diff_subln, w_branch_conv, w_branch_win, w_branch_diff, w_out, w_mlp1, w_mlp2, norm_final):
    xs = (x_prompt.reshape(ROWS_CTX, D), x_sample.reshape(ROWS_LAT, D))
    cvec = jnp.concatenate([c_ctx[None], c, jnp.zeros((N_MOD_ROWS - 1 - N_LAT_SEQ, D), jnp.float32)], axis=0)
    mods = _modulation(cvec, w_ada, b_ada).reshape(N_LAYERS * N_MOD_ROWS * N_MOD, 1, D)

    tables = _rope_tables()
    sink = win_sink.reshape(N_LAYERS * WIN_HEADS)
    lam_params = jnp.stack([lambda_q1, lambda_k1, lambda_q2, lambda_k2], axis=1)
    subln = diff_subln.reshape(N_LAYERS, 1, HEAD)
    ck_win = cache_win_k.reshape(N_LAT_SEQ, N_LAYERS, PAST * WIN_KV, HEAD)
    cv_win = cache_win_v.reshape(N_LAT_SEQ, N_LAYERS, PAST * WIN_KV, HEAD)
    ck_diff = cache_diff_k.reshape(N_LAT_SEQ, N_LAYERS, PAST * DIFF_HEADS, HEAD)
    cv_diff = cache_diff_v.reshape(N_LAT_SEQ, N_LAYERS, PAST * DIFF_HEADS, HEAD)
    norm_final_3d = norm_final.reshape(1, 1, D)
    norm_mix_3d = norm_mix.reshape(N_LAYERS, 1, D)

    h = _prologue(*xs, mods, norm_mix, 0)
    states = None
    for layer in range(N_LAYERS):
        in_job = functools.partial(_CastJob, w_in, layer, cbw=IN_CBW)
        ya, (w_attn, wa, wb, wc, w_o) = _conv_branch(
            h, w_in, conv_w, layer,
            [in_job(k0=OFF_ATTN // IN_CBW, nb=ATTN_COLS // IN_CBW)]
            + [_whole(w, layer) for w in (w_branch_conv, w_branch_win, w_branch_diff, w_out)])
        qkv, states, yb_ctx, yc_ctx, (w_gate,) = _attn_proj(
            h, w_attn, tables, sink, lam_params, subln, layer, states,
            [in_job(k0=OFF_GATE // IN_CBW, nb=3 * D // IN_CBW)])
        yb_lat, yc_lat, (w1, w2) = _lat_attn(qkv, ck_win, cv_win, ck_diff, cv_diff, sink, lam_params, subln,
                                             layer, [_whole(w_mlp1, layer), _whole(w_mlp2, layer)])
        merged = _merge(h, ya, yb_ctx, yb_lat, yc_ctx, yc_lat, w_gate, wa, wb, wc)
        x, h2 = _out_proj(xs, merged, w_o, mods, norm_mlp, layer)
        if layer == N_LAYERS - 1:
            y_ctx, y_lat = _mlp(h2, x, w1, w2, mods, norm_final_3d, layer, True)
        else:
            x, h = _mlp(h2, x, w1, w2, mods, norm_mix_3d, layer, False)
            xs = (x,)

    new_win_k, new_win_v, new_diff_k, new_diff_v = states
    return (y_ctx.reshape(N_CTX_SEQ, CTX_LEN, D), y_lat.reshape(N_LAT_SEQ, LAT_LEN, D),
            new_win_k.reshape(N_CTX_SEQ, N_LAYERS, CTX_LEN, WIN_KV, HEAD),
            new_win_v.reshape(N_CTX_SEQ, N_LAYERS, CTX_LEN, WIN_KV, HEAD),
            new_diff_k.reshape(N_CTX_SEQ, N_LAYERS, CTX_LEN, DIFF_HEADS, HEAD),
            new_diff_v.reshape(N_CTX_SEQ, N_LAYERS, CTX_LEN, DIFF_HEADS, HEAD))
```

```python
import functools
import math
from typing import NamedTuple

import jax
import jax.numpy as jnp
import numpy as np
from jax import lax
from jax.experimental import pallas as pl
from jax.experimental.pallas import tpu as pltpu

D = 2048
N_CTX_SEQ = 16
CTX_LEN = 256
N_LAT_SEQ = 2
LAT_LEN = 2048
N_LAYERS = 2
PAST = 512
GRID_W = 64
HEAD = 128
CONV_DIM = 1024
WIN_HEADS = 8
WIN_KV = 2
WIN_GROUP = WIN_HEADS // WIN_KV
WINDOW = 128
QBLK = 128
DIFF_HEADS = 8
DIFF_QK = 64
D_FF = 4 * D
EPS = 1e-6
ROPE_BASE = 10000.0
NEG = -1e30

ROWS_CTX = N_CTX_SEQ * CTX_LEN
ROWS_LAT = N_LAT_SEQ * LAT_LEN
ROWS = ROWS_CTX + ROWS_LAT
N_MOD_ROWS = 8
N_MOD = 6

OFF_ATTN = 3 * CONV_DIM
ATTN_COLS = (WIN_HEADS + 2 * WIN_KV) * HEAD + 3 * DIFF_HEADS * HEAD
OFF_GATE = OFF_ATTN + ATTN_COLS
IN_CBW = 1536
A_WQ, A_WK, A_WV = 0, WIN_HEADS * HEAD, (WIN_HEADS + WIN_KV) * HEAD
A_DQ = (WIN_HEADS + 2 * WIN_KV) * HEAD
A_DK = A_DQ + DIFF_HEADS * HEAD
A_DV = A_DK + DIFF_HEADS * HEAD

VMEM_LIMIT_V7X = 60 * 1024 * 1024


def _params(sem, vmem=VMEM_LIMIT_V7X):
    return pltpu.CompilerParams(dimension_semantics=sem, vmem_limit_bytes=vmem)


def _bf(x):
    return x.astype(jnp.bfloat16)


def _dot(a, b):
    return jnp.dot(a, b, preferred_element_type=jnp.float32)


def _dot_nt(a, b):
    return lax.dot_general(a, b, (((1,), (1,)), ((), ())), preferred_element_type=jnp.float32)


def _sigmoid(x):
    return 1.0 / (1.0 + jnp.exp(-x))


def _mod_group(i, tm):
    n_ctx = ROWS_CTX // tm
    return jnp.where(i < n_ctx, 0, 1 + (i - n_ctx) // (LAT_LEN // tm))


def _mod_spec(layer, which, tm, lag=0):
    def imap(i, *_):
        group = _mod_group(jnp.maximum(i - lag, 0), tm)
        return (layer * N_MOD_ROWS * N_MOD + group * N_MOD + which, 0, 0)
    return pl.BlockSpec((None, 1, D), imap)


def _vec_spec(layer, width):
    return pl.BlockSpec((None, 1, width), lambda *_: (layer, 0, 0))


def _rms_mod(x, g, scale, shift):
    y = x * lax.rsqrt(jnp.mean(x * x, axis=-1, keepdims=True) + EPS) * g
    return y * (1.0 + scale) + shift


def _mod_kernel(c_ref, w_ref, b_ref, o_ref):
    c = c_ref[...]
    s = c * _sigmoid(c)
    o_ref[...] = _dot(_bf(s), _bf(w_ref[...])) + b_ref[...]


def _modulation(cvec, w_ada, b_ada):
    tn = 1024
    n = N_MOD * D
    return pl.pallas_call(
        _mod_kernel,
        grid=(N_LAYERS, n // tn),
        in_specs=[pl.BlockSpec((N_MOD_ROWS, D), lambda l, j: (0, 0)),
                  pl.BlockSpec((None, D, tn), lambda l, j: (l, 0, j)),
                  pl.BlockSpec((None, 1, tn), lambda l, j: (l, 0, j))],
        out_specs=pl.BlockSpec((None, N_MOD_ROWS, tn), lambda l, j: (l, 0, j)),
        out_shape=jax.ShapeDtypeStruct((N_LAYERS, N_MOD_ROWS, n), jnp.float32),
        compiler_params=_params(("parallel", "parallel")),
        name="modulation",
    )(cvec, w_ada, b_ada.reshape(N_LAYERS, 1, n))


def _split_specs(tm, width):
    n_ctx = ROWS_CTX // tm
    return (pl.BlockSpec((tm, width), lambda i, *_: (jnp.minimum(i, n_ctx - 1), 0)),
            pl.BlockSpec((tm, width), lambda i, *_: (jnp.maximum(i - n_ctx, 0), 0)))


def _pick_rows(tm, ctx_ref, lat_ref):
    return jnp.where(pl.program_id(0) < ROWS_CTX // tm, ctx_ref[...], lat_ref[...])


PRO_TM = 1024


def _prologue_kernel(xc_ref, xl_ref, g_ref, shift_ref, scale_ref, h_ref):
    x = _pick_rows(PRO_TM, xc_ref, xl_ref)
    h_ref[...] = _bf(_rms_mod(x, g_ref[...], scale_ref[...], shift_ref[...]))


def _prologue(x_ctx, x_lat, mods, norm_mix, layer):
    tm = PRO_TM
    return pl.pallas_call(
        _prologue_kernel,
        grid=(ROWS // tm,),
        in_specs=[*_split_specs(tm, D),
                  _vec_spec(layer, D),
                  _mod_spec(layer, 0, tm),
                  _mod_spec(layer, 1, tm)],
        out_specs=pl.BlockSpec((tm, D), lambda i: (i, 0)),
        out_shape=jax.ShapeDtypeStruct((ROWS, D), jnp.bfloat16),
        compiler_params=_params(("arbitrary",)),
        name="prologue",
    )(x_ctx, x_lat, norm_mix.reshape(N_LAYERS, 1, D), mods, mods)


class _CastJob(NamedTuple):
    src: jax.Array
    layer: int
    k0: int
    nb: int
    cbw: int

    @property
    def rows(self):
        return self.src.shape[1]


def _whole(src, layer):
    return _CastJob(src, layer, 0, 1, src.shape[2])


def _cast_plumbing(jobs, n_steps, step_of):
    in_specs, args, out_specs, out_shapes = [], [], [], []
    for job in jobs:
        rows = job.rows // n_steps
        assert rows * n_steps == job.rows and rows % 16 == 0
        for k in range(job.nb):
            in_specs.append(pl.BlockSpec(
                (None, rows, job.cbw), lambda *g, job=job, k=k: (job.layer, step_of(*g), job.k0 + k)))
            args.append(job.src)
        out_specs.append(pl.BlockSpec((rows, job.nb * job.cbw), lambda *g: (step_of(*g), 0)))
        out_shapes.append(jax.ShapeDtypeStruct((job.rows, job.nb * job.cbw), jnp.bfloat16))
    return in_specs, args, out_specs, out_shapes


def _cast_plan(jobs):
    return tuple((job.nb, job.cbw) for job in jobs)


def _run_casts(plan, in_refs, out_refs):
    in_refs = iter(in_refs)
    for (nb, cbw), out_ref in zip(plan, out_refs):
        for k in range(nb):
            out_ref[:, k * cbw:(k + 1) * cbw] = _bf(next(in_refs)[...])


def _n_cast_inputs(plan):
    return sum(nb for nb, _ in plan)


CONV_TM = LAT_LEN
CONV_TN = 256
CONV_EDGE = 16


def _conv_kernel(plan, h_ref, wb_ref, wc_ref, wx_ref, cw_ref, *refs):
    n_in = _n_cast_inputs(plan)
    o_ref = refs[n_in]
    _run_casts(plan, refs[:n_in], refs[n_in + 1:])
    i = pl.program_id(0)
    wb, wc, wx = _bf(wb_ref[...]), _bf(wc_ref[...]), _bf(wx_ref[...])
    cw = cw_ref[...]
    seq_mask = jnp.where(i < ROWS_CTX // CONV_TM, CTX_LEN - 1, LAT_LEN - 1)

    def project(r0, rows):
        h = h_ref[r0:r0 + rows]
        return _dot(h, wb), _dot(h, wc) * _dot(h, wx)

    def gated_conv(cb, z, row0):
        n = z.shape[0]
        pos = (row0 + lax.broadcasted_iota(jnp.int32, z.shape, 0)) & seq_mask
        z_prev = jnp.where(pos == 0, 0.0, pltpu.roll(z, 1, 0))
        z_next = jnp.where(pos == seq_mask, 0.0, pltpu.roll(z, n - 1, 0))
        return _bf(cb * (z_prev * cw[0:1, :] + z * cw[1:2, :] + z_next * cw[2:3, :]))

    half, edge = CONV_TM // 2, CONV_EDGE
    cb_a, z_a = project(0, half)
    cb_b, z_b = project(half, half)
    o_ref[0:half - edge] = gated_conv(cb_a, z_a, 0)[:half - edge]
    cb_w = jnp.concatenate([cb_a[half - 2 * edge:], cb_b], axis=0)
    z_w = jnp.concatenate([z_a[half - 2 * edge:], z_b], axis=0)
    o_ref[half - edge:] = gated_conv(cb_w, z_w, half - 2 * edge)[edge:]


def _conv_branch(h, w_in, conv_w, layer, jobs):
    nj = CONV_DIM // CONV_TN
    ni = ROWS // CONV_TM

    def wspec(group):
        return pl.BlockSpec((None, D, CONV_TN), lambda i, j: (layer, 0, group * nj + j))

    c_in, c_args, c_out, c_shapes = _cast_plumbing(jobs, ni * nj, lambda i, j: i * nj + j)
    out = pl.pallas_call(
        functools.partial(_conv_kernel, _cast_plan(jobs)),
        grid=(ni, nj),
        in_specs=[pl.BlockSpec((CONV_TM, D), lambda i, j: (i, 0)),
                  wspec(0), wspec(1), wspec(2),
                  pl.BlockSpec((None, 3, CONV_TN), lambda i, j: (layer, 0, j))] + c_in,
        out_specs=[pl.BlockSpec((CONV_TM, CONV_TN), lambda i, j: (i, j))] + c_out,
        out_shape=[jax.ShapeDtypeStruct((ROWS, CONV_DIM), jnp.bfloat16)] + c_shapes,
        compiler_params=_params(("arbitrary", "arbitrary")),
        name="conv_branch",
    )(h, w_in, w_in, w_in, conv_w, *c_args)
    return out[0], out[1:]


ATTN_TM = CTX_LEN
ATTN_CHUNK = 512
ROPE_WIN, ROPE_DIFF, ROPE_NONE = 0, 1, 2
_SLAB_KIND = ([ROPE_WIN] * (WIN_HEADS + WIN_KV) + [ROPE_NONE] * WIN_KV
              + [ROPE_DIFF] * (2 * DIFF_HEADS) + [ROPE_NONE] * DIFF_HEADS)
_ROPE_SHIFT = {ROPE_WIN: HEAD // 4, ROPE_DIFF: DIFF_QK // 4}
_STATE_COLS = ((A_WK, WIN_KV * HEAD), (A_WV, WIN_KV * HEAD),
               (A_DK, DIFF_HEADS * HEAD), (A_DV, DIFF_HEADS * HEAD))
_STATE_ROWS_BY_HEAD = (True, True, False, False)


def _state_block(idx):
    width = _STATE_COLS[idx][1]
    return (CTX_LEN * width // HEAD, HEAD) if _STATE_ROWS_BY_HEAD[idx] else (CTX_LEN, width)
DIFF_Q_SCALE = DIFF_QK ** -0.5 * math.log2(math.e)


def _rope_tables():
    t = np.arange(LAT_LEN, dtype=np.int32)
    rows = (t // GRID_W).astype(np.float32)[:, None]
    cols = (t % GRID_W).astype(np.float32)[:, None]
    lane = np.arange(HEAD, dtype=np.int32)[None, :]
    out = []
    for kind in (ROPE_WIN, ROPE_DIFF):
        s = _ROPE_SHIFT[kind]
        within = lane % (4 * s)
        freq = (within % s).astype(np.float32)
        inv = np.float32(ROPE_BASE) ** (-freq / np.float32(s))
        ang = (np.where(within < 2 * s, rows, cols) * inv).astype(np.float32)
        first = (within % (2 * s)) < s
        cos, sin = np.cos(ang), np.sin(ang)
        out += [cos, np.where(first, -sin, 0.0), np.where(first, 0.0, sin)]
    return jnp.asarray(np.stack(out).astype(np.float32))


def _attn_proj_kernel(first, layer, plan, h_ref, w_ref, tab_ref, sink_ref, lam_ref, g_ref, *refs):
    n_in = _n_cast_inputs(plan)
    n_st = len(_STATE_COLS)
    pos = (0 if first else n_st) + n_in
    cast_in = refs[pos - n_in:pos]
    qkv_ref = refs[pos]
    state_refs = refs[pos + 1:pos + 1 + n_st]
    yb_ref, yc_ref = refs[pos + 1 + n_st:pos + 3 + n_st]
    cast_out = refs[pos + 3 + n_st:-2]
    tiles = refs[-2:]
    i = pl.program_id(0)

    def project(ctx, dst_ref):
        h = h_ref[...]
        if ctx and first:
            for ref in state_refs:
                ref[1:] = jnp.zeros((N_LAYERS - 1,) + ref.shape[1:], jnp.float32)
        for c0 in range(0, ATTN_COLS, ATTN_CHUNK):
            acc = _dot(h, w_ref[:, c0:c0 + ATTN_CHUNK])
            for s0 in range(0, ATTN_CHUNK, HEAD):
                col = c0 + s0
                x = acc[:, s0:s0 + HEAD]
                kind = _SLAB_KIND[col // HEAD]
                if ctx:
                    for ref, (src, width), by_head in zip(state_refs, _STATE_COLS, _STATE_ROWS_BY_HEAD):
                        if src <= col < src + width:
                            dst = col - src
                            if by_head:
                                where = (pl.ds(dst // HEAD, CTX_LEN, stride=width // HEAD), slice(None))
                            else:
                                where = (slice(None), slice(dst, dst + HEAD))
                            if first:
                                ref[(0,) + where] = x
                            else:
                                ref[where] = x
                elif kind != ROPE_NONE:
                    s = _ROPE_SHIFT[kind]
                    x = (x * tab_ref[3 * kind] + pltpu.roll(x, HEAD - s, 1) * tab_ref[3 * kind + 1]
                         + pltpu.roll(x, s, 1) * tab_ref[3 * kind + 2])
                if A_DQ <= col < A_DK:
                    x = x * DIFF_Q_SCALE
                dst_ref[:, col:col + HEAD] = _bf(x)

    def step(ctx, parity, attend):
        def run():
            _run_casts(plan, cast_in, cast_out)
            if attend:
                _ctx_attention(layer, tiles[1 - parity], sink_ref, lam_ref, g_ref, yb_ref, yc_ref)
            project(ctx, tiles[parity] if ctx else qkv_ref)
        return run

    @pl.when(i == 0)
    def _():
        tiles[1][...] = jnp.zeros_like(tiles[1])

    is_ctx = i < N_CTX_SEQ
    for parity in (0, 1):
        pl.when(is_ctx & (i % 2 == parity))(step(True, parity, True))
    pl.when(i == N_CTX_SEQ)(step(False, N_CTX_SEQ % 2, True))
    pl.when(i > N_CTX_SEQ)(step(False, 0, False))


def _attn_proj(h, w_attn, tables, sink, lam_params, subln, layer, prev_states, jobs):
    tm = ATTN_TM
    per_lat = LAT_LEN // tm
    first = prev_states is None
    last_ctx = N_CTX_SEQ - 1

    def state_spec(idx):
        if first:
            return pl.BlockSpec((None, N_LAYERS) + _state_block(idx), lambda i: (jnp.minimum(i, last_ctx), 0, 0, 0))
        return pl.BlockSpec((None, None) + _state_block(idx), lambda i: (jnp.minimum(i, last_ctx), layer, 0, 0))

    in_specs = [pl.BlockSpec((tm, D), lambda i: (i, 0)),
                pl.BlockSpec((D, ATTN_COLS), lambda i: (0, 0), pipeline_mode=pl.Buffered(1)),
                pl.BlockSpec((6, tm, HEAD), lambda i: (0, jnp.maximum(i - N_CTX_SEQ, 0) % per_lat, 0)),
                pl.BlockSpec(memory_space=pltpu.SMEM),
                pl.BlockSpec((None, 4, DIFF_QK), lambda i: (layer, 0, 0)),
                _vec_spec(layer, HEAD)]
    args = [h, w_attn, tables, sink, lam_params, subln]
    aliases = {}
    if not first:
        for k, st in enumerate(prev_states):
            in_specs.append(pl.BlockSpec(memory_space=pl.ANY))
            aliases[len(args)] = 1 + k
            args.append(st)
    c_in, c_args, c_out, c_shapes = _cast_plumbing(jobs, ROWS // tm, lambda i: i)
    y_spec = pl.BlockSpec((CTX_LEN, WIN_HEADS * HEAD), lambda i: (jnp.clip(i - 1, 0, last_ctx), 0))
    y_shape = jax.ShapeDtypeStruct((ROWS_CTX, WIN_HEADS * HEAD), jnp.bfloat16)
    out = pl.pallas_call(
        functools.partial(_attn_proj_kernel, first, layer, _cast_plan(jobs)),
        grid=(ROWS // tm,),
        in_specs=in_specs + c_in,
        out_specs=[pl.BlockSpec((tm, ATTN_COLS), lambda i: (jnp.maximum(i - N_CTX_SEQ, 0), 0))]
                  + [state_spec(idx) for idx in range(len(_STATE_COLS))] + [y_spec, y_spec] + c_out,
        out_shape=[jax.ShapeDtypeStruct((ROWS_LAT, ATTN_COLS), jnp.bfloat16)]
                  + [jax.ShapeDtypeStruct((N_CTX_SEQ, N_LAYERS) + _state_block(idx), jnp.float32)
                     for idx in range(len(_STATE_COLS))] + [y_shape, y_shape] + c_shapes,
        scratch_shapes=[pltpu.VMEM((CTX_LEN, ATTN_COLS), jnp.bfloat16)] * 2,
        input_output_aliases=aliases,
        compiler_params=_params(("arbitrary",)),
        name="attn_proj",
    )(*args, *c_args)
    return out[0], out[1:5], out[5], out[6], out[7:]


def _slab(ref, col0, idx):
    return ref[:, col0 + idx * HEAD:col0 + (idx + 1) * HEAD]


def _with_ones(v):
    return jnp.concatenate([v, jnp.ones_like(v)], axis=1)


def _pipelined(items, scores, finish):
    s = scores(items[0])
    for r, item in enumerate(items):
        s_next = scores(items[r + 1]) if r + 1 < len(items) else None
        finish(item, s)
        s = s_next


WIN_STAGE = WIN_GROUP
WIN_STAGES = tuple(range(0, WIN_HEADS, WIN_STAGE))


def _win_scores(q_ref, col0, h0, k, bias):
    q = jnp.concatenate([_slab(q_ref, col0, h0 + hh) for hh in range(WIN_STAGE)], axis=0)
    s = _dot_nt(q, k) * (HEAD ** -0.5)
    return s if bias is None else s + bias


def _win_finish(layer, h0, s, v, sink_ref, o_ref):
    rows = s.shape[0] // WIN_STAGE
    sink = jnp.concatenate(
        [jnp.full((rows, 1), sink_ref[layer * WIN_HEADS + h0 + hh], jnp.float32)
         for hh in range(WIN_STAGE)], axis=0)
    m = jnp.maximum(jnp.max(s, axis=-1, keepdims=True), sink)
    r = _dot(_bf(jnp.exp(s - m)), _with_ones(v))
    o = r[:, :HEAD] / (r[:, HEAD:HEAD + 1] + jnp.exp(sink - m))
    for hh in range(WIN_STAGE):
        head = h0 + hh
        o_ref[:, head * HEAD:(head + 1) * HEAD] = _bf(o[hh * rows:(hh + 1) * rows])


def _win_band(n, kseq_ref, vseq_ref):
    start = pl.multiple_of(jnp.clip((n - 1) * QBLK, 0, LAT_LEN - 3 * QBLK), QBLK)
    kb = kseq_ref[pl.ds(start, 3 * QBLK), :]
    vb = vseq_ref[pl.ds(start, 3 * QBLK), :]
    shape = (QBLK, PAST + 3 * QBLK)
    col = lax.broadcasted_iota(jnp.int32, shape, 1)
    qpos = n * QBLK + lax.broadcasted_iota(jnp.int32, shape, 0)
    kpos = start + col - PAST
    bias = jnp.where((col < PAST) | (jnp.abs(qpos - kpos) <= WINDOW), 0.0, NEG)
    return kb, vb, jnp.concatenate([bias] * WIN_STAGE, axis=0)


def _win_keys(cache_ref, band, h0):
    g = h0 // WIN_GROUP
    cached = cache_ref[pl.ds(g, PAST, stride=WIN_KV), :]
    return jnp.concatenate([_bf(cached), band[:, g * HEAD:(g + 1) * HEAD]], axis=0)


def _lam_init(layer):
    return 0.8 - 0.6 * math.exp(-0.3 * layer)


def _diff_lambda(layer, lam_ref):
    lp = lam_ref[...]
    return (jnp.exp(jnp.sum(lp[0:1] * lp[1:2], axis=-1, keepdims=True))
            - jnp.exp(jnp.sum(lp[2:3] * lp[3:4], axis=-1, keepdims=True)) + _lam_init(layer))


def _diff_scores(q, k):
    first = lax.broadcasted_iota(jnp.int32, q.shape, 1) < DIFF_QK
    zero = jnp.zeros_like(q)
    return _dot_nt(jnp.concatenate([jnp.where(first, q, zero), jnp.where(first, zero, q)], axis=0), k)


def _diff_finish(layer, s, v_ones, lam, g):
    tq = s.shape[0] // 2
    e = _bf(jnp.exp2(s - jnp.max(s, axis=-1, keepdims=True)))
    r = _dot(e, v_ones)
    o = (r[:tq, :HEAD] * (1.0 / r[:tq, HEAD:HEAD + 1])
         - r[tq:, :HEAD] * (lam / r[tq:, HEAD:HEAD + 1]))
    o = o * lax.rsqrt(jnp.mean(o * o, axis=-1, keepdims=True) + EPS) * g
    return _bf(o * (1.0 - _lam_init(layer)))


def _ctx_attention(layer, qkv_ref, sink_ref, lam_ref, g_ref, yb_ref, yc_ref):
    lam = _diff_lambda(layer, lam_ref)
    stages = [("win", h0) for h0 in WIN_STAGES] + [("diff", h) for h in range(DIFF_HEADS)]

    def scores(stage):
        kind, idx = stage
        if kind == "win":
            return _win_scores(qkv_ref, A_WQ, idx, _slab(qkv_ref, A_WK, idx // WIN_GROUP), None)
        return _diff_scores(_slab(qkv_ref, A_DQ, idx), _slab(qkv_ref, A_DK, idx))

    def finish(stage, s):
        kind, idx = stage
        if kind == "win":
            _win_finish(layer, idx, s, _slab(qkv_ref, A_WV, idx // WIN_GROUP), sink_ref, yb_ref)
        else:
            yc_ref[:, idx * HEAD:(idx + 1) * HEAD] = _diff_finish(
                layer, s, _with_ones(_slab(qkv_ref, A_DV, idx)), lam, g_ref[...])

    _pipelined(stages, scores, finish)


LAT_DIFF_SUB = 128
LAT_WIN_BLOCKS = LAT_LEN // QBLK // DIFF_HEADS


def _lat_attn_kernel(layer, plan, dq_ref, dkc_ref, dvc_ref, dk_ref, dv_ref, lam_ref, g_ref,
                     wq_ref, wkc_ref, wvc_ref, wk_ref, wv_ref, sink_ref, *refs):
    n_in = _n_cast_inputs(plan)
    yc_ref, yb_ref = refs[n_in:n_in + 2]
    kall_ref, vall_ref = refs[-2:]
    _run_casts(plan, refs[:n_in], refs[n_in + 2:-2])
    h = pl.program_id(1)

    head_rows = pl.ds(h, PAST, stride=DIFF_HEADS)
    kall_ref[0:PAST] = _bf(dkc_ref[head_rows, :])
    kall_ref[PAST:] = dk_ref[...]
    vall_ref[0:PAST] = _with_ones(_bf(dvc_ref[head_rows, :]))
    vall_ref[PAST:] = _with_ones(dv_ref[...])
    lam = _diff_lambda(layer, lam_ref)
    bands = [_win_band(h * LAT_WIN_BLOCKS + blk, wk_ref, wv_ref) for blk in range(LAT_WIN_BLOCKS)]

    diff_stages = [("diff", r0, None) for r0 in range(0, LAT_LEN, LAT_DIFF_SUB)]
    win_stages = [("win", blk, h0) for blk in range(LAT_WIN_BLOCKS) for h0 in WIN_STAGES]
    every = len(diff_stages) // len(win_stages)
    stages = []
    for k, stage in enumerate(diff_stages):
        stages.append(stage)
        if k % every == 0 and k // every < len(win_stages):
            stages.append(win_stages[k // every])

    def rows(ref, blk):
        return ref.at[pl.ds(blk * QBLK, QBLK)]

    def scores(stage):
        kind, a, h0 = stage
        if kind == "diff":
            return _diff_scores(dq_ref[a:a + LAT_DIFF_SUB], kall_ref[...])
        kb, _, bias = bands[a]
        return _win_scores(rows(wq_ref, a), 0, h0, _win_keys(wkc_ref, kb, h0), bias)

    def finish(stage, s):
        kind, a, h0 = stage
        if kind == "diff":
            yc_ref[a:a + LAT_DIFF_SUB] = _diff_finish(layer, s, vall_ref[...], lam, g_ref[...])
        else:
            _win_finish(layer, h0, s, _win_keys(wvc_ref, bands[a][1], h0), sink_ref, rows(yb_ref, a))

    _pipelined(stages, scores, finish)


def _lat_attn(qkv, win_ck, win_cv, diff_ck, diff_cv, sink, lam_params, subln, layer, jobs):
    kvw = WIN_KV * HEAD
    wrows = LAT_WIN_BLOCKS * QBLK

    def step(b, h):
        return b * DIFF_HEADS + h

    def cache_spec(heads):
        return pl.BlockSpec((None, None, PAST * heads, HEAD), lambda b, h: (b, layer, 0, 0))

    def head_spec(col0):
        return pl.BlockSpec((LAT_LEN, HEAD), lambda b, h: (b, col0 // HEAD + h))

    def seq_spec(col0):
        return pl.BlockSpec((LAT_LEN, kvw), lambda b, h: (b, col0 // kvw))

    c_in, c_args, c_out, c_shapes = _cast_plumbing(jobs, N_LAT_SEQ * DIFF_HEADS, step)
    out = pl.pallas_call(
        functools.partial(_lat_attn_kernel, layer, _cast_plan(jobs)),
        grid=(N_LAT_SEQ, DIFF_HEADS),
        in_specs=[head_spec(A_DQ), cache_spec(DIFF_HEADS), cache_spec(DIFF_HEADS),
                  head_spec(A_DK), head_spec(A_DV),
                  pl.BlockSpec((None, 4, DIFF_QK), lambda b, h: (layer, 0, 0)),
                  _vec_spec(layer, HEAD),
                  pl.BlockSpec((wrows, WIN_HEADS * HEAD), lambda b, h: (step(b, h), 0)),
                  cache_spec(WIN_KV), cache_spec(WIN_KV), seq_spec(A_WK), seq_spec(A_WV),
                  pl.BlockSpec(memory_space=pltpu.SMEM)] + c_in,
        out_specs=[pl.BlockSpec((LAT_LEN, HEAD), lambda b, h: (b, h)),
                   pl.BlockSpec((wrows, WIN_HEADS * HEAD), lambda b, h: (step(b, h), 0))] + c_out,
        out_shape=[jax.ShapeDtypeStruct((ROWS_LAT, DIFF_HEADS * HEAD), jnp.bfloat16),
                   jax.ShapeDtypeStruct((ROWS_LAT, WIN_HEADS * HEAD), jnp.bfloat16)] + c_shapes,
        scratch_shapes=[pltpu.VMEM((PAST + LAT_LEN, HEAD), jnp.bfloat16),
                        pltpu.VMEM((PAST + LAT_LEN, 2 * HEAD), jnp.bfloat16)],
        compiler_params=_params(("arbitrary", "arbitrary")),
        name="lat_attn",
    )(qkv, diff_ck, diff_cv, qkv, qkv, lam_params, subln, qkv, win_ck, win_cv, qkv, qkv, sink, *c_args)
    return out[1], out[0], out[2:]


MERGE_TM = 1024
MERGE_TN = 512


def _merge_kernel(h_ref, ya_ref, ybc_ref, ybl_ref, ycc_ref, ycl_ref,
                  wga_ref, wgb_ref, wgc_ref, wa_ref, wb_ref, wc_ref, o_ref):
    h = h_ref[...]
    yb = _pick_rows(MERGE_TM, ybc_ref, ybl_ref)
    yc = _pick_rows(MERGE_TM, ycc_ref, ycl_ref)
    m = _sigmoid(_dot(h, wga_ref[...])) * _dot(ya_ref[...], wa_ref[...])
    m += _sigmoid(_dot(h, wgb_ref[...])) * _dot(yb, wb_ref[...])
    m += _sigmoid(_dot(h, wgc_ref[...])) * _dot(yc, wc_ref[...])
    o_ref[...] = _bf(m)


def _merge(h, ya, yb_ctx, yb_lat, yc_ctx, yc_lat, w_gate, wa, wb, wc):
    tm, tn = MERGE_TM, MERGE_TN
    nj = D // tn

    def gate_spec(branch):
        return pl.BlockSpec((D, tn), lambda i, j: (0, branch * nj + j))

    w_spec = pl.BlockSpec((CONV_DIM, tn), lambda i, j: (0, j))
    return pl.pallas_call(
        _merge_kernel,
        grid=(ROWS // tm, nj),
        in_specs=[pl.BlockSpec((tm, D), lambda i, j: (i, 0)),
                  pl.BlockSpec((tm, CONV_DIM), lambda i, j: (i, 0)),
                  *_split_specs(tm, WIN_HEADS * HEAD), *_split_specs(tm, DIFF_HEADS * HEAD),
                  gate_spec(0), gate_spec(1), gate_spec(2), w_spec, w_spec, w_spec],
        out_specs=pl.BlockSpec((tm, tn), lambda i, j: (i, j)),
        out_shape=jax.ShapeDtypeStruct((ROWS, D), jnp.bfloat16),
        compiler_params=_params(("parallel", "arbitrary")),
        name="merge",
    )(h, ya, yb_ctx, yb_lat, yc_ctx, yc_lat, w_gate, w_gate, w_gate, wa, wb, wc)


OUT_TM = 512


def _out_proj_kernel(n_x, *refs):
    x_refs = refs[:n_x]
    m_ref, w_ref, gate_ref, g_ref, shift_ref, scale_ref, xo_ref, h_ref = refs[n_x:]
    x = x_refs[0][...] if n_x == 1 else _pick_rows(OUT_TM, *x_refs)
    x = x + gate_ref[...] * _dot(m_ref[...], w_ref[...])
    xo_ref[...] = x
    h_ref[...] = _bf(_rms_mod(x, g_ref[...], scale_ref[...], shift_ref[...]))


def _out_proj(xs, merged, w_out, mods, norm_mlp, layer):
    tm = OUT_TM
    row = pl.BlockSpec((tm, D), lambda i: (i, 0))
    x_specs = [row] if len(xs) == 1 else list(_split_specs(tm, D))
    return pl.pallas_call(
        functools.partial(_out_proj_kernel, len(xs)),
        grid=(ROWS // tm,),
        in_specs=x_specs + [row,
                            pl.BlockSpec((D, D), lambda i: (0, 0)),
                            _mod_spec(layer, 2, tm), _vec_spec(layer, D),
                            _mod_spec(layer, 3, tm), _mod_spec(layer, 4, tm)],
        out_specs=[row, row],
        out_shape=[jax.ShapeDtypeStruct((ROWS, D), jnp.float32),
                   jax.ShapeDtypeStruct((ROWS, D), jnp.bfloat16)],
        compiler_params=_params(("arbitrary",)),
        name="out_proj",
    )(*xs, merged, w_out, mods, norm_mlp.reshape(N_LAYERS, 1, D), mods, mods)


MLP_TM = 512
MLP_TF = 1024


MLP_TILES = ROWS // MLP_TM


def _mlp_kernel(final, h_ref, w1_ref, w2_ref, x_ref, gate_ref, g_ref, shift_ref, scale_ref,
                *rest):
    outs, accs = rest[:-2], rest[-2:]
    i = pl.program_id(0)
    j = pl.program_id(1)

    def matmuls(acc_ref, first_chunk):
        t = jnp.maximum(_dot(h_ref[...], w1_ref[...]), 0.0)
        upd = _dot(_bf(t * t), w2_ref[...])
        acc_ref[...] = upd if first_chunk else acc_ref[...] + upd

    def epilogue(acc_ref):
        x = x_ref[...] + gate_ref[...] * acc_ref[...]
        if final:
            yc_ref, yl_ref = outs
            y = x * lax.rsqrt(jnp.mean(x * x, axis=-1, keepdims=True) + EPS) * g_ref[...]

            @pl.when(i - 1 < ROWS_CTX // MLP_TM)
            def _():
                yc_ref[...] = y

            @pl.when(i - 1 >= ROWS_CTX // MLP_TM)
            def _():
                yl_ref[...] = y
        else:
            xo_ref, hn_ref = outs
            xo_ref[...] = x
            hn_ref[...] = _bf(_rms_mod(x, g_ref[...], scale_ref[...], shift_ref[...]))

    def step(parity, first_chunk, finish_previous):
        def run():
            matmuls(accs[parity], first_chunk)
            if finish_previous:
                epilogue(accs[1 - parity])
        return run

    has_tile = i < MLP_TILES
    pl.when((i == 0) & (j == 0))(step(0, True, False))
    for parity in (0, 1):
        mine = has_tile & (i % 2 == parity)
        pl.when(mine & (i > 0) & (j == 0))(step(parity, True, True))
        pl.when(mine & (j > 0))(step(parity, False, False))
    pl.when((i == MLP_TILES) & (j == 0))(lambda: epilogue(accs[(MLP_TILES - 1) % 2]))


def _mlp(h2, x, w1, w2, mods, norm_next, layer, final):
    tm, tf = MLP_TM, MLP_TF
    last = MLP_TILES - 1
    n_ctx = ROWS_CTX // tm

    def done(i):
        return jnp.clip(i - 1, 0, last)

    def chunk(i, j):
        return jnp.where(i == MLP_TILES, 0, j)

    lagged = pl.BlockSpec((tm, D), lambda i, j: (done(i), 0))
    nxt = layer if final else layer + 1
    g_spec = pl.BlockSpec((None, 1, D), lambda i, j: (0 if final else nxt, 0, 0))
    if final:
        out_specs = [pl.BlockSpec((tm, D), lambda i, j: (jnp.minimum(done(i), n_ctx - 1), 0)),
                     pl.BlockSpec((tm, D), lambda i, j: (jnp.maximum(done(i) - n_ctx, 0), 0))]
        out_shape = [jax.ShapeDtypeStruct((ROWS_CTX, D), jnp.float32),
                     jax.ShapeDtypeStruct((ROWS_LAT, D), jnp.float32)]
    else:
        out_specs = [lagged, lagged]
        out_shape = [jax.ShapeDtypeStruct((ROWS, D), jnp.float32),
                     jax.ShapeDtypeStruct((ROWS, D), jnp.bfloat16)]
    return pl.pallas_call(
        functools.partial(_mlp_kernel, final),
        grid=(MLP_TILES + 1, D_FF // tf),
        in_specs=[pl.BlockSpec((tm, D), lambda i, j: (jnp.minimum(i, last), 0)),
                  pl.BlockSpec((D, tf), lambda i, j: (0, chunk(i, j))),
                  pl.BlockSpec((tf, D), lambda i, j: (chunk(i, j), 0)),
                  lagged, _mod_spec(layer, 5, tm, lag=1), g_spec,
                  _mod_spec(nxt, 0, tm, lag=1), _mod_spec(nxt, 1, tm, lag=1)],
        out_specs=out_specs,
        out_shape=out_shape,
        scratch_shapes=[pltpu.VMEM((tm, D), jnp.float32)] * 2,
        compiler_params=_params(("arbitrary", "arbitrary")),
        name="mlp_final" if final else "mlp",
    )(h2, w1, w2, x, mods, norm_next, mods, mods)


def kernel(x_prompt, x_sample, cache_win_k, cache_win_v, cache_diff_k, cache_diff_v, c, c_ctx, w_ada, b_ada, norm_mix, norm_mlp, w_in, conv_w, win_sink, lambda_q1, lambda_k1, lambda_q2, lambda_k2, diff_subln, w_branch_conv, w_branch_win, w_branch_diff, w_out, w_mlp1, w_mlp2, norm_final):
    xs = (x_prompt.reshape(ROWS_CTX, D), x_sample.reshape(ROWS_LAT, D))
    cvec = jnp.concatenate([c_ctx[None], c, jnp.zeros((N_MOD_ROWS - 1 - N_LAT_SEQ, D), jnp.float32)], axis=0)
    mods = _modulation(cvec, w_ada, b_ada).reshape(N_LAYERS * N_MOD_ROWS * N_MOD, 1, D)

    tables = _rope_tables()
    sink = win_sink.reshape(N_LAYERS * WIN_HEADS)
    lam_params = jnp.stack([lambda_q1, lambda_k1, lambda_q2, lambda_k2], axis=1)
    subln = diff_subln.reshape(N_LAYERS, 1, HEAD)
    ck_win = cache_win_k.reshape(N_LAT_SEQ, N_LAYERS, PAST * WIN_KV, HEAD)
    cv_win = cache_win_v.reshape(N_LAT_SEQ, N_LAYERS, PAST * WIN_KV, HEAD)
    ck_diff = cache_diff_k.reshape(N_LAT_SEQ, N_LAYERS, PAST * DIFF_HEADS, HEAD)
    cv_diff = cache_diff_v.reshape(N_LAT_SEQ, N_LAYERS, PAST * DIFF_HEADS, HEAD)
    norm_final_3d = norm_final.reshape(1, 1, D)
    norm_mix_3d = norm_mix.reshape(N_LAYERS, 1, D)

    h = _prologue(*xs, mods, norm_mix, 0)
    states = None
    for layer in range(N_LAYERS):
        in_job = functools.partial(_CastJob, w_in, layer, cbw=IN_CBW)
        ya, (w_attn, wa, wb, wc, w_o) = _conv_branch(
            h, w_in, conv_w, layer,
            [in_job(k0=OFF_ATTN // IN_CBW, nb=ATTN_COLS // IN_CBW)]
            + [_whole(w, layer) for w in (w_branch_conv, w_branch_win, w_branch_diff, w_out)])
        qkv, states, yb_ctx, yc_ctx, (w_gate,) = _attn_proj(
            h, w_attn, tables, sink, lam_params, subln, layer, states,
            [in_job(k0=OFF_GATE // IN_CBW, nb=3 * D // IN_CBW)])
        yb_lat, yc_lat, (w1, w2) = _lat_attn(qkv, ck_win, cv_win, ck_diff, cv_diff, sink, lam_params, subln,
                                             layer, [_whole(w_mlp1, layer), _whole(w_mlp2, layer)])
        merged = _merge(h, ya, yb_ctx, yb_lat, yc_ctx, yc_lat, w_gate, wa, wb, wc)
        x, h2 = _out_proj(xs, merged, w_o, mods, norm_mlp, layer)
        if layer == N_LAYERS - 1:
            y_ctx, y_lat = _mlp(h2, x, w1, w2, mods, norm_final_3d, layer, True)
        else:
            x, h = _mlp(h2, x, w1, w2, mods, norm_mix_3d, layer, False)
            xs = (x,)

    new_win_k, new_win_v, new_diff_k, new_diff_v = states
    return (y_ctx.reshape(N_CTX_SEQ, CTX_LEN, D), y_lat.reshape(N_LAT_SEQ, LAT_LEN, D),
            new_win_k.reshape(N_CTX_SEQ, N_LAYERS, CTX_LEN, WIN_KV, HEAD),
            new_win_v.reshape(N_CTX_SEQ, N_LAYERS, CTX_LEN, WIN_KV, HEAD),
            new_diff_k.reshape(N_CTX_SEQ, N_LAYERS, CTX_LEN, DIFF_HEADS, HEAD),
            new_diff_v.reshape(N_CTX_SEQ, N_LAYERS, CTX_LEN, DIFF_HEADS, HEAD))
```

```python
import functools
import math
from typing import NamedTuple

import jax
import jax.numpy as jnp
import numpy as np
from jax import lax
from jax.experimental import pallas as pl
from jax.experimental.pallas import tpu as pltpu

D = 2048
N_CTX_SEQ = 16
CTX_LEN = 256
N_LAT_SEQ = 2
LAT_LEN = 2048
N_LAYERS = 2
PAST = 512
GRID_W = 64
HEAD = 128
CONV_DIM = 1024
WIN_HEADS = 8
WIN_KV = 2
WIN_GROUP = WIN_HEADS // WIN_KV
WINDOW = 128
QBLK = 128
DIFF_HEADS = 8
DIFF_QK = 64
D_FF = 4 * D
EPS = 1e-6
ROPE_BASE = 10000.0
NEG = -1e30

ROWS_CTX = N_CTX_SEQ * CTX_LEN
ROWS_LAT = N_LAT_SEQ * LAT_LEN
ROWS = ROWS_CTX + ROWS_LAT
N_MOD_ROWS = 8
N_MOD = 6

OFF_ATTN = 3 * CONV_DIM
ATTN_COLS = (WIN_HEADS + 2 * WIN_KV) * HEAD + 3 * DIFF_HEADS * HEAD
OFF_GATE = OFF_ATTN + ATTN_COLS
IN_CBW = 1536
A_WQ, A_WK, A_WV = 0, WIN_HEADS * HEAD, (WIN_HEADS + WIN_KV) * HEAD
A_DQ = (WIN_HEADS + 2 * WIN_KV) * HEAD
A_DK = A_DQ + DIFF_HEADS * HEAD
A_DV = A_DK + DIFF_HEADS * HEAD

VMEM_LIMIT_V7X = 60 * 1024 * 1024


def _params(sem, vmem=VMEM_LIMIT_V7X):
    return pltpu.CompilerParams(dimension_semantics=sem, vmem_limit_bytes=vmem)


def _bf(x):
    return x.astype(jnp.bfloat16)


def _dot(a, b):
    return jnp.dot(a, b, preferred_element_type=jnp.float32)


def _dot_nt(a, b):
    return lax.dot_general(a, b, (((1,), (1,)), ((), ())), preferred_element_type=jnp.float32)


def _sigmoid(x):
    return 1.0 / (1.0 + jnp.exp(-x))


def _mod_group(i, tm):
    n_ctx = ROWS_CTX // tm
    return jnp.where(i < n_ctx, 0, 1 + (i - n_ctx) // (LAT_LEN // tm))


def _mod_spec(layer, which, tm, lag=0):
    def imap(i, *_):
        group = _mod_group(jnp.maximum(i - lag, 0), tm)
        return (layer * N_MOD_ROWS * N_MOD + group * N_MOD + which, 0, 0)
    return pl.BlockSpec((None, 1, D), imap)


def _vec_spec(layer, width):
    return pl.BlockSpec((None, 1, width), lambda *_: (layer, 0, 0))


def _rms_mod(x, g, scale, shift):
    y = x * lax.rsqrt(jnp.mean(x * x, axis=-1, keepdims=True) + EPS) * g
    return y * (1.0 + scale) + shift


def _mod_kernel(c_ref, w_ref, b_ref, o_ref):
    c = c_ref[...]
    s = c * _sigmoid(c)
    o_ref[...] = _dot(_bf(s), _bf(w_ref[...])) + b_ref[...]


def _modulation(cvec, w_ada, b_ada):
    tn = 1024
    n = N_MOD * D
    return pl.pallas_call(
        _mod_kernel,
        grid=(N_LAYERS, n // tn),
        in_specs=[pl.BlockSpec((N_MOD_ROWS, D), lambda l, j: (0, 0)),
                  pl.BlockSpec((None, D, tn), lambda l, j: (l, 0, j)),
                  pl.BlockSpec((None, 1, tn), lambda l, j: (l, 0, j))],
        out_specs=pl.BlockSpec((None, N_MOD_ROWS, tn), lambda l, j: (l, 0, j)),
        out_shape=jax.ShapeDtypeStruct((N_LAYERS, N_MOD_ROWS, n), jnp.float32),
        compiler_params=_params(("parallel", "parallel")),
        name="modulation",
    )(cvec, w_ada, b_ada.reshape(N_LAYERS, 1, n))


def _split_specs(tm, width):
    n_ctx = ROWS_CTX // tm
    return (pl.BlockSpec((tm, width), lambda i, *_: (jnp.minimum(i, n_ctx - 1), 0)),
            pl.BlockSpec((tm, width), lambda i, *_: (jnp.maximum(i - n_ctx, 0), 0)))


def _pick_rows(tm, ctx_ref, lat_ref):
    return jnp.where(pl.program_id(0) < ROWS_CTX // tm, ctx_ref[...], lat_ref[...])


PRO_TM = 1024


def _prologue_kernel(xc_ref, xl_ref, g_ref, shift_ref, scale_ref, h_ref):
    x = _pick_rows(PRO_TM, xc_ref, xl_ref)
    h_ref[...] = _bf(_rms_mod(x, g_ref[...], scale_ref[...], shift_ref[...]))


def _prologue(x_ctx, x_lat, mods, norm_mix, layer):
    tm = PRO_TM
    return pl.pallas_call(
        _prologue_kernel,
        grid=(ROWS // tm,),
        in_specs=[*_split_specs(tm, D),
                  _vec_spec(layer, D),
                  _mod_spec(layer, 0, tm),
                  _mod_spec(layer, 1, tm)],
        out_specs=pl.BlockSpec((tm, D), lambda i: (i, 0)),
        out_shape=jax.ShapeDtypeStruct((ROWS, D), jnp.bfloat16),
        compiler_params=_params(("arbitrary",)),
        name="prologue",
    )(x_ctx, x_lat, norm_mix.reshape(N_LAYERS, 1, D), mods, mods)


class _CastJob(NamedTuple):
    src: jax.Array
    layer: int
    k0: int
    nb: int
    cbw: int

    @property
    def rows(self):
        return self.src.shape[1]


def _whole(src, layer):
    return _CastJob(src, layer, 0, 1, src.shape[2])


def _cast_plumbing(jobs, n_steps, step_of):
    in_specs, args, out_specs, out_shapes = [], [], [], []
    for job in jobs:
        rows = job.rows // n_steps
        assert rows * n_steps == job.rows and rows % 16 == 0
        for k in range(job.nb):
            in_specs.append(pl.BlockSpec(
                (None, rows, job.cbw), lambda *g, job=job, k=k: (job.layer, step_of(*g), job.k0 + k)))
            args.append(job.src)
        out_specs.append(pl.BlockSpec((rows, job.nb * job.cbw), lambda *g: (step_of(*g), 0)))
        out_shapes.append(jax.ShapeDtypeStruct((job.rows, job.nb * job.cbw), jnp.bfloat16))
    return in_specs, args, out_specs, out_shapes


def _cast_plan(jobs):
    return tuple((job.nb, job.cbw) for job in jobs)


def _run_casts(plan, in_refs, out_refs):
    in_refs = iter(in_refs)
    for (nb, cbw), out_ref in zip(plan, out_refs):
        for k in range(nb):
            out_ref[:, k * cbw:(k + 1) * cbw] = _bf(next(in_refs)[...])


def _n_cast_inputs(plan):
    return sum(nb for nb, _ in plan)


CONV_TM = LAT_LEN
CONV_TN = 256


def _conv_kernel(plan, h_ref, wb_ref, wc_ref, wx_ref, cw_ref, *refs):
    n_in = _n_cast_inputs(plan)
    o_ref = refs[n_in]
    _run_casts(plan, refs[:n_in], refs[n_in + 1:])
    i = pl.program_id(0)
    h = h_ref[...]
    cb = _dot(h, _bf(wb_ref[...]))
    z = _dot(h, _bf(wc_ref[...])) * _dot(h, _bf(wx_ref[...]))
    seq_mask = jnp.where(i < ROWS_CTX // CONV_TM, CTX_LEN - 1, LAT_LEN - 1)
    pos = lax.broadcasted_iota(jnp.int32, z.shape, 0) & seq_mask
    z_prev = jnp.where(pos == 0, 0.0, pltpu.roll(z, 1, 0))
    z_next = jnp.where(pos == seq_mask, 0.0, pltpu.roll(z, CONV_TM - 1, 0))
    cw = cw_ref[...]
    conv = z_prev * cw[0:1, :] + z * cw[1:2, :] + z_next * cw[2:3, :]
    o_ref[...] = _bf(cb * conv)


def _conv_branch(h, w_in, conv_w, layer, jobs):
    nj = CONV_DIM // CONV_TN
    ni = ROWS // CONV_TM

    def wspec(group):
        return pl.BlockSpec((None, D, CONV_TN), lambda i, j: (layer, 0, group * nj + j))

    c_in, c_args, c_out, c_shapes = _cast_plumbing(jobs, ni * nj, lambda i, j: i * nj + j)
    out = pl.pallas_call(
        functools.partial(_conv_kernel, _cast_plan(jobs)),
        grid=(ni, nj),
        in_specs=[pl.BlockSpec((CONV_TM, D), lambda i, j: (i, 0)),
                  wspec(0), wspec(1), wspec(2),
                  pl.BlockSpec((None, 3, CONV_TN), lambda i, j: (layer, 0, j))] + c_in,
        out_specs=[pl.BlockSpec((CONV_TM, CONV_TN), lambda i, j: (i, j))] + c_out,
        out_shape=[jax.ShapeDtypeStruct((ROWS, CONV_DIM), jnp.bfloat16)] + c_shapes,
        compiler_params=_params(("arbitrary", "arbitrary")),
        name="conv_branch",
    )(h, w_in, w_in, w_in, conv_w, *c_args)
    return out[0], out[1:]


ATTN_TM = CTX_LEN
ATTN_CHUNK = 512
ROPE_WIN, ROPE_DIFF, ROPE_NONE = 0, 1, 2
_SLAB_KIND = ([ROPE_WIN] * (WIN_HEADS + WIN_KV) + [ROPE_NONE] * WIN_KV
              + [ROPE_DIFF] * (2 * DIFF_HEADS) + [ROPE_NONE] * DIFF_HEADS)
_ROPE_SHIFT = {ROPE_WIN: HEAD // 4, ROPE_DIFF: DIFF_QK // 4}
_STATE_COLS = ((A_WK, WIN_KV * HEAD), (A_WV, WIN_KV * HEAD),
               (A_DK, DIFF_HEADS * HEAD), (A_DV, DIFF_HEADS * HEAD))
_STATE_ROWS_BY_HEAD = (True, True, False, False)


def _state_block(idx):
    width = _STATE_COLS[idx][1]
    return (CTX_LEN * width // HEAD, HEAD) if _STATE_ROWS_BY_HEAD[idx] else (CTX_LEN, width)
DIFF_Q_SCALE = DIFF_QK ** -0.5 * math.log2(math.e)


def _rope_tables():
    t = np.arange(LAT_LEN, dtype=np.int32)
    rows = (t // GRID_W).astype(np.float32)[:, None]
    cols = (t % GRID_W).astype(np.float32)[:, None]
    lane = np.arange(HEAD, dtype=np.int32)[None, :]
    out = []
    for kind in (ROPE_WIN, ROPE_DIFF):
        s = _ROPE_SHIFT[kind]
        within = lane % (4 * s)
        freq = (within % s).astype(np.float32)
        inv = np.float32(ROPE_BASE) ** (-freq / np.float32(s))
        ang = (np.where(within < 2 * s, rows, cols) * inv).astype(np.float32)
        first = (within % (2 * s)) < s
        cos, sin = np.cos(ang), np.sin(ang)
        out += [cos, np.where(first, -sin, 0.0), np.where(first, 0.0, sin)]
    return jnp.asarray(np.stack(out).astype(np.float32))


def _attn_proj_kernel(first, layer, plan, h_ref, w_ref, tab_ref, sink_ref, lam_ref, g_ref, *refs):
    n_in = _n_cast_inputs(plan)
    n_st = len(_STATE_COLS)
    pos = (0 if first else n_st) + n_in
    cast_in = refs[pos - n_in:pos]
    qkv_ref = refs[pos]
    state_refs = refs[pos + 1:pos + 1 + n_st]
    yb_ref, yc_ref = refs[pos + 1 + n_st:pos + 3 + n_st]
    cast_out = refs[pos + 3 + n_st:-2]
    tiles = refs[-2:]
    i = pl.program_id(0)

    def project(ctx, dst_ref):
        h = h_ref[...]
        if ctx and first:
            for ref in state_refs:
                ref[1:] = jnp.zeros((N_LAYERS - 1,) + ref.shape[1:], jnp.float32)
        for c0 in range(0, ATTN_COLS, ATTN_CHUNK):
            acc = _dot(h, w_ref[:, c0:c0 + ATTN_CHUNK])
            for s0 in range(0, ATTN_CHUNK, HEAD):
                col = c0 + s0
                x = acc[:, s0:s0 + HEAD]
                kind = _SLAB_KIND[col // HEAD]
                if ctx:
                    for ref, (src, width), by_head in zip(state_refs, _STATE_COLS, _STATE_ROWS_BY_HEAD):
                        if src <= col < src + width:
                            dst = col - src
                            if by_head:
                                where = (pl.ds(dst // HEAD, CTX_LEN, stride=width // HEAD), slice(None))
                            else:
                                where = (slice(None), slice(dst, dst + HEAD))
                            if first:
                                ref[(0,) + where] = x
                            else:
                                ref[where] = x
                elif kind != ROPE_NONE:
                    s = _ROPE_SHIFT[kind]
                    x = (x * tab_ref[3 * kind] + pltpu.roll(x, HEAD - s, 1) * tab_ref[3 * kind + 1]
                         + pltpu.roll(x, s, 1) * tab_ref[3 * kind + 2])
                if A_DQ <= col < A_DK:
                    x = x * DIFF_Q_SCALE
                dst_ref[:, col:col + HEAD] = _bf(x)

    def step(ctx, parity, attend):
        def run():
            _run_casts(plan, cast_in, cast_out)
            if attend:
                _ctx_attention(layer, tiles[1 - parity], sink_ref, lam_ref, g_ref, yb_ref, yc_ref)
            project(ctx, tiles[parity] if ctx else qkv_ref)
        return run

    @pl.when(i == 0)
    def _():
        tiles[1][...] = jnp.zeros_like(tiles[1])

    is_ctx = i < N_CTX_SEQ
    for parity in (0, 1):
        pl.when(is_ctx & (i % 2 == parity))(step(True, parity, True))
    pl.when(i == N_CTX_SEQ)(step(False, N_CTX_SEQ % 2, True))
    pl.when(i > N_CTX_SEQ)(step(False, 0, False))


def _attn_proj(h, w_attn, tables, sink, lam_params, subln, layer, prev_states, jobs):
    tm = ATTN_TM
    per_lat = LAT_LEN // tm
    first = prev_states is None
    last_ctx = N_CTX_SEQ - 1

    def state_spec(idx):
        if first:
            return pl.BlockSpec((None, N_LAYERS) + _state_block(idx), lambda i: (jnp.minimum(i, last_ctx), 0, 0, 0))
        return pl.BlockSpec((None, None) + _state_block(idx), lambda i: (jnp.minimum(i, last_ctx), layer, 0, 0))

    in_specs = [pl.BlockSpec((tm, D), lambda i: (i, 0)),
                pl.BlockSpec((D, ATTN_COLS), lambda i: (0, 0), pipeline_mode=pl.Buffered(1)),
                pl.BlockSpec((6, tm, HEAD), lambda i: (0, jnp.maximum(i - N_CTX_SEQ, 0) % per_lat, 0)),
                pl.BlockSpec(memory_space=pltpu.SMEM),
                pl.BlockSpec((None, 4, DIFF_QK), lambda i: (layer, 0, 0)),
                _vec_spec(layer, HEAD)]
    args = [h, w_attn, tables, sink, lam_params, subln]
    aliases = {}
    if not first:
        for k, st in enumerate(prev_states):
            in_specs.append(pl.BlockSpec(memory_space=pl.ANY))
            aliases[len(args)] = 1 + k
            args.append(st)
    c_in, c_args, c_out, c_shapes = _cast_plumbing(jobs, ROWS // tm, lambda i: i)
    y_spec = pl.BlockSpec((CTX_LEN, WIN_HEADS * HEAD), lambda i: (jnp.clip(i - 1, 0, last_ctx), 0))
    y_shape = jax.ShapeDtypeStruct((ROWS_CTX, WIN_HEADS * HEAD), jnp.bfloat16)
    out = pl.pallas_call(
        functools.partial(_attn_proj_kernel, first, layer, _cast_plan(jobs)),
        grid=(ROWS // tm,),
        in_specs=in_specs + c_in,
        out_specs=[pl.BlockSpec((tm, ATTN_COLS), lambda i: (jnp.maximum(i - N_CTX_SEQ, 0), 0))]
                  + [state_spec(idx) for idx in range(len(_STATE_COLS))] + [y_spec, y_spec] + c_out,
        out_shape=[jax.ShapeDtypeStruct((ROWS_LAT, ATTN_COLS), jnp.bfloat16)]
                  + [jax.ShapeDtypeStruct((N_CTX_SEQ, N_LAYERS) + _state_block(idx), jnp.float32)
                     for idx in range(len(_STATE_COLS))] + [y_shape, y_shape] + c_shapes,
        scratch_shapes=[pltpu.VMEM((CTX_LEN, ATTN_COLS), jnp.bfloat16)] * 2,
        input_output_aliases=aliases,
        compiler_params=_params(("arbitrary",)),
        name="attn_proj",
    )(*args, *c_args)
    return out[0], out[1:5], out[5], out[6], out[7:]


def _slab(ref, col0, idx):
    return ref[:, col0 + idx * HEAD:col0 + (idx + 1) * HEAD]


def _with_ones(v):
    return jnp.concatenate([v, jnp.ones_like(v)], axis=1)


def _pipelined(items, scores, finish):
    s = scores(items[0])
    for r, item in enumerate(items):
        s_next = scores(items[r + 1]) if r + 1 < len(items) else None
        finish(item, s)
        s = s_next


WIN_STAGE = WIN_GROUP
WIN_STAGES = tuple(range(0, WIN_HEADS, WIN_STAGE))


def _win_scores(q_ref, col0, h0, k, bias):
    q = jnp.concatenate([_slab(q_ref, col0, h0 + hh) for hh in range(WIN_STAGE)], axis=0)
    s = _dot_nt(q, k) * (HEAD ** -0.5)
    return s if bias is None else s + bias


def _win_finish(layer, h0, s, v, sink_ref, o_ref):
    rows = s.shape[0] // WIN_STAGE
    sink = jnp.concatenate(
        [jnp.full((rows, 1), sink_ref[layer * WIN_HEADS + h0 + hh], jnp.float32)
         for hh in range(WIN_STAGE)], axis=0)
    m = jnp.maximum(jnp.max(s, axis=-1, keepdims=True), sink)
    r = _dot(_bf(jnp.exp(s - m)), _with_ones(v))
    o = r[:, :HEAD] / (r[:, HEAD:HEAD + 1] + jnp.exp(sink - m))
    for hh in range(WIN_STAGE):
        head = h0 + hh
        o_ref[:, head * HEAD:(head + 1) * HEAD] = _bf(o[hh * rows:(hh + 1) * rows])


def _win_band(n, kseq_ref, vseq_ref):
    start = pl.multiple_of(jnp.clip((n - 1) * QBLK, 0, LAT_LEN - 3 * QBLK), QBLK)
    kb = kseq_ref[pl.ds(start, 3 * QBLK), :]
    vb = vseq_ref[pl.ds(start, 3 * QBLK), :]
    shape = (QBLK, PAST + 3 * QBLK)
    col = lax.broadcasted_iota(jnp.int32, shape, 1)
    qpos = n * QBLK + lax.broadcasted_iota(jnp.int32, shape, 0)
    kpos = start + col - PAST
    bias = jnp.where((col < PAST) | (jnp.abs(qpos - kpos) <= WINDOW), 0.0, NEG)
    return kb, vb, jnp.concatenate([bias] * WIN_STAGE, axis=0)


def _win_keys(cache_ref, band, h0):
    g = h0 // WIN_GROUP
    cached = cache_ref[pl.ds(g, PAST, stride=WIN_KV), :]
    return jnp.concatenate([_bf(cached), band[:, g * HEAD:(g + 1) * HEAD]], axis=0)


def _lam_init(layer):
    return 0.8 - 0.6 * math.exp(-0.3 * layer)


def _diff_lambda(layer, lam_ref):
    lp = lam_ref[...]
    return (jnp.exp(jnp.sum(lp[0:1] * lp[1:2], axis=-1, keepdims=True))
            - jnp.exp(jnp.sum(lp[2:3] * lp[3:4], axis=-1, keepdims=True)) + _lam_init(layer))


def _diff_scores(q, k):
    first = lax.broadcasted_iota(jnp.int32, q.shape, 1) < DIFF_QK
    zero = jnp.zeros_like(q)
    return _dot_nt(jnp.concatenate([jnp.where(first, q, zero), jnp.where(first, zero, q)], axis=0), k)


def _diff_finish(layer, s, v_ones, lam, g):
    tq = s.shape[0] // 2
    e = _bf(jnp.exp2(s - jnp.max(s, axis=-1, keepdims=True)))
    r = _dot(e, v_ones)
    o = (r[:tq, :HEAD] * (1.0 / r[:tq, HEAD:HEAD + 1])
         - r[tq:, :HEAD] * (lam / r[tq:, HEAD:HEAD + 1]))
    o = o * lax.rsqrt(jnp.mean(o * o, axis=-1, keepdims=True) + EPS) * g
    return _bf(o * (1.0 - _lam_init(layer)))


def _ctx_attention(layer, qkv_ref, sink_ref, lam_ref, g_ref, yb_ref, yc_ref):
    lam = _diff_lambda(layer, lam_ref)
    stages = [("win", h0) for h0 in WIN_STAGES] + [("diff", h) for h in range(DIFF_HEADS)]

    def scores(stage):
        kind, idx = stage
        if kind == "win":
            return _win_scores(qkv_ref, A_WQ, idx, _slab(qkv_ref, A_WK, idx // WIN_GROUP), None)
        return _diff_scores(_slab(qkv_ref, A_DQ, idx), _slab(qkv_ref, A_DK, idx))

    def finish(stage, s):
        kind, idx = stage
        if kind == "win":
            _win_finish(layer, idx, s, _slab(qkv_ref, A_WV, idx // WIN_GROUP), sink_ref, yb_ref)
        else:
            yc_ref[:, idx * HEAD:(idx + 1) * HEAD] = _diff_finish(
                layer, s, _with_ones(_slab(qkv_ref, A_DV, idx)), lam, g_ref[...])

    _pipelined(stages, scores, finish)


LAT_DIFF_SUB = 128
LAT_WIN_BLOCKS = LAT_LEN // QBLK // DIFF_HEADS


def _lat_attn_kernel(layer, plan, dq_ref, dkc_ref, dvc_ref, dk_ref, dv_ref, lam_ref, g_ref,
                     wq_ref, wkc_ref, wvc_ref, wk_ref, wv_ref, sink_ref, *refs):
    n_in = _n_cast_inputs(plan)
    yc_ref, yb_ref = refs[n_in:n_in + 2]
    kall_ref, vall_ref = refs[-2:]
    _run_casts(plan, refs[:n_in], refs[n_in + 2:-2])
    h = pl.program_id(1)

    head_rows = pl.ds(h, PAST, stride=DIFF_HEADS)
    kall_ref[0:PAST] = _bf(dkc_ref[head_rows, :])
    kall_ref[PAST:] = dk_ref[...]
    vall_ref[0:PAST] = _with_ones(_bf(dvc_ref[head_rows, :]))
    vall_ref[PAST:] = _with_ones(dv_ref[...])
    lam = _diff_lambda(layer, lam_ref)
    bands = [_win_band(h * LAT_WIN_BLOCKS + blk, wk_ref, wv_ref) for blk in range(LAT_WIN_BLOCKS)]

    diff_stages = [("diff", r0, None) for r0 in range(0, LAT_LEN, LAT_DIFF_SUB)]
    win_stages = [("win", blk, h0) for blk in range(LAT_WIN_BLOCKS) for h0 in WIN_STAGES]
    every = len(diff_stages) // len(win_stages)
    stages = []
    for k, stage in enumerate(diff_stages):
        stages.append(stage)
        if k % every == 0 and k // every < len(win_stages):
            stages.append(win_stages[k // every])

    def rows(ref, blk):
        return ref.at[pl.ds(blk * QBLK, QBLK)]

    def scores(stage):
        kind, a, h0 = stage
        if kind == "diff":
            return _diff_scores(dq_ref[a:a + LAT_DIFF_SUB], kall_ref[...])
        kb, _, bias = bands[a]
        return _win_scores(rows(wq_ref, a), 0, h0, _win_keys(wkc_ref, kb, h0), bias)

    def finish(stage, s):
        kind, a, h0 = stage
        if kind == "diff":
            yc_ref[a:a + LAT_DIFF_SUB] = _diff_finish(layer, s, vall_ref[...], lam, g_ref[...])
        else:
            _win_finish(layer, h0, s, _win_keys(wvc_ref, bands[a][1], h0), sink_ref, rows(yb_ref, a))

    _pipelined(stages, scores, finish)


def _lat_attn(qkv, win_ck, win_cv, diff_ck, diff_cv, sink, lam_params, subln, layer, jobs):
    kvw = WIN_KV * HEAD
    wrows = LAT_WIN_BLOCKS * QBLK

    def step(b, h):
        return b * DIFF_HEADS + h

    def cache_spec(heads):
        return pl.BlockSpec((None, None, PAST * heads, HEAD), lambda b, h: (b, layer, 0, 0))

    def head_spec(col0):
        return pl.BlockSpec((LAT_LEN, HEAD), lambda b, h: (b, col0 // HEAD + h))

    def seq_spec(col0):
        return pl.BlockSpec((LAT_LEN, kvw), lambda b, h: (b, col0 // kvw))

    c_in, c_args, c_out, c_shapes = _cast_plumbing(jobs, N_LAT_SEQ * DIFF_HEADS, step)
    out = pl.pallas_call(
        functools.partial(_lat_attn_kernel, layer, _cast_plan(jobs)),
        grid=(N_LAT_SEQ, DIFF_HEADS),
        in_specs=[head_spec(A_DQ), cache_spec(DIFF_HEADS), cache_spec(DIFF_HEADS),
                  head_spec(A_DK), head_spec(A_DV),
                  pl.BlockSpec((None, 4, DIFF_QK), lambda b, h: (layer, 0, 0)),
                  _vec_spec(layer, HEAD),
                  pl.BlockSpec((wrows, WIN_HEADS * HEAD), lambda b, h: (step(b, h), 0)),
                  cache_spec(WIN_KV), cache_spec(WIN_KV), seq_spec(A_WK), seq_spec(A_WV),
                  pl.BlockSpec(memory_space=pltpu.SMEM)] + c_in,
        out_specs=[pl.BlockSpec((LAT_LEN, HEAD), lambda b, h: (b, h)),
                   pl.BlockSpec((wrows, WIN_HEADS * HEAD), lambda b, h: (step(b, h), 0))] + c_out,
        out_shape=[jax.ShapeDtypeStruct((ROWS_LAT, DIFF_HEADS * HEAD), jnp.bfloat16),
                   jax.ShapeDtypeStruct((ROWS_LAT, WIN_HEADS * HEAD), jnp.bfloat16)] + c_shapes,
        scratch_shapes=[pltpu.VMEM((PAST + LAT_LEN, HEAD), jnp.bfloat16),
                        pltpu.VMEM((PAST + LAT_LEN, 2 * HEAD), jnp.bfloat16)],
        compiler_params=_params(("arbitrary", "arbitrary")),
        name="lat_attn",
    )(qkv, diff_ck, diff_cv, qkv, qkv, lam_params, subln, qkv, win_ck, win_cv, qkv, qkv, sink, *c_args)
    return out[1], out[0], out[2:]


MERGE_TM = 1024
MERGE_TN = 512


def _merge_kernel(h_ref, ya_ref, ybc_ref, ybl_ref, ycc_ref, ycl_ref,
                  wga_ref, wgb_ref, wgc_ref, wa_ref, wb_ref, wc_ref, o_ref):
    is_ctx = pl.program_id(0) < ROWS_CTX // MERGE_TM
    for r0 in range(0, MERGE_TM, MERGE_TM // 2):
        rows = slice(r0, r0 + MERGE_TM // 2)
        h = h_ref[rows]
        yb = jnp.where(is_ctx, ybc_ref[rows], ybl_ref[rows])
        yc = jnp.where(is_ctx, ycc_ref[rows], ycl_ref[rows])
        m = _sigmoid(_dot(h, wga_ref[...])) * _dot(ya_ref[rows], wa_ref[...])
        m += _sigmoid(_dot(h, wgb_ref[...])) * _dot(yb, wb_ref[...])
        m += _sigmoid(_dot(h, wgc_ref[...])) * _dot(yc, wc_ref[...])
        o_ref[rows] = _bf(m)


def _merge(h, ya, yb_ctx, yb_lat, yc_ctx, yc_lat, w_gate, wa, wb, wc):
    tm, tn = MERGE_TM, MERGE_TN
    nj = D // tn

    def gate_spec(branch):
        return pl.BlockSpec((D, tn), lambda i, j: (0, branch * nj + j))

    w_spec = pl.BlockSpec((CONV_DIM, tn), lambda i, j: (0, j))
    return pl.pallas_call(
        _merge_kernel,
        grid=(ROWS // tm, nj),
        in_specs=[pl.BlockSpec((tm, D), lambda i, j: (i, 0)),
                  pl.BlockSpec((tm, CONV_DIM), lambda i, j: (i, 0)),
                  *_split_specs(tm, WIN_HEADS * HEAD), *_split_specs(tm, DIFF_HEADS * HEAD),
                  gate_spec(0), gate_spec(1), gate_spec(2), w_spec, w_spec, w_spec],
        out_specs=pl.BlockSpec((tm, tn), lambda i, j: (i, j)),
        out_shape=jax.ShapeDtypeStruct((ROWS, D), jnp.bfloat16),
        compiler_params=_params(("parallel", "arbitrary")),
        name="merge",
    )(h, ya, yb_ctx, yb_lat, yc_ctx, yc_lat, w_gate, w_gate, w_gate, wa, wb, wc)


OUT_TM = 512


def _out_proj_kernel(n_x, *refs):
    x_refs = refs[:n_x]
    m_ref, w_ref, gate_ref, g_ref, shift_ref, scale_ref, xo_ref, h_ref = refs[n_x:]
    x = x_refs[0][...] if n_x == 1 else _pick_rows(OUT_TM, *x_refs)
    x = x + gate_ref[...] * _dot(m_ref[...], w_ref[...])
    xo_ref[...] = x
    h_ref[...] = _bf(_rms_mod(x, g_ref[...], scale_ref[...], shift_ref[...]))


def _out_proj(xs, merged, w_out, mods, norm_mlp, layer):
    tm = OUT_TM
    row = pl.BlockSpec((tm, D), lambda i: (i, 0))
    x_specs = [row] if len(xs) == 1 else list(_split_specs(tm, D))
    return pl.pallas_call(
        functools.partial(_out_proj_kernel, len(xs)),
        grid=(ROWS // tm,),
        in_specs=x_specs + [row,
                            pl.BlockSpec((D, D), lambda i: (0, 0)),
                            _mod_spec(layer, 2, tm), _vec_spec(layer, D),
                            _mod_spec(layer, 3, tm), _mod_spec(layer, 4, tm)],
        out_specs=[row, row],
        out_shape=[jax.ShapeDtypeStruct((ROWS, D), jnp.float32),
                   jax.ShapeDtypeStruct((ROWS, D), jnp.bfloat16)],
        compiler_params=_params(("arbitrary",)),
        name="out_proj",
    )(*xs, merged, w_out, mods, norm_mlp.reshape(N_LAYERS, 1, D), mods, mods)


MLP_TM = 512
MLP_TF = 1024


MLP_TILES = ROWS // MLP_TM


def _mlp_kernel(final, h_ref, w1_ref, w2_ref, x_ref, gate_ref, g_ref, shift_ref, scale_ref,
                *rest):
    outs, accs = rest[:-2], rest[-2:]
    i = pl.program_id(0)
    j = pl.program_id(1)

    def matmuls(acc_ref, first_chunk):
        t = jnp.maximum(_dot(h_ref[...], w1_ref[...]), 0.0)
        upd = _dot(_bf(t * t), w2_ref[...])
        acc_ref[...] = upd if first_chunk else acc_ref[...] + upd

    def epilogue(acc_ref):
        x = x_ref[...] + gate_ref[...] * acc_ref[...]
        if final:
            yc_ref, yl_ref = outs
            y = x * lax.rsqrt(jnp.mean(x * x, axis=-1, keepdims=True) + EPS) * g_ref[...]

            @pl.when(i - 1 < ROWS_CTX // MLP_TM)
            def _():
                yc_ref[...] = y

            @pl.when(i - 1 >= ROWS_CTX // MLP_TM)
            def _():
                yl_ref[...] = y
        else:
            xo_ref, hn_ref = outs
            xo_ref[...] = x
            hn_ref[...] = _bf(_rms_mod(x, g_ref[...], scale_ref[...], shift_ref[...]))

    def step(parity, first_chunk, finish_previous):
        def run():
            matmuls(accs[parity], first_chunk)
            if finish_previous:
                epilogue(accs[1 - parity])
        return run

    has_tile = i < MLP_TILES
    pl.when((i == 0) & (j == 0))(step(0, True, False))
    for parity in (0, 1):
        mine = has_tile & (i % 2 == parity)
        pl.when(mine & (i > 0) & (j == 0))(step(parity, True, True))
        pl.when(mine & (j > 0))(step(parity, False, False))
    pl.when((i == MLP_TILES) & (j == 0))(lambda: epilogue(accs[(MLP_TILES - 1) % 2]))


def _mlp(h2, x, w1, w2, mods, norm_next, layer, final):
    tm, tf = MLP_TM, MLP_TF
    last = MLP_TILES - 1
    n_ctx = ROWS_CTX // tm

    def done(i):
        return jnp.clip(i - 1, 0, last)

    def chunk(i, j):
        return jnp.where(i == MLP_TILES, 0, j)

    lagged = pl.BlockSpec((tm, D), lambda i, j: (done(i), 0))
    nxt = layer if final else layer + 1
    g_spec = pl.BlockSpec((None, 1, D), lambda i, j: (0 if final else nxt, 0, 0))
    if final:
        out_specs = [pl.BlockSpec((tm, D), lambda i, j: (jnp.minimum(done(i), n_ctx - 1), 0)),
                     pl.BlockSpec((tm, D), lambda i, j: (jnp.maximum(done(i) - n_ctx, 0), 0))]
        out_shape = [jax.ShapeDtypeStruct((ROWS_CTX, D), jnp.float32),
                     jax.ShapeDtypeStruct((ROWS_LAT, D), jnp.float32)]
    else:
        out_specs = [lagged, lagged]
        out_shape = [jax.ShapeDtypeStruct((ROWS, D), jnp.float32),
                     jax.ShapeDtypeStruct((ROWS, D), jnp.bfloat16)]
    return pl.pallas_call(
        functools.partial(_mlp_kernel, final),
        grid=(MLP_TILES + 1, D_FF // tf),
        in_specs=[pl.BlockSpec((tm, D), lambda i, j: (jnp.minimum(i, last), 0)),
                  pl.BlockSpec((D, tf), lambda i, j: (0, chunk(i, j))),
                  pl.BlockSpec((tf, D), lambda i, j: (chunk(i, j), 0)),
                  lagged, _mod_spec(layer, 5, tm, lag=1), g_spec,
                  _mod_spec(nxt, 0, tm, lag=1), _mod_spec(nxt, 1, tm, lag=1)],
        out_specs=out_specs,
        out_shape=out_shape,
        scratch_shapes=[pltpu.VMEM((tm, D), jnp.float32)] * 2,
        compiler_params=_params(("arbitrary", "arbitrary")),
        name="mlp_final" if final else "mlp",
    )(h2, w1, w2, x, mods, norm_next, mods, mods)


def kernel(x_prompt, x_sample, cache_win_k, cache_win_v, cache_diff_k, cache_diff_v, c, c_ctx, w_ada, b_ada, norm_mix, norm_mlp, w_in, conv_w, win_sink, lambda_q1, lambda_k1, lambda_q2, lambda_k2, diff_subln, w_branch_conv, w_branch_win, w_branch_diff, w_out, w_mlp1, w_mlp2, norm_final):
    xs = (x_prompt.reshape(ROWS_CTX, D), x_sample.reshape(ROWS_LAT, D))
    cvec = jnp.concatenate([c_ctx[None], c, jnp.zeros((N_MOD_ROWS - 1 - N_LAT_SEQ, D), jnp.float32)], axis=0)
    mods = _modulation(cvec, w_ada, b_ada).reshape(N_LAYERS * N_MOD_ROWS * N_MOD, 1, D)

    tables = _rope_tables()
    sink = win_sink.reshape(N_LAYERS * WIN_HEADS)
    lam_params = jnp.stack([lambda_q1, lambda_k1, lambda_q2, lambda_k2], axis=1)
    subln = diff_subln.reshape(N_LAYERS, 1, HEAD)
    ck_win = cache_win_k.reshape(N_LAT_SEQ, N_LAYERS, PAST * WIN_KV, HEAD)
    cv_win = cache_win_v.reshape(N_LAT_SEQ, N_LAYERS, PAST * WIN_KV, HEAD)
    ck_diff = cache_diff_k.reshape(N_LAT_SEQ, N_LAYERS, PAST * DIFF_HEADS, HEAD)
    cv_diff = cache_diff_v.reshape(N_LAT_SEQ, N_LAYERS, PAST * DIFF_HEADS, HEAD)
    norm_final_3d = norm_final.reshape(1, 1, D)
    norm_mix_3d = norm_mix.reshape(N_LAYERS, 1, D)

    h = _prologue(*xs, mods, norm_mix, 0)
    states = None
    for layer in range(N_LAYERS):
        in_job = functools.partial(_CastJob, w_in, layer, cbw=IN_CBW)
        ya, (w_attn, wa, wb, wc, w_o) = _conv_branch(
            h, w_in, conv_w, layer,
            [in_job(k0=OFF_ATTN // IN_CBW, nb=ATTN_COLS // IN_CBW)]
            + [_whole(w, layer) for w in (w_branch_conv, w_branch_win, w_branch_diff, w_out)])
        qkv, states, yb_ctx, yc_ctx, (w_gate,) = _attn_proj(
            h, w_attn, tables, sink, lam_params, subln, layer, states,
            [in_job(k0=OFF_GATE // IN_CBW, nb=3 * D // IN_CBW)])
        yb_lat, yc_lat, (w1, w2) = _lat_attn(qkv, ck_win, cv_win, ck_diff, cv_diff, sink, lam_params, subln,
                                             layer, [_whole(w_mlp1, layer), _whole(w_mlp2, layer)])
        merged = _merge(h, ya, yb_ctx, yb_lat, yc_ctx, yc_lat, w_gate, wa, wb, wc)
        x, h2 = _out_proj(xs, merged, w_o, mods, norm_mlp, layer)
        if layer == N_LAYERS - 1:
            y_ctx, y_lat = _mlp(h2, x, w1, w2, mods, norm_final_3d, layer, True)
        else:
            x, h = _mlp(h2, x, w1, w2, mods, norm_mix_3d, layer, False)
            xs = (x,)

    new_win_k, new_win_v, new_diff_k, new_diff_v = states
    return (y_ctx.reshape(N_CTX_SEQ, CTX_LEN, D), y_lat.reshape(N_LAT_SEQ, LAT_LEN, D),
            new_win_k.reshape(N_CTX_SEQ, N_LAYERS, CTX_LEN, WIN_KV, HEAD),
            new_win_v.reshape(N_CTX_SEQ, N_LAYERS, CTX_LEN, WIN_KV, HEAD),
            new_diff_k.reshape(N_CTX_SEQ, N_LAYERS, CTX_LEN, DIFF_HEADS, HEAD),
            new_diff_v.reshape(N_CTX_SEQ, N_LAYERS, CTX_LEN, DIFF_HEADS, HEAD))
```
